```python
import math
import jax, jax.numpy as jnp
from jax import lax
import numpy as np

D_MODEL = 1024
BATCH = 8
SEQ = 2048
DEPTH = 1

N_HEADS = 8
HEAD_DIM = 64
ATTN_WIDTH = N_HEADS * HEAD_DIM
DILATED_PATTERNS = ((128, 1), (512, 4), (2048, 16))
BAND_BLOCK = 128
ROPE_THETA = 10000.0
GMLP_WIDTH = 1024
GMLP_GROUPS = 8
GMLP_GROUP_DIM = GMLP_WIDTH // GMLP_GROUPS
GMLP_CHUNK = 128
IN_WIDTH = 3 * ATTN_WIDTH + 2 * GMLP_WIDTH + 2 * D_MODEL
N_EXPERTS = 256
TOP_K = 8
EXPERT_DIM = 256
SHARED_DIM = 256
N_GROUPS = 8
TOPK_GROUPS = 4
ROUTED_SCALE = 2.5
MOE_BLOCK = 128
DEEPNORM_ALPHA = (2.0 * DEPTH) ** 0.25
DEEPNORM_BETA = (8.0 * DEPTH) ** -0.25
LN_EPS = 1e-5

kernel_name = 'hybrid_dilated_gmlp_moe_block'


def layer_norm(x, g, b):
    xf = x.astype(jnp.float32)
    mu = xf.mean(-1, keepdims=True)
    var = jnp.square(xf - mu).mean(-1, keepdims=True)
    y = (xf - mu) * lax.rsqrt(var + LN_EPS)
    return (y * g.astype(jnp.float32) + b.astype(jnp.float32)).astype(x.dtype)


def rotary(t, pos):
    half = HEAD_DIM // 2
    inv_freq = ROPE_THETA ** (-jnp.arange(half, dtype=jnp.float32) / half)
    ang = pos.astype(jnp.float32)[:, None] * inv_freq[None, :]
    cos = jnp.cos(ang)[None, :, None, :]
    sin = jnp.sin(ang)[None, :, None, :]
    t1 = t[..., :half].astype(jnp.float32)
    t2 = t[..., half:].astype(jnp.float32)
    out = jnp.concatenate([t1 * cos - t2 * sin, t2 * cos + t1 * sin], axis=-1)
    return out.astype(t.dtype)


def dilated_band_attention(q, k, v, window, dilation):
    B, S, H, E = q.shape
    n = S // dilation
    span = window // dilation
    nb = -(-n // BAND_BLOCK)
    n_pad = nb * BAND_BLOCK

    def to_blocks(t):
        t = t.reshape(B, n, dilation, H, E).transpose(0, 2, 3, 1, 4)
        t = jnp.pad(t, ((0, 0), (0, 0), (0, 0), (0, n_pad - n), (0, 0)))
        return t.reshape(B, dilation, H, nb, BAND_BLOCK, E)

    def with_prev(t):
        prev = jnp.pad(t, ((0, 0), (0, 0), (0, 0), (1, 0), (0, 0), (0, 0)))[:, :, :, :-1]
        return jnp.concatenate([prev, t], axis=4)

    qb = to_blocks(q)
    kc = with_prev(to_blocks(k))
    vc = with_prev(to_blocks(v))
    s = jnp.einsum('brhnqe,brhnke->brhnqk', qb, kc).astype(jnp.float32) * (HEAD_DIM ** -0.5)
    qi = jnp.arange(BAND_BLOCK)[:, None]
    kj = jnp.arange(2 * BAND_BLOCK)[None, :]
    dist = qi + BAND_BLOCK - kj
    key_idx = jnp.arange(nb)[:, None, None] * BAND_BLOCK - BAND_BLOCK + kj[None]
    valid = (dist >= 0) & (dist <= span) & (key_idx >= 0)
    s = jnp.where(valid, s, -jnp.inf)
    m = s.max(-1, keepdims=True)
    p = jnp.exp(s - m)
    l = p.sum(-1, keepdims=True)
    o = jnp.einsum('brhnqk,brhnke->brhnqe', (p / l).astype(v.dtype), vc)
    lse = (m + jnp.log(l))[..., 0]
    o = o.reshape(B, dilation, H, n_pad, E)[:, :, :, :n].transpose(0, 3, 1, 2, 4).reshape(B, S, H, E)
    lse = lse.reshape(B, dilation, H, n_pad)[..., :n].transpose(0, 3, 1, 2).reshape(B, S, H)
    return o, lse


def spatial_gating(uv, ln_g, ln_b, w_s, b_s):
    B, S, _ = uv.shape
    z = jax.nn.gelu(uv, approximate=False)
    u, vv = z[..., :GMLP_WIDTH], z[..., GMLP_WIDTH:]
    vv = layer_norm(vv, ln_g, ln_b)
    nc = S // GMLP_CHUNK
    vv = vv.reshape(B, nc, GMLP_CHUNK, GMLP_GROUPS, GMLP_GROUP_DIM)
    causal = jnp.tril(jnp.ones((GMLP_CHUNK, GMLP_CHUNK), dtype=bool))
    ws = jnp.where(causal[None], w_s, jnp.zeros_like(w_s))
    sv = jnp.einsum('gts,bcsgk->bctgk', ws, vv) + b_s.T[None, None, :, :, None]
    return u * sv.reshape(B, S, GMLP_WIDTH)


def token_mixer(h, pos, w_in, ln_v_g, ln_v_b, w_spatial, b_spatial, w_proj_a, w_proj_b, w_out):
    B, S, _ = h.shape
    proj = h @ w_in
    q, k, v, uv, gate_pre = jnp.split(
        proj, [ATTN_WIDTH, 2 * ATTN_WIDTH, 3 * ATTN_WIDTH, 3 * ATTN_WIDTH + 2 * GMLP_WIDTH], axis=-1)
    q = rotary(q.reshape(B, S, N_HEADS, HEAD_DIM), pos)
    k = rotary(k.reshape(B, S, N_HEADS, HEAD_DIM), pos)
    v = v.reshape(B, S, N_HEADS, HEAD_DIM)
    outs = []
    lses = []
    for window, dilation in DILATED_PATTERNS:
        o_p, lse_p = dilated_band_attention(q, k, v, window, dilation)
        outs.append(o_p)
        lses.append(lse_p)
    wts = jax.nn.softmax(jnp.stack(lses, axis=0), axis=0)
    o_a = jnp.einsum('pbsh,pbshe->bshe', wts.astype(v.dtype), jnp.stack(outs, axis=0))
    y_a = o_a.reshape(B, S, ATTN_WIDTH) @ w_proj_a
    y_b = spatial_gating(uv, ln_v_g, ln_v_b, w_spatial, b_spatial) @ w_proj_b
    gates = jax.nn.sigmoid(gate_pre)
    g_a, g_b = gates[..., :D_MODEL], gates[..., D_MODEL:]
    return (g_a * y_a + g_b * y_b) @ w_out


def moe_ffn(h, w_router, router_bias, w_gate_e, w_up_e, w_down_e, w_gate_sh, w_up_sh, w_down_sh):
    B, S, D = h.shape
    T = B * S
    hf = h.reshape(T, D)
    scores = jax.nn.sigmoid((hf @ w_router).astype(jnp.float32))
    biased = scores + router_bias.astype(jnp.float32)
    per_group = N_EXPERTS // N_GROUPS
    grp_score = lax.top_k(biased.reshape(T, N_GROUPS, per_group), 2)[0].sum(-1)
    _, top_groups = lax.top_k(grp_score, TOPK_GROUPS)
    gmask = jax.nn.one_hot(top_groups, N_GROUPS, dtype=jnp.float32).sum(1) > 0
    emask = jnp.repeat(gmask, per_group, axis=1)
    _, idx = lax.top_k(jnp.where(emask, biased, -jnp.inf), TOP_K)
    w = jnp.take_along_axis(scores, idx, axis=1)
    w = ROUTED_SCALE * w / w.sum(-1, keepdims=True)
    TK = T * TOP_K
    flat_e = idx.reshape(TK)
    flat_tok = jnp.repeat(jnp.arange(T, dtype=jnp.int32), TOP_K)
    flat_w = w.reshape(TK)
    order = jnp.argsort(flat_e)
    sorted_e = flat_e[order]
    counts = jnp.bincount(flat_e, length=N_EXPERTS)
    start = jnp.cumsum(counts) - counts
    padded = ((counts + MOE_BLOCK - 1) // MOE_BLOCK) * MOE_BLOCK
    padded_end = jnp.cumsum(padded)
    padded_start = padded_end - padded
    dest = padded_start[sorted_e] + jnp.arange(TK) - start[sorted_e]
    n_blocks = -(-TK // MOE_BLOCK) + N_EXPERTS
    P = n_blocks * MOE_BLOCK
    row_tok = jnp.zeros((P,), jnp.int32).at[dest].set(flat_tok[order])
    row_w = jnp.zeros((P,), h.dtype).at[dest].set(flat_w[order].astype(h.dtype))
    block_e = jnp.minimum(
        jnp.searchsorted(padded_end, jnp.arange(n_blocks) * MOE_BLOCK, side='right'), N_EXPERTS - 1)

    def expert_block(args):
        tok, wt, e = args
        xb = hf[tok]
        a = xb @ w_gate_e[e]
        b = xb @ w_up_e[e]
        return ((jax.nn.silu(a) * b) @ w_down_e[e]) * wt[:, None]

    y = lax.map(expert_block, (row_tok.reshape(n_blocks, MOE_BLOCK),
                               row_w.reshape(n_blocks, MOE_BLOCK), block_e))
    routed = jax.ops.segment_sum(y.reshape(P, D), row_tok, num_segments=T)
    shared = (jax.nn.silu(hf @ w_gate_sh) * (hf @ w_up_sh)) @ w_down_sh
    return (routed + shared).reshape(B, S, D)


def setup_inputs(seed: int = 0) -> dict:
    key = jax.random.key(seed)
    ks = jax.random.split(key, 24)
    L = DEPTH
    D = D_MODEL

    def nrm(k, shape, scale):
        return jax.random.normal(k, shape, jnp.float32) * scale

    return {
        'x': nrm(ks[0], (BATCH, SEQ, D), 1.0),
        'c': nrm(ks[1], (BATCH, D), 1.0),
        'w_ada': nrm(ks[2], (L, D, 6 * D), 0.5 * D ** -0.5),
        'b_ada': nrm(ks[3], (L, 6 * D), 0.02),
        'w_in': nrm(ks[4], (L, D, IN_WIDTH), D ** -0.5),
        'ln_v_g': 1.0 + nrm(ks[5], (L, GMLP_WIDTH), 0.1),
        'ln_v_b': nrm(ks[6], (L, GMLP_WIDTH), 0.02),
        'w_spatial': nrm(ks[7], (L, GMLP_GROUPS, GMLP_CHUNK, GMLP_CHUNK), GMLP_CHUNK ** -0.5),
        'b_spatial': 1.0 + nrm(ks[8], (L, GMLP_GROUPS, GMLP_CHUNK), 0.1),
        'w_proj_a': nrm(ks[9], (L, ATTN_WIDTH, D), ATTN_WIDTH ** -0.5),
        'w_proj_b': nrm(ks[10], (L, GMLP_WIDTH, D), GMLP_WIDTH ** -0.5),
        'w_out': nrm(ks[11], (L, D, D), DEEPNORM_BETA * D ** -0.5),
        'ln1_g': 1.0 + nrm(ks[12], (L, D), 0.1),
        'ln1_b': nrm(ks[13], (L, D), 0.02),
        'w_router': nrm(ks[14], (L, D, N_EXPERTS), D ** -0.5),
        'router_bias': nrm(ks[15], (L, N_EXPERTS), 0.01),
        'w_gate_e': nrm(ks[16], (L, N_EXPERTS, D, EXPERT_DIM), D ** -0.5),
        'w_up_e': nrm(ks[17], (L, N_EXPERTS, D, EXPERT_DIM), D ** -0.5),
        'w_down_e': nrm(ks[18], (L, N_EXPERTS, EXPERT_DIM, D), DEEPNORM_BETA * EXPERT_DIM ** -0.5),
        'w_gate_sh': nrm(ks[19], (L, D, SHARED_DIM), D ** -0.5),
        'w_up_sh': nrm(ks[20], (L, D, SHARED_DIM), D ** -0.5),
        'w_down_sh': nrm(ks[21], (L, SHARED_DIM, D), DEEPNORM_BETA * SHARED_DIM ** -0.5),
        'ln2_g': 1.0 + nrm(ks[22], (L, D), 0.1),
        'ln2_b': nrm(ks[23], (L, D), 0.02),
    }


def reference(x, c, w_ada, b_ada, w_in, ln_v_g, ln_v_b, w_spatial, b_spatial, w_proj_a, w_proj_b,
              w_out, ln1_g, ln1_b, w_router, router_bias, w_gate_e, w_up_e, w_down_e,
              w_gate_sh, w_up_sh, w_down_sh, ln2_g, ln2_b):
    S = x.shape[1]
    pos = jnp.arange(S, dtype=jnp.int32)
    for l in range(DEPTH):
        mod = jax.nn.silu(c) @ w_ada[l] + b_ada[l]
        sh_a, sc_a, g_a, sh_f, sc_f, g_f = jnp.split(mod[:, None, :], 6, axis=-1)
        h = x * (1.0 + sc_a) + sh_a
        mix = token_mixer(h, pos, w_in[l], ln_v_g[l], ln_v_b[l], w_spatial[l], b_spatial[l],
                          w_proj_a[l], w_proj_b[l], w_out[l])
        x = layer_norm(DEEPNORM_ALPHA * x + g_a * mix, ln1_g[l], ln1_b[l])
        h = x * (1.0 + sc_f) + sh_f
        ffn = moe_ffn(h, w_router[l], router_bias[l], w_gate_e[l], w_up_e[l], w_down_e[l],
                      w_gate_sh[l], w_up_sh[l], w_down_sh[l])
        x = layer_norm(DEEPNORM_ALPHA * x + g_f * ffn, ln2_g[l], ln2_b[l])
    return x
```

```python
import functools
import math

import jax
import jax.numpy as jnp
from jax import lax
from jax.experimental import pallas as pl
from jax.experimental.pallas import tpu as pltpu

F32 = jnp.float32
BF16 = jnp.bfloat16

N_HEADS = 8
HEAD_DIM = 64
DILATIONS = (1, 4, 16)
BAND = 128
ROPE_THETA = 10000.0
GMLP_GROUPS = 8
GMLP_CHUNK = 128
N_EXPERT_GROUPS = 8
TOPK_GROUPS = 4
TOP_K = 8
ROUTED_SCALE = 2.5
LN_EPS = 1e-5
LANES = 128
SUBLANES = 8
VMEM_LIMIT = 56 * 1024 * 1024

TM_IN = 512
TM_POST = 256
TM_ROW = 128
BM = 256


def _cparams(sem):
    return pltpu.CompilerParams(dimension_semantics=sem, vmem_limit_bytes=VMEM_LIMIT)


def _const_spec(shape):
    nd = len(shape)
    return pl.BlockSpec(shape, lambda *_: (0,) * nd)


def _layer_norm(r, g, b):
    mu = jnp.mean(r, axis=-1, keepdims=True)
    d = r - mu
    var = jnp.mean(d * d, axis=-1, keepdims=True)
    return d * lax.rsqrt(var + LN_EPS) * g + b


def _gelu(t):
    return 0.5 * t * (1.0 + lax.erf(t * (1.0 / math.sqrt(2.0))))


def _sigmoid(t):
    return 1.0 / (1.0 + jnp.exp(-t))


def _ada_kernel(c_ref, w_ref, b_ref, o_ref):
    c = c_ref[...]
    o_ref[...] = jnp.dot(c * _sigmoid(c), w_ref[...], preferred_element_type=F32) + b_ref[...]


def _ada(c, w, b):
    bsz, d = c.shape
    n = w.shape[1]
    tn = 1024
    return pl.pallas_call(
        _ada_kernel,
        grid=(n // tn,),
        in_specs=[_const_spec((bsz, d)), pl.BlockSpec((d, tn), lambda j: (0, j)),
                  pl.BlockSpec((1, tn), lambda j: (0, j))],
        out_specs=pl.BlockSpec((bsz, tn), lambda j: (0, j)),
        out_shape=jax.ShapeDtypeStruct((bsz, n), F32),
        compiler_params=_cparams(("arbitrary",)),
        name="ada",
    )(c, w, b.reshape(1, n))


def _inproj_kernel(x_ref, mod_ref, cos_ref, sin_ref, wq_ref, wk_ref, wv_ref, wu_ref, wvg_ref,
                   wga_ref, wgb_ref, lng_ref, lnb_ref, wsp_ref, bsp_ref, wpb_ref,
                   q_ref, k_ref, v_ref, ga_ref, gbyb_ref, u_scr, vv_scr, gated_scr):
    tm = x_ref.shape[0]
    shift = mod_ref[0, 0:1, :]
    scale = mod_ref[0, 1:2, :]
    h = (x_ref[...] * (1.0 + scale) + shift).astype(BF16)

    cos = cos_ref[...]
    sin = sin_ref[...]
    aw = cos.shape[1]
    lane = lax.broadcasted_iota(jnp.int32, (tm, aw), 1)
    first_half = (lane % HEAD_DIM) < (HEAD_DIM // 2)

    def rope(t):
        partner = jnp.where(first_half, pltpu.roll(t, aw - HEAD_DIM // 2, 1),
                            pltpu.roll(t, HEAD_DIM // 2, 1))
        return t * cos + partner * sin

    q = jnp.dot(h, wq_ref[...], preferred_element_type=F32)
    q_ref[...] = (rope(q) * (HEAD_DIM ** -0.5)).astype(BF16)
    k = jnp.dot(h, wk_ref[...], preferred_element_type=F32)
    k_ref[...] = rope(k).astype(BF16)
    v_ref[...] = jnp.dot(h, wv_ref[...], preferred_element_type=F32).astype(BF16)

    u_scr[...] = _gelu(jnp.dot(h, wu_ref[...], preferred_element_type=F32))
    vg = _gelu(jnp.dot(h, wvg_ref[...], preferred_element_type=F32))
    vv_scr[...] = _layer_norm(vg, lng_ref[...], lnb_ref[...]).astype(BF16)

    ti = lax.broadcasted_iota(jnp.int32, (GMLP_CHUNK, GMLP_CHUNK), 0)
    si = lax.broadcasted_iota(jnp.int32, (GMLP_CHUNK, GMLP_CHUNK), 1)
    causal = si <= ti
    gw = u_scr.shape[1] // GMLP_GROUPS
    for g in range(GMLP_GROUPS):
        ws = jnp.where(causal, wsp_ref[g], 0.0).astype(BF16)
        cols = slice(g * gw, (g + 1) * gw)
        for c in range(tm // GMLP_CHUNK):
            rows = slice(c * GMLP_CHUNK, (c + 1) * GMLP_CHUNK)
            sv = jnp.dot(ws, vv_scr[rows, cols], preferred_element_type=F32) + bsp_ref[:, cols]
            gated_scr[rows, cols] = (u_scr[rows, cols] * sv).astype(BF16)

    y_b = jnp.dot(gated_scr[...], wpb_ref[...], preferred_element_type=F32)
    g_b = _sigmoid(jnp.dot(h, wgb_ref[...], preferred_element_type=F32))
    gbyb_ref[...] = (g_b * y_b).astype(BF16)
    ga_ref[...] = _sigmoid(jnp.dot(h, wga_ref[...], preferred_element_type=F32)).astype(BF16)


def _inproj(xf, mod, cos, sin, wq, wk, wv, wu, wvg, wga, wgb, lng, lnb, wsp, bsp, wpb, seq):
    t, d = xf.shape
    aw = wq.shape[1]
    gwid = wu.shape[1]
    tm = TM_IN
    per_seq = seq // tm
    row = lambda i: (i, 0)
    tab = lambda i: (i % per_seq, 0)
    return pl.pallas_call(
        _inproj_kernel,
        grid=(t // tm,),
        in_specs=[pl.BlockSpec((tm, d), row),
                  pl.BlockSpec((1,) + mod.shape[1:], lambda i: (i // per_seq, 0, 0)),
                  pl.BlockSpec((tm, aw), tab), pl.BlockSpec((tm, aw), tab),
                  _const_spec(wq.shape), _const_spec(wk.shape), _const_spec(wv.shape),
                  _const_spec(wu.shape), _const_spec(wvg.shape), _const_spec(wga.shape),
                  _const_spec(wgb.shape), _const_spec(lng.shape), _const_spec(lnb.shape),
                  _const_spec(wsp.shape), _const_spec(bsp.shape), _const_spec(wpb.shape)],
        out_specs=[pl.BlockSpec((tm, aw), row)] * 3 + [pl.BlockSpec((tm, d), row)] * 2,
        out_shape=[jax.ShapeDtypeStruct((t, aw), BF16)] * 3 + [jax.ShapeDtypeStruct((t, d), BF16)] * 2,
        scratch_shapes=[pltpu.VMEM((tm, gwid), F32), pltpu.VMEM((tm, gwid), BF16),
                        pltpu.VMEM((tm, gwid), BF16)],
        compiler_params=_cparams(("arbitrary",)),
        name="inproj",
    )(xf, mod, cos, sin, wq, wk, wv, wu, wvg, wga, wgb, lng, lnb, wsp, bsp, wpb)


def _rows(start, size, stride):
    if stride == 1:
        return pl.ds(start, size)
    return pl.ds(start, size, stride=stride)


def _attn_kernel(q_ref, k_ref, v_ref, o_ref, qf, kf, vf, acc_scr, m_scr, l_scr):
    seq = qf.shape[0]
    qf[...] = q_ref[0].astype(F32)
    kf[...] = k_ref[0].astype(F32)
    vf[...] = v_ref[0].astype(F32)

    lane = lax.broadcasted_iota(jnp.int32, (BAND, LANES), 1)
    head0 = lane < HEAD_DIM
    neg = jnp.float32(-1e30)

    def band_mask(nk):
        qi = lax.broadcasted_iota(jnp.int32, (BAND, nk), 0)
        kj = lax.broadcasted_iota(jnp.int32, (BAND, nk), 1)
        if nk == BAND:
            return kj <= qi
        return (kj >= qi) & (kj <= qi + BAND)

    def attend(p, q_start, k_start, nk, stride):
        qrows = _rows(q_start, BAND, stride)
        krows = _rows(k_start, nk, stride)
        qb = qf[qrows, :]
        kb = kf[krows, :].astype(BF16)
        vb = vf[krows, :].astype(BF16)
        mask = band_mask(nk)
        parts = []
        for hd in range(2):
            qh = jnp.where(head0 if hd == 0 else ~head0, qb, 0.0).astype(BF16)
            s = lax.dot_general(qh, kb, (((1,), (1,)), ((), ())), preferred_element_type=F32)
            s = jnp.where(mask, s, neg)
            m = jnp.max(s, axis=-1, keepdims=True)
            pr = jnp.exp(s - m)
            l = jnp.sum(pr, axis=-1, keepdims=True)
            acc = jnp.dot(pr.astype(BF16), vb, preferred_element_type=F32)
            parts.append((acc, m, l))
        acc_scr[p, qrows, :] = jnp.where(head0, parts[0][0], parts[1][0])
        m_scr[p, qrows, :] = jnp.where(head0, parts[0][1], parts[1][1])
        l_scr[p, qrows, :] = jnp.where(head0, parts[0][2], parts[1][2])

    for p, dil in enumerate(DILATIONS):
        span = dil * BAND
        nb = seq // span

        def first_block(r, c, p=p, dil=dil):
            start = r if dil > 1 else pl.multiple_of(r * BAND, BAND)
            attend(p, start, start, BAND, dil)
            return c

        lax.fori_loop(0, dil, first_block, 0)

        if nb > 1:
            def later_block(idx, c, p=p, dil=dil, span=span):
                r = idx % dil
                blk = idx // dil + 1
                q_start = r + blk * span
                if dil == 1:
                    q_start = pl.multiple_of(q_start, BAND)
                attend(p, q_start, q_start - span, 2 * BAND, dil)
                return c

            lax.fori_loop(0, dil * (nb - 1), later_block, 0)

    m_all = jnp.maximum(jnp.maximum(m_scr[0], m_scr[1]), m_scr[2])
    num = jnp.zeros(m_all.shape, F32)
    den = jnp.zeros(m_all.shape, F32)
    for p in range(len(DILATIONS)):
        w = jnp.exp(m_scr[p] - m_all)
        num = num + w * acc_scr[p]
        den = den + w * l_scr[p]
    o_ref[0] = (num / den).astype(BF16)


def _attention(q, k, v):
    bsz, seq, aw = q.shape
    spec = pl.BlockSpec((1, seq, LANES), lambda b, hp: (b, 0, hp))
    npat = len(DILATIONS)
    return pl.pallas_call(
        _attn_kernel,
        grid=(bsz, aw // LANES),
        in_specs=[spec, spec, spec],
        out_specs=spec,
        out_shape=jax.ShapeDtypeStruct((bsz, seq, aw), BF16),
        scratch_shapes=[pltpu.VMEM((seq, LANES), F32)] * 3 + [pltpu.VMEM((npat, seq, LANES), F32)] * 3,
        compiler_params=_cparams(("arbitrary", "arbitrary")),
        name="attn",
    )(q, k, v)


def _post_kernel(alpha, oa_ref, ga_ref, gbyb_ref, x_ref, mod_ref, wpa_ref, wo_ref, l1g_ref, l1b_ref,
                 wrt_ref, rb_ref, wgs_ref, wus_ref, wds_ref, tri_ref,
                 h2_ref, base_ref, e_ref, w_ref, r_ref, cnt_ref):
    tm = x_ref.shape[0]
    n_exp = wrt_ref.shape[0]
    per_group = n_exp // N_EXPERT_GROUPS
    gate_a = mod_ref[0, 2:3, :]
    shift_f = mod_ref[0, 3:4, :]
    scale_f = mod_ref[0, 4:5, :]
    gate_f = mod_ref[0, 5:6, :]

    y_a = jnp.dot(oa_ref[...], wpa_ref[...], preferred_element_type=F32)
    merged = ga_ref[...].astype(F32) * y_a + gbyb_ref[...].astype(F32)
    mix = jnp.dot(merged.astype(BF16), wo_ref[...], preferred_element_type=F32)
    x1 = _layer_norm(alpha * x_ref[...] + gate_a * mix, l1g_ref[...], l1b_ref[...])
    h2 = x1 * (1.0 + scale_f) + shift_f
    h2b = h2.astype(BF16)

    for s in range(SUBLANES):
        h2_ref[pl.ds(s, tm, stride=SUBLANES), :] = h2[:, s * LANES:(s + 1) * LANES]

    a = jnp.dot(h2b, wgs_ref[...], preferred_element_type=F32)
    b = jnp.dot(h2b, wus_ref[...], preferred_element_type=F32)
    shared = jnp.dot((a * _sigmoid(a) * b).astype(BF16), wds_ref[...], preferred_element_type=F32)
    base_ref[...] = alpha * x1 + gate_f * shared

    logits = lax.dot_general(wrt_ref[...], h2b, (((1,), (1,)), ((), ())), preferred_element_type=F32)
    scores = _sigmoid(logits)
    biased = scores + rb_ref[...]
    ninf = jnp.float32(-jnp.inf)

    sub = lax.broadcasted_iota(jnp.int32, (per_group, tm), 0)
    grp_rows = []
    for g in range(N_EXPERT_GROUPS):
        blk = biased[g * per_group:(g + 1) * per_group, :]
        m1 = jnp.max(blk, axis=0, keepdims=True)
        f1 = jnp.min(jnp.where(blk == m1, sub, per_group), axis=0, keepdims=True)
        m2 = jnp.max(jnp.where(sub == f1, ninf, blk), axis=0, keepdims=True)
        grp_rows.append(m1 + m2)
    grp = jnp.concatenate(grp_rows, axis=0)

    gi = lax.broadcasted_iota(jnp.int32, grp.shape, 0)
    chosen = jnp.zeros(grp.shape, jnp.bool_)
    cur = grp
    for _ in range(TOPK_GROUPS):
        m = jnp.max(cur, axis=0, keepdims=True)
        f = jnp.min(jnp.where(cur == m, gi, N_EXPERT_GROUPS), axis=0, keepdims=True)
        hit = gi == f
        chosen = chosen | hit
        cur = jnp.where(hit, ninf, cur)
    emask = jnp.concatenate(
        [jnp.broadcast_to(chosen[g:g + 1, :], (per_group, tm)) for g in range(N_EXPERT_GROUPS)], axis=0)

    ei = lax.broadcasted_iota(jnp.int32, (n_exp, tm), 0)
    cand = jnp.where(emask, biased, ninf)
    hits, ids, wts = [], [], []
    for _ in range(TOP_K):
        m = jnp.max(cand, axis=0, keepdims=True)
        f = jnp.min(jnp.where(cand == m, ei, n_exp), axis=0, keepdims=True)
        hit = ei == f
        wts.append(jnp.sum(jnp.where(hit, scores, 0.0), axis=0, keepdims=True))
        cand = jnp.where(hit, ninf, cand)
        hits.append(hit)
        ids.append(f)
    onehot = hits[0]
    for hit in hits[1:]:
        onehot = onehot | hit
    onehot = jnp.where(onehot, 1.0, 0.0)

    @pl.when(pl.program_id(0) == 0)
    def _():
        cnt_ref[...] = jnp.zeros(cnt_ref.shape, F32)

    before = jnp.dot(onehot.astype(BF16), tri_ref[...], preferred_element_type=F32) + cnt_ref[...]
    ranks = [jnp.sum(jnp.where(hit, before, 0.0), axis=0, keepdims=True) for hit in hits]
    cnt_ref[...] += jnp.sum(onehot, axis=1, keepdims=True)

    denom = wts[0]
    for wk in wts[1:]:
        denom = denom + wk
    e_ref[...] = jnp.concatenate(ids, axis=0)
    w_ref[...] = jnp.concatenate([ROUTED_SCALE * wk / denom for wk in wts], axis=0)
    r_ref[...] = jnp.concatenate(ranks, axis=0).astype(jnp.int32)


def _post(alpha, oa, ga, gbyb, xf, mod, wpa, wo, l1g, l1b, wrt, rb, wgs, wus, wds, seq):
    t, d = xf.shape
    aw = oa.shape[1]
    n_exp = wrt.shape[0]
    tm = TM_POST
    per_seq = seq // tm
    row = lambda i: (i, 0)
    col = lambda i: (0, i)
    tri = (jnp.arange(tm)[:, None] < jnp.arange(tm)[None, :]).astype(BF16)
    return pl.pallas_call(
        functools.partial(_post_kernel, alpha),
        grid=(t // tm,),
        in_specs=[pl.BlockSpec((tm, aw), row), pl.BlockSpec((tm, d), row), pl.BlockSpec((tm, d), row),
                  pl.BlockSpec((tm, d), row),
                  pl.BlockSpec((1,) + mod.shape[1:], lambda i: (i // per_seq, 0, 0)),
                  _const_spec(wpa.shape), _const_spec(wo.shape), _const_spec(l1g.shape),
                  _const_spec(l1b.shape), _const_spec(wrt.shape), _const_spec(rb.shape),
                  _const_spec(wgs.shape), _const_spec(wus.shape), _const_spec(wds.shape),
                  _const_spec(tri.shape)],
        out_specs=[pl.BlockSpec((tm * SUBLANES, LANES), row), pl.BlockSpec((tm, d), row),
                   pl.BlockSpec((TOP_K, tm), col), pl.BlockSpec((TOP_K, tm), col),
                   pl.BlockSpec((TOP_K, tm), col), _const_spec((n_exp, 1))],
        out_shape=[jax.ShapeDtypeStruct((t * SUBLANES, LANES), F32), jax.ShapeDtypeStruct((t, d), F32),
                   jax.ShapeDtypeStruct((TOP_K, t), jnp.int32), jax.ShapeDtypeStruct((TOP_K, t), F32),
                   jax.ShapeDtypeStruct((TOP_K, t), jnp.int32), jax.ShapeDtypeStruct((n_exp, 1), F32)],
        compiler_params=_cparams(("arbitrary",)),
        name="post",
    )(oa, ga, gbyb, xf, mod, wpa, wo, l1g, l1b, wrt, rb, wgs, wus, wds, tri)


def _row_copy(src_ref, src_row, dst_ref, dst_row, sem):
    return pltpu.make_async_copy(
        src_ref.at[pl.ds(pl.multiple_of(src_row * SUBLANES, SUBLANES), SUBLANES), :],
        dst_ref.at[pl.ds(pl.multiple_of(dst_row * SUBLANES, SUBLANES), SUBLANES), :], sem)


def _dispatch_kernel(dest_ref, h2_ref, xs_ref, sem):
    tm = dest_ref.shape[1]
    tok0 = pl.program_id(0) * tm

    def issue(j, c):
        for k in range(TOP_K):
            _row_copy(h2_ref, tok0 + j, xs_ref, dest_ref[k, j], sem).start()
        return c

    lax.fori_loop(0, tm, issue, 0)
    n = tm * TOP_K * SUBLANES
    pltpu.make_async_copy(h2_ref.at[pl.ds(0, n), :], xs_ref.at[pl.ds(0, n), :], sem).wait()


def _dispatch(dest, h2t, n_rows):
    t = dest.shape[1]
    tm = TM_ROW
    return pl.pallas_call(
        _dispatch_kernel,
        grid=(t // tm,),
        in_specs=[pl.BlockSpec((TOP_K, tm), lambda i: (0, i), memory_space=pltpu.SMEM),
                  pl.BlockSpec(memory_space=pl.ANY)],
        out_specs=pl.BlockSpec(memory_space=pl.ANY),
        out_shape=jax.ShapeDtypeStruct((n_rows * SUBLANES, LANES), F32),
        scratch_shapes=[pltpu.SemaphoreType.DMA],
        compiler_params=_cparams(("arbitrary",)),
        name="dispatch",
    )(dest, h2t)


def _expert_kernel(be_ref, nv_ref, nu_ref, x_ref, wg_ref, wu_ref, wd_ref, y_ref):
    i = pl.program_id(0)

    @pl.when(i < nu_ref[0])
    def _():
        x = jnp.concatenate([x_ref[pl.ds(s, BM, stride=SUBLANES), :] for s in range(SUBLANES)], axis=1)
        row = lax.broadcasted_iota(jnp.int32, (BM, 1), 0)
        x = jnp.where(row < nv_ref[i], x, 0.0).astype(BF16)
        a = jnp.dot(x, wg_ref[0].astype(BF16), preferred_element_type=F32)
        b = jnp.dot(x, wu_ref[0].astype(BF16), preferred_element_type=F32)
        act = (a * _sigmoid(a) * b).astype(BF16)
        y = jnp.dot(act, wd_ref[0].astype(BF16), preferred_element_type=F32)
        for s in range(SUBLANES):
            y_ref[pl.ds(s, BM, stride=SUBLANES), :] = y[:, s * LANES:(s + 1) * LANES]


def _experts(block_e, block_nv, n_used, xs, wg, wu, wd):
    n_blocks = block_e.shape[0]
    n_exp, d, de = wg.shape
    blk = lambda i, be, nv, nu: (jnp.minimum(i, nu[0] - 1), 0)
    wsel = lambda i, be, nv, nu: (be[jnp.minimum(i, nu[0] - 1)], 0, 0)
    grid_spec = pltpu.PrefetchScalarGridSpec(
        num_scalar_prefetch=3,
        grid=(n_blocks,),
        in_specs=[pl.BlockSpec((BM * SUBLANES, LANES), blk),
                  pl.BlockSpec((1, d, de), wsel), pl.BlockSpec((1, d, de), wsel),
                  pl.BlockSpec((1, de, d), wsel)],
        out_specs=pl.BlockSpec((BM * SUBLANES, LANES), blk),
    )
    return pl.pallas_call(
        _expert_kernel,
        grid_spec=grid_spec,
        out_shape=jax.ShapeDtypeStruct(xs.shape, F32),
        compiler_params=_cparams(("arbitrary",)),
        name="experts",
    )(block_e, block_nv, n_used, xs, wg, wu, wd)


def _combine_kernel(dest_ref, w_ref, base_ref, mod_ref, g_ref, b_ref, ys_ref, o_ref, buf, sem):
    tm = dest_ref.shape[1]
    slab = tm * SUBLANES

    def issue(j, c):
        for k in range(TOP_K):
            _row_copy(ys_ref, dest_ref[k, j], buf, k * tm + j, sem).start()
        return c

    lax.fori_loop(0, tm, issue, 0)
    pltpu.make_async_copy(ys_ref.at[pl.ds(0, TOP_K * slab), :], buf, sem).wait()

    wpad = jnp.concatenate([w_ref[...], jnp.zeros((LANES - TOP_K, tm), F32)], axis=0)
    wcol = wpad.T
    routed = jnp.zeros(base_ref.shape, F32)
    for k in range(TOP_K):
        yk = jnp.concatenate(
            [buf[pl.ds(k * slab + s, tm, stride=SUBLANES), :] for s in range(SUBLANES)], axis=1)
        routed = routed + wcol[:, k:k + 1] * yk
    gate_f = mod_ref[0, 5:6, :]
    o_ref[...] = _layer_norm(base_ref[...] + gate_f * routed, g_ref[...], b_ref[...])


def _combine(dest, wts, base, mod, g, b, ys, seq):
    t, d = base.shape
    tm = TM_ROW
    per_seq = seq // tm
    col = lambda i: (0, i)
    row = lambda i: (i, 0)
    return pl.pallas_call(
        _combine_kernel,
        grid=(t // tm,),
        in_specs=[pl.BlockSpec((TOP_K, tm), col, memory_space=pltpu.SMEM),
                  pl.BlockSpec((TOP_K, tm), col), pl.BlockSpec((tm, d), row),
                  pl.BlockSpec((1,) + mod.shape[1:], lambda i: (i // per_seq, 0, 0)),
                  _const_spec(g.shape), _const_spec(b.shape),
                  pl.BlockSpec(memory_space=pl.ANY)],
        out_specs=pl.BlockSpec((tm, d), row),
        out_shape=jax.ShapeDtypeStruct((t, d), F32),
        scratch_shapes=[pltpu.VMEM((TOP_K * tm * SUBLANES, LANES), F32), pltpu.SemaphoreType.DMA],
        compiler_params=_cparams(("arbitrary",)),
        name="combine",
    )(dest, wts, base, mod, g, b, ys)


def _rope_tables(seq):
    half = HEAD_DIM // 2
    inv_freq = ROPE_THETA ** (-jnp.arange(half, dtype=F32) / half)
    ang = jnp.arange(seq, dtype=F32)[:, None] * inv_freq[None, :]
    cos = jnp.concatenate([jnp.cos(ang), jnp.cos(ang)], axis=1)
    sin = jnp.concatenate([-jnp.sin(ang), jnp.sin(ang)], axis=1)
    return jnp.tile(cos, (1, N_HEADS)), jnp.tile(sin, (1, N_HEADS))


def _layer(x, c, w_ada, b_ada, w_in, ln_v_g, ln_v_b, w_spatial, b_spatial, w_proj_a, w_proj_b, w_out,
           ln1_g, ln1_b, w_router, router_bias, w_gate_e, w_up_e, w_down_e, w_gate_sh, w_up_sh,
           w_down_sh, ln2_g, ln2_b, alpha):
    bsz, seq, d = x.shape
    assert seq % (max(DILATIONS) * BAND) == 0 and seq % TM_IN == 0 and d == SUBLANES * LANES
    t = bsz * seq
    aw = N_HEADS * HEAD_DIM
    gwid = ln_v_g.shape[0]
    n_exp = w_router.shape[1]
    xf = x.reshape(t, d)

    mod = _ada(c, w_ada, b_ada).reshape(bsz, 6, d)

    cuts = [0, aw, 2 * aw, 3 * aw, 3 * aw + gwid, 3 * aw + 2 * gwid, 3 * aw + 2 * gwid + d,
            3 * aw + 2 * gwid + 2 * d]
    wq, wk, wv, wu, wvg, wga, wgb = [w_in[:, a:b].astype(BF16) for a, b in zip(cuts[:-1], cuts[1:])]
    cos, sin = _rope_tables(seq)
    bsp = jnp.repeat(b_spatial.T, gwid // GMLP_GROUPS, axis=1)
    q, k, v, ga, gbyb = _inproj(xf, mod, cos, sin, wq, wk, wv, wu, wvg, wga, wgb,
                                ln_v_g.reshape(1, gwid), ln_v_b.reshape(1, gwid), w_spatial, bsp,
                                w_proj_b.astype(BF16), seq)

    oa = _attention(q.reshape(bsz, seq, aw), k.reshape(bsz, seq, aw), v.reshape(bsz, seq, aw))

    h2t, base, e_idx, wts, rank, cnt = _post(
        alpha, oa.reshape(t, aw), ga, gbyb, xf, mod, w_proj_a.astype(BF16), w_out.astype(BF16),
        ln1_g.reshape(1, d), ln1_b.reshape(1, d), w_router.T.astype(BF16), router_bias.reshape(n_exp, 1),
        w_gate_sh.astype(BF16), w_up_sh.astype(BF16), w_down_sh.astype(BF16), seq)

    counts = cnt[:, 0].astype(jnp.int32)
    nblk = (counts + BM - 1) // BM
    blk_end = jnp.cumsum(nblk)
    blk_start = blk_end - nblk
    n_blocks = (t * TOP_K) // BM + n_exp
    bidx = jnp.arange(n_blocks, dtype=jnp.int32)
    block_e = jnp.minimum(jnp.searchsorted(blk_end, bidx, side="right"), n_exp - 1).astype(jnp.int32)
    block_nv = jnp.clip(counts[block_e] - (bidx - blk_start[block_e]) * BM, 0, BM).astype(jnp.int32)
    n_used = blk_end[-1:].astype(jnp.int32)
    dest = (blk_start * BM)[e_idx] + rank

    xs = _dispatch(dest, h2t, n_blocks * BM)
    ys = _experts(block_e, block_nv, n_used, xs, w_gate_e, w_up_e, w_down_e)
    out = _combine(dest, wts, base, mod, ln2_g.reshape(1, d), ln2_b.reshape(1, d), ys, seq)
    return out.reshape(bsz, seq, d)


def kernel(x, c, w_ada, b_ada, w_in, ln_v_g, ln_v_b, w_spatial, b_spatial, w_proj_a, w_proj_b, w_out,
           ln1_g, ln1_b, w_router, router_bias, w_gate_e, w_up_e, w_down_e, w_gate_sh, w_up_sh,
           w_down_sh, ln2_g, ln2_b):
    depth = w_ada.shape[0]
    alpha = (2.0 * depth) ** 0.25
    for l in range(depth):
        x = _layer(x, c, w_ada[l], b_ada[l], w_in[l], ln_v_g[l], ln_v_b[l], w_spatial[l], b_spatial[l],
                   w_proj_a[l], w_proj_b[l], w_out[l], ln1_g[l], ln1_b[l], w_router[l], router_bias[l],
                   w_gate_e[l], w_up_e[l], w_down_e[l], w_gate_sh[l], w_up_sh[l], w_down_sh[l],
                   ln2_g[l], ln2_b[l], alpha)
    return x
```

```python
import functools
import math

import jax
import jax.numpy as jnp
from jax import lax
from jax.experimental import pallas as pl
from jax.experimental.pallas import tpu as pltpu

F32 = jnp.float32
BF16 = jnp.bfloat16

N_HEADS = 8
HEAD_DIM = 64
DILATIONS = (1, 4, 16)
BAND = 128
ROPE_THETA = 10000.0
GMLP_GROUPS = 8
GMLP_CHUNK = 128
N_EXPERT_GROUPS = 8
TOPK_GROUPS = 4
TOP_K = 8
ROUTED_SCALE = 2.5
LN_EPS = 1e-5
LANES = 128
SUBLANES = 8
VMEM_LIMIT = 56 * 1024 * 1024

TM_IN = 512
TM_POST = 256
TM_DISPATCH = 256
TM_COMBINE = 128
BM = 256
ATTN_BLOCKS_PER_TRIP = 4


def _cparams(sem):
    return pltpu.CompilerParams(dimension_semantics=sem, vmem_limit_bytes=VMEM_LIMIT)


def _const_spec(shape):
    nd = len(shape)
    return pl.BlockSpec(shape, lambda *_: (0,) * nd)


def _layer_norm(r, g, b):
    mu = jnp.mean(r, axis=-1, keepdims=True)
    d = r - mu
    var = jnp.mean(d * d, axis=-1, keepdims=True)
    return d * lax.rsqrt(var + LN_EPS) * g + b


def _gelu(t):
    return 0.5 * t * (1.0 + lax.erf(t * (1.0 / math.sqrt(2.0))))


def _sigmoid(t):
    return 1.0 / (1.0 + jnp.exp(-t))


def _ada_kernel(c_ref, w_ref, b_ref, o_ref):
    c = c_ref[...]
    o_ref[...] = jnp.dot(c * _sigmoid(c), w_ref[...], preferred_element_type=F32) + b_ref[...]


def _ada(c, w, b):
    bsz, d = c.shape
    n = w.shape[1]
    tn = 1024
    return pl.pallas_call(
        _ada_kernel,
        grid=(n // tn,),
        in_specs=[_const_spec((bsz, d)), pl.BlockSpec((d, tn), lambda j: (0, j)),
                  pl.BlockSpec((1, tn), lambda j: (0, j))],
        out_specs=pl.BlockSpec((bsz, tn), lambda j: (0, j)),
        out_shape=jax.ShapeDtypeStruct((bsz, n), F32),
        compiler_params=_cparams(("arbitrary",)),
        name="ada",
    )(c, w, b.reshape(1, n))


def _inproj_kernel(x_ref, mod_ref, cos_ref, sin_ref, wq_ref, wk_ref, wv_ref, wu_ref, wvg_ref,
                   wga_ref, wgb_ref, lng_ref, lnb_ref, wsp_ref, bsp_ref, wpb_ref,
                   q_ref, k_ref, v_ref, ga_ref, gbyb_ref, u_scr, vv_scr, gated_scr):
    tm = x_ref.shape[0]
    shift = mod_ref[0, 0:1, :]
    scale = mod_ref[0, 1:2, :]
    h = (x_ref[...] * (1.0 + scale) + shift).astype(BF16)

    cos = cos_ref[...]
    sin = sin_ref[...]
    aw = cos.shape[1]
    lane = lax.broadcasted_iota(jnp.int32, (tm, aw), 1)
    first_half = (lane % HEAD_DIM) < (HEAD_DIM // 2)

    def rope(t):
        partner = jnp.where(first_half, pltpu.roll(t, aw - HEAD_DIM // 2, 1),
                            pltpu.roll(t, HEAD_DIM // 2, 1))
        return t * cos + partner * sin

    q = jnp.dot(h, wq_ref[...], preferred_element_type=F32)
    q_ref[...] = (rope(q) * (HEAD_DIM ** -0.5)).astype(BF16)
    k = jnp.dot(h, wk_ref[...], preferred_element_type=F32)
    k_ref[...] = rope(k).astype(BF16)
    v_ref[...] = jnp.dot(h, wv_ref[...], preferred_element_type=F32).astype(BF16)

    u_scr[...] = _gelu(jnp.dot(h, wu_ref[...], preferred_element_type=F32))
    vg = _gelu(jnp.dot(h, wvg_ref[...], preferred_element_type=F32))
    vv_scr[...] = _layer_norm(vg, lng_ref[...], lnb_ref[...]).astype(BF16)

    ti = lax.broadcasted_iota(jnp.int32, (GMLP_CHUNK, GMLP_CHUNK), 0)
    si = lax.broadcasted_iota(jnp.int32, (GMLP_CHUNK, GMLP_CHUNK), 1)
    causal = si <= ti
    gw = u_scr.shape[1] // GMLP_GROUPS
    for g in range(GMLP_GROUPS):
        ws = jnp.where(causal, wsp_ref[g], 0.0).astype(BF16)
        cols = slice(g * gw, (g + 1) * gw)
        for c in range(tm // GMLP_CHUNK):
            rows = slice(c * GMLP_CHUNK, (c + 1) * GMLP_CHUNK)
            sv = jnp.dot(ws, vv_scr[rows, cols], preferred_element_type=F32) + bsp_ref[:, cols]
            gated_scr[rows, cols] = (u_scr[rows, cols] * sv).astype(BF16)

    y_b = jnp.dot(gated_scr[...], wpb_ref[...], preferred_element_type=F32)
    g_b = _sigmoid(jnp.dot(h, wgb_ref[...], preferred_element_type=F32))
    gbyb_ref[...] = (g_b * y_b).astype(BF16)
    ga_ref[...] = _sigmoid(jnp.dot(h, wga_ref[...], preferred_element_type=F32)).astype(BF16)


def _inproj(xf, mod, cos, sin, wq, wk, wv, wu, wvg, wga, wgb, lng, lnb, wsp, bsp, wpb, seq):
    t, d = xf.shape
    aw = wq.shape[1]
    gwid = wu.shape[1]
    tm = TM_IN
    per_seq = seq // tm
    row = lambda i: (i, 0)
    tab = lambda i: (i % per_seq, 0)
    return pl.pallas_call(
        _inproj_kernel,
        grid=(t // tm,),
        in_specs=[pl.BlockSpec((tm, d), row),
                  pl.BlockSpec((1,) + mod.shape[1:], lambda i: (i // per_seq, 0, 0)),
                  pl.BlockSpec((tm, aw), tab), pl.BlockSpec((tm, aw), tab),
                  _const_spec(wq.shape), _const_spec(wk.shape), _const_spec(wv.shape),
                  _const_spec(wu.shape), _const_spec(wvg.shape), _const_spec(wga.shape),
                  _const_spec(wgb.shape), _const_spec(lng.shape), _const_spec(lnb.shape),
                  _const_spec(wsp.shape), _const_spec(bsp.shape), _const_spec(wpb.shape)],
        out_specs=[pl.BlockSpec((tm, aw), row)] * 3 + [pl.BlockSpec((tm, d), row)] * 2,
        out_shape=[jax.ShapeDtypeStruct((t, aw), BF16)] * 3 + [jax.ShapeDtypeStruct((t, d), BF16)] * 2,
        scratch_shapes=[pltpu.VMEM((tm, gwid), F32), pltpu.VMEM((tm, gwid), BF16),
                        pltpu.VMEM((tm, gwid), BF16)],
        compiler_params=_cparams(("arbitrary",)),
        name="inproj",
    )(xf, mod, cos, sin, wq, wk, wv, wu, wvg, wga, wgb, lng, lnb, wsp, bsp, wpb)


def _rows(start, size, stride):
    if stride == 1:
        return pl.ds(start, size)
    return pl.ds(start, size, stride=stride)


def _band_aligned(start):
    return start if isinstance(start, int) else pl.multiple_of(start, BAND)


def _interleaved_loop(count, load, compute, store):
    per_trip = max(u for u in range(1, ATTN_BLOCKS_PER_TRIP + 1) if count % u == 0) if count else 1

    def trip(it, c):
        items = [it * per_trip + u for u in range(per_trip)]
        results = [compute(ops) for ops in [load(i) for i in items]]
        for i, res in zip(items, results):
            store(i, res)
        return c

    if count == per_trip:
        trip(0, 0)
    elif count:
        lax.fori_loop(0, count // per_trip, trip, 0)


def _attn_kernel(*refs):
    npat = len(DILATIONS)
    qkv = [refs[3 * p:3 * p + 3] for p in range(npat)]
    o_ref, out_scr, lse_scr = refs[3 * npat:]
    seq = o_ref.shape[1]

    lane = lax.broadcasted_iota(jnp.int32, (BAND, LANES), 1)
    head0 = lane < HEAD_DIM
    neg = jnp.float32(-1e30)

    qi = lax.broadcasted_iota(jnp.int32, (BAND, 2 * BAND), 0)
    kj = lax.broadcasted_iota(jnp.int32, (BAND, 2 * BAND), 1)

    def load(p, blk, nb):
        q_ref, k_ref, v_ref = qkv[p]
        q_start = blk * BAND
        k_start = jnp.maximum(q_start - BAND, 0)
        off = q_start - k_start
        lowest = jnp.where(blk % nb == 0, off, qi)
        mask = (kj >= lowest) & (kj <= qi + off)
        krows = pl.ds(_band_aligned(k_start), 2 * BAND)
        return (q_ref[0, 0, pl.ds(_band_aligned(q_start), BAND), :], k_ref[0, 0, krows, :],
                v_ref[0, 0, krows, :], mask)

    def attend(ops):
        qb, kb, vb, mask = ops
        zero = jnp.zeros(qb.shape, qb.dtype)
        parts = []
        for hd in range(2):
            qh = jnp.where(head0 if hd == 0 else ~head0, qb, zero)
            s = lax.dot_general(qh, kb, (((1,), (1,)), ((), ())), preferred_element_type=F32)
            s = jnp.where(mask, s, neg)
            m = jnp.max(s, axis=-1, keepdims=True)
            pr = jnp.exp(s - m)
            l = jnp.sum(pr, axis=-1, keepdims=True)
            acc = jnp.dot(pr.astype(BF16), vb, preferred_element_type=F32)
            parts.append((acc / l, m + jnp.log(l)))
        return [jnp.where(head0, a0, a1) for a0, a1 in zip(*parts)]

    def store(p, seq_start, stride, res):
        rows = _rows(seq_start if stride > 1 else _band_aligned(seq_start), BAND, stride)
        out_scr[p, rows, :] = res[0]
        lse_scr[p, rows, :] = res[1]

    for p, dil in enumerate(DILATIONS):
        n = seq // dil
        nb = n // BAND

        _interleaved_loop(
            seq // BAND, lambda blk, p=p, nb=nb: load(p, blk, nb), attend,
            lambda blk, res, p=p, dil=dil, nb=nb: store(p, blk // nb + dil * BAND * (blk % nb), dil, res))

    top = jnp.maximum(jnp.maximum(lse_scr[0], lse_scr[1]), lse_scr[2])
    num = jnp.zeros(top.shape, F32)
    den = jnp.zeros(top.shape, F32)
    for p in range(npat):
        w = jnp.exp(lse_scr[p] - top)
        num = num + w * out_scr[p]
        den = den + w
    o_ref[0] = (num / den).astype(BF16)


def _class_major(t, dil):
    bsz, seq, aw = t.shape
    t = t.reshape(bsz, seq // dil, dil, aw // LANES, LANES)
    return t.transpose(0, 3, 2, 1, 4).reshape(bsz, aw // LANES, seq, LANES)


def _attention(q, k, v):
    bsz, seq, aw = q.shape
    npat = len(DILATIONS)
    spec = pl.BlockSpec((1, 1, seq, LANES), lambda b, hp: (b, hp, 0, 0))
    operands = [_class_major(t, dil) for dil in DILATIONS for t in (q, k, v)]
    return pl.pallas_call(
        _attn_kernel,
        grid=(bsz, aw // LANES),
        in_specs=[spec] * (3 * npat),
        out_specs=pl.BlockSpec((1, seq, LANES), lambda b, hp: (b, 0, hp)),
        out_shape=jax.ShapeDtypeStruct((bsz, seq, aw), BF16),
        scratch_shapes=[pltpu.VMEM((npat, seq, LANES), F32)] * 2,
        compiler_params=_cparams(("arbitrary", "arbitrary")),
        name="attn",
    )(*operands)


def _post_kernel(alpha, oa_ref, ga_ref, gbyb_ref, x_ref, mod_ref, wpa_ref, wo_ref, l1g_ref, l1b_ref,
                 wrt_ref, rb_ref, wgs_ref, wus_ref, wds_ref, tri_ref,
                 h2_ref, base_ref, e_ref, w_ref, r_ref, cnt_ref):
    tm = x_ref.shape[0]
    n_exp = wrt_ref.shape[0]
    per_group = n_exp // N_EXPERT_GROUPS
    gate_a = mod_ref[0, 2:3, :]
    shift_f = mod_ref[0, 3:4, :]
    scale_f = mod_ref[0, 4:5, :]
    gate_f = mod_ref[0, 5:6, :]

    y_a = jnp.dot(oa_ref[...], wpa_ref[...], preferred_element_type=F32)
    merged = ga_ref[...].astype(F32) * y_a + gbyb_ref[...].astype(F32)
    mix = jnp.dot(merged.astype(BF16), wo_ref[...], preferred_element_type=F32)
    x1 = _layer_norm(alpha * x_ref[...] + gate_a * mix, l1g_ref[...], l1b_ref[...])
    h2 = x1 * (1.0 + scale_f) + shift_f
    h2b = h2.astype(BF16)

    for s in range(SUBLANES):
        h2_ref[pl.ds(s, tm, stride=SUBLANES), :] = h2[:, s * LANES:(s + 1) * LANES]

    a = jnp.dot(h2b, wgs_ref[...], preferred_element_type=F32)
    b = jnp.dot(h2b, wus_ref[...], preferred_element_type=F32)
    shared = jnp.dot((a * _sigmoid(a) * b).astype(BF16), wds_ref[...], preferred_element_type=F32)
    base_ref[...] = alpha * x1 + gate_f * shared

    logits = lax.dot_general(wrt_ref[...], h2b, (((1,), (1,)), ((), ())), preferred_element_type=F32)
    scores = _sigmoid(logits)
    biased = scores + rb_ref[...]
    ninf = jnp.float32(-jnp.inf)

    sub = lax.broadcasted_iota(jnp.int32, (per_group, tm), 0)
    grp_rows = []
    for g in range(N_EXPERT_GROUPS):
        blk = biased[g * per_group:(g + 1) * per_group, :]
        m1 = jnp.max(blk, axis=0, keepdims=True)
        f1 = jnp.min(jnp.where(blk == m1, sub, per_group), axis=0, keepdims=True)
        m2 = jnp.max(jnp.where(sub == f1, ninf, blk), axis=0, keepdims=True)
        grp_rows.append(m1 + m2)
    grp = jnp.concatenate(grp_rows, axis=0)

    gi = lax.broadcasted_iota(jnp.int32, grp.shape, 0)
    chosen = jnp.zeros(grp.shape, jnp.bool_)
    cur = grp
    for _ in range(TOPK_GROUPS):
        m = jnp.max(cur, axis=0, keepdims=True)
        f = jnp.min(jnp.where(cur == m, gi, N_EXPERT_GROUPS), axis=0, keepdims=True)
        hit = gi == f
        chosen = chosen | hit
        cur = jnp.where(hit, ninf, cur)
    emask = jnp.concatenate(
        [jnp.broadcast_to(chosen[g:g + 1, :], (per_group, tm)) for g in range(N_EXPERT_GROUPS)], axis=0)

    ei = lax.broadcasted_iota(jnp.int32, (n_exp, tm), 0)
    cand = jnp.where(emask, biased, ninf)
    hits, ids, wts = [], [], []
    for _ in range(TOP_K):
        m = jnp.max(cand, axis=0, keepdims=True)
        f = jnp.min(jnp.where(cand == m, ei, n_exp), axis=0, keepdims=True)
        hit = ei == f
        wts.append(jnp.sum(jnp.where(hit, scores, 0.0), axis=0, keepdims=True))
        cand = jnp.where(hit, ninf, cand)
        hits.append(hit)
        ids.append(f)
    onehot = hits[0]
    for hit in hits[1:]:
        onehot = onehot | hit
    onehot = jnp.where(onehot, 1.0, 0.0)

    @pl.when(pl.program_id(0) == 0)
    def _():
        cnt_ref[...] = jnp.zeros(cnt_ref.shape, F32)

    before = jnp.dot(onehot.astype(BF16), tri_ref[...], preferred_element_type=F32) + cnt_ref[...]
    ranks = [jnp.sum(jnp.where(hit, before, 0.0), axis=0, keepdims=True) for hit in hits]
    cnt_ref[...] += jnp.sum(onehot, axis=1, keepdims=True)

    denom = wts[0]
    for wk in wts[1:]:
        denom = denom + wk
    e_ref[...] = jnp.concatenate(ids, axis=0)
    w_ref[...] = jnp.concatenate([ROUTED_SCALE * wk / denom for wk in wts], axis=0)
    r_ref[...] = jnp.concatenate(ranks, axis=0).astype(jnp.int32)


def _post(alpha, oa, ga, gbyb, xf, mod, wpa, wo, l1g, l1b, wrt, rb, wgs, wus, wds, seq):
    t, d = xf.shape
    aw = oa.shape[1]
    n_exp = wrt.shape[0]
    tm = TM_POST
    per_seq = seq // tm
    row = lambda i: (i, 0)
    col = lambda i: (0, i)
    tri = (jnp.arange(tm)[:, None] < jnp.arange(tm)[None, :]).astype(BF16)
    return pl.pallas_call(
        functools.partial(_post_kernel, alpha),
        grid=(t // tm,),
        in_specs=[pl.BlockSpec((tm, aw), row), pl.BlockSpec((tm, d), row), pl.BlockSpec((tm, d), row),
                  pl.BlockSpec((tm, d), row),
                  pl.BlockSpec((1,) + mod.shape[1:], lambda i: (i // per_seq, 0, 0)),
                  _const_spec(wpa.shape), _const_spec(wo.shape), _const_spec(l1g.shape),
                  _const_spec(l1b.shape), _const_spec(wrt.shape), _const_spec(rb.shape),
                  _const_spec(wgs.shape), _const_spec(wus.shape), _const_spec(wds.shape),
                  _const_spec(tri.shape)],
        out_specs=[pl.BlockSpec((tm * SUBLANES, LANES), row), pl.BlockSpec((tm, d), row),
                   pl.BlockSpec((TOP_K, tm), col), pl.BlockSpec((TOP_K, tm), col),
                   pl.BlockSpec((TOP_K, tm), col), _const_spec((n_exp, 1))],
        out_shape=[jax.ShapeDtypeStruct((t * SUBLANES, LANES), F32), jax.ShapeDtypeStruct((t, d), F32),
                   jax.ShapeDtypeStruct((TOP_K, t), jnp.int32), jax.ShapeDtypeStruct((TOP_K, t), F32),
                   jax.ShapeDtypeStruct((TOP_K, t), jnp.int32), jax.ShapeDtypeStruct((n_exp, 1), F32)],
        compiler_params=_cparams(("arbitrary",)),
        name="post",
    )(oa, ga, gbyb, xf, mod, wpa, wo, l1g, l1b, wrt, rb, wgs, wus, wds, tri)


def _row_copy(src_ref, src_row, dst_ref, dst_row, sem):
    return pltpu.make_async_copy(
        src_ref.at[pl.ds(pl.multiple_of(src_row * SUBLANES, SUBLANES), SUBLANES), :],
        dst_ref.at[pl.ds(pl.multiple_of(dst_row * SUBLANES, SUBLANES), SUBLANES), :], sem)


def _dispatch_kernel(n_blocks, rs_ref, ps_ref, pl_ref, nu_ref, e_ref, r_ref, h2_ref, xs_ref, dest_ref,
                     zbuf, sem, zsem):
    i = pl.program_id(0)
    n_steps = pl.num_programs(0)
    tm = e_ref.shape[1]
    n_exp = rs_ref.shape[0]
    per_step = -(-n_exp // n_steps)
    zbuf[...] = jnp.zeros(zbuf.shape, F32)

    def zero_fill(act):
        for q in range(per_step):
            e = jnp.minimum(i * per_step + q, n_exp - 1)
            live = i * per_step + q < n_exp
            gap = pl_ref[e]
            for bit in [1 << s for s in range(BM.bit_length() - 1)]:
                @pl.when(live & ((gap & bit) != 0))
                def _(bit=bit, e=e, gap=gap):
                    off = ps_ref[e] + (gap & ~(2 * bit - 1))
                    act(pltpu.make_async_copy(
                        zbuf.at[pl.ds(0, bit * SUBLANES), :],
                        xs_ref.at[pl.ds(pl.multiple_of(off * SUBLANES, SUBLANES), bit * SUBLANES), :], zsem))
            blk = nu_ref[0] + i * per_step + q
            @pl.when(blk < n_blocks)
            def _(blk=blk):
                rows = BM * SUBLANES
                act(pltpu.make_async_copy(
                    zbuf, xs_ref.at[pl.ds(pl.multiple_of(blk * rows, rows), rows), :], zsem))

    zero_fill(lambda cp: cp.start(priority=1))

    def issue(j, c):
        for k in range(TOP_K):
            d = rs_ref[e_ref[k, j]] + r_ref[k, j]
            dest_ref[k, j] = d
            _row_copy(h2_ref, j, xs_ref, d, sem).start(priority=k % 2)
        return c

    lax.fori_loop(0, tm, issue, 0)
    n = tm * TOP_K * SUBLANES
    pltpu.make_async_copy(xs_ref.at[pl.ds(0, n), :], xs_ref.at[pl.ds(0, n), :], sem).wait()
    zero_fill(lambda cp: cp.wait())


def _dispatch(row_start, pad_start, pad_len, n_used, e_idx, rank, h2t, n_blocks):
    t = e_idx.shape[1]
    tm = TM_DISPATCH
    col = lambda i, *_: (0, i)
    smem_blk = pl.BlockSpec((TOP_K, tm), col, memory_space=pltpu.SMEM)
    grid_spec = pltpu.PrefetchScalarGridSpec(
        num_scalar_prefetch=4,
        grid=(t // tm,),
        in_specs=[smem_blk, smem_blk, pl.BlockSpec((tm * SUBLANES, LANES), lambda i, *_: (i, 0))],
        out_specs=[pl.BlockSpec(memory_space=pl.ANY), smem_blk],
        scratch_shapes=[pltpu.VMEM((BM * SUBLANES, LANES), F32), pltpu.SemaphoreType.DMA,
                        pltpu.SemaphoreType.DMA],
    )
    return pl.pallas_call(
        functools.partial(_dispatch_kernel, n_blocks),
        grid_spec=grid_spec,
        out_shape=[jax.ShapeDtypeStruct((n_blocks * BM * SUBLANES, LANES), F32),
                   jax.ShapeDtypeStruct((TOP_K, t), jnp.int32)],
        compiler_params=_cparams(("arbitrary",)),
        name="dispatch",
    )(row_start, pad_start, pad_len, n_used, e_idx, rank, h2t)


def _expert_kernel(be_ref, nu_ref, x_ref, wg_ref, wu_ref, wd_ref, y_ref):
    @pl.when(pl.program_id(0) < nu_ref[0])
    def _():
        x = jnp.concatenate([x_ref[pl.ds(s, BM, stride=SUBLANES), :] for s in range(SUBLANES)], axis=1)
        x = x.astype(BF16)
        a = jnp.dot(x, wg_ref[0].astype(BF16), preferred_element_type=F32)
        b = jnp.dot(x, wu_ref[0].astype(BF16), preferred_element_type=F32)
        act = (a * _sigmoid(a) * b).astype(BF16)
        y = jnp.dot(act, wd_ref[0].astype(BF16), preferred_element_type=F32)
        for s in range(SUBLANES):
            y_ref[pl.ds(s, BM, stride=SUBLANES), :] = y[:, s * LANES:(s + 1) * LANES]


def _experts(block_e, n_used, xs, wg, wu, wd):
    n_blocks = block_e.shape[0]
    n_exp, d, de = wg.shape
    blk = lambda i, be, nu: (jnp.minimum(i, nu[0] - 1), 0)
    wsel = lambda i, be, nu: (be[jnp.minimum(i, nu[0] - 1)], 0, 0)
    grid_spec = pltpu.PrefetchScalarGridSpec(
        num_scalar_prefetch=2,
        grid=(n_blocks,),
        in_specs=[pl.BlockSpec((BM * SUBLANES, LANES), blk),
                  pl.BlockSpec((1, d, de), wsel), pl.BlockSpec((1, d, de), wsel),
                  pl.BlockSpec((1, de, d), wsel)],
        out_specs=pl.BlockSpec((BM * SUBLANES, LANES), blk),
    )
    return pl.pallas_call(
        _expert_kernel,
        grid_spec=grid_spec,
        out_shape=jax.ShapeDtypeStruct(xs.shape, F32),
        input_output_aliases={2: 0},
        compiler_params=_cparams(("arbitrary",)),
        name="experts",
    )(block_e, n_used, xs, wg, wu, wd)


def _combine_kernel(dest_ref, w_ref, base_ref, mod_ref, g_ref, b_ref, ys_ref, o_ref, buf, sem):
    tm = dest_ref.shape[1]
    slab = tm * SUBLANES

    def issue(j, c):
        for k in range(TOP_K):
            _row_copy(ys_ref, dest_ref[k, j], buf, k * tm + j, sem).start(priority=k % 2)
        return c

    lax.fori_loop(0, tm, issue, 0)
    pltpu.make_async_copy(ys_ref.at[pl.ds(0, TOP_K * slab), :], buf, sem).wait()

    wpad = jnp.concatenate([w_ref[...], jnp.zeros((LANES - TOP_K, tm), F32)], axis=0)
    wcol = wpad.T
    routed = jnp.zeros(base_ref.shape, F32)
    for k in range(TOP_K):
        yk = jnp.concatenate(
            [buf[pl.ds(k * slab + s, tm, stride=SUBLANES), :] for s in range(SUBLANES)], axis=1)
        routed = routed + wcol[:, k:k + 1] * yk
    gate_f = mod_ref[0, 5:6, :]
    o_ref[...] = _layer_norm(base_ref[...] + gate_f * routed, g_ref[...], b_ref[...])


def _combine(dest, wts, base, mod, g, b, ys, seq):
    t, d = base.shape
    tm = TM_COMBINE
    per_seq = seq // tm
    col = lambda i: (0, i)
    row = lambda i: (i, 0)
    return pl.pallas_call(
        _combine_kernel,
        grid=(t // tm,),
        in_specs=[pl.BlockSpec((TOP_K, tm), col, memory_space=pltpu.SMEM),
                  pl.BlockSpec((TOP_K, tm), col), pl.BlockSpec((tm, d), row),
                  pl.BlockSpec((1,) + mod.shape[1:], lambda i: (i // per_seq, 0, 0)),
                  _const_spec(g.shape), _const_spec(b.shape),
                  pl.BlockSpec(memory_space=pl.ANY)],
        out_specs=pl.BlockSpec((tm, d), row),
        out_shape=jax.ShapeDtypeStruct((t, d), F32),
        scratch_shapes=[pltpu.VMEM((TOP_K * tm * SUBLANES, LANES), F32), pltpu.SemaphoreType.DMA],
        compiler_params=_cparams(("arbitrary",)),
        name="combine",
    )(dest, wts, base, mod, g, b, ys)


def _rope_tables(seq):
    half = HEAD_DIM // 2
    inv_freq = ROPE_THETA ** (-jnp.arange(half, dtype=F32) / half)
    ang = jnp.arange(seq, dtype=F32)[:, None] * inv_freq[None, :]
    cos = jnp.concatenate([jnp.cos(ang), jnp.cos(ang)], axis=1)
    sin = jnp.concatenate([-jnp.sin(ang), jnp.sin(ang)], axis=1)
    return jnp.tile(cos, (1, N_HEADS)), jnp.tile(sin, (1, N_HEADS))


def _layer(x, c, w_ada, b_ada, w_in, ln_v_g, ln_v_b, w_spatial, b_spatial, w_proj_a, w_proj_b, w_out,
           ln1_g, ln1_b, w_router, router_bias, w_gate_e, w_up_e, w_down_e, w_gate_sh, w_up_sh,
           w_down_sh, ln2_g, ln2_b, alpha):
    bsz, seq, d = x.shape
    assert seq % (max(DILATIONS) * BAND) == 0 and seq % TM_IN == 0 and d == SUBLANES * LANES
    t = bsz * seq
    aw = N_HEADS * HEAD_DIM
    gwid = ln_v_g.shape[0]
    n_exp = w_router.shape[1]
    xf = x.reshape(t, d)

    mod = _ada(c, w_ada, b_ada).reshape(bsz, 6, d)

    cuts = [0, aw, 2 * aw, 3 * aw, 3 * aw + gwid, 3 * aw + 2 * gwid, 3 * aw + 2 * gwid + d,
            3 * aw + 2 * gwid + 2 * d]
    wq, wk, wv, wu, wvg, wga, wgb = [w_in[:, a:b].astype(BF16) for a, b in zip(cuts[:-1], cuts[1:])]
    cos, sin = _rope_tables(seq)
    bsp = jnp.repeat(b_spatial.T, gwid // GMLP_GROUPS, axis=1)
    q, k, v, ga, gbyb = _inproj(xf, mod, cos, sin, wq, wk, wv, wu, wvg, wga, wgb,
                                ln_v_g.reshape(1, gwid), ln_v_b.reshape(1, gwid), w_spatial, bsp,
                                w_proj_b.astype(BF16), seq)

    oa = _attention(q.reshape(bsz, seq, aw), k.reshape(bsz, seq, aw), v.reshape(bsz, seq, aw))

    h2t, base, e_idx, wts, rank, cnt = _post(
        alpha, oa.reshape(t, aw), ga, gbyb, xf, mod, w_proj_a.astype(BF16), w_out.astype(BF16),
        ln1_g.reshape(1, d), ln1_b.reshape(1, d), w_router.T.astype(BF16), router_bias.reshape(n_exp, 1),
        w_gate_sh.astype(BF16), w_up_sh.astype(BF16), w_down_sh.astype(BF16), seq)

    counts = cnt[:, 0].astype(jnp.int32)
    nblk = (counts + BM - 1) // BM
    blk_end = jnp.cumsum(nblk)
    blk_start = blk_end - nblk
    n_blocks = (t * TOP_K) // BM + n_exp
    bidx = jnp.arange(n_blocks, dtype=jnp.int32)
    block_e = jnp.minimum(jnp.sum(blk_end[None, :] <= bidx[:, None], axis=1), n_exp - 1).astype(jnp.int32)
    n_used = blk_end[-1:].astype(jnp.int32)
    row_start = (blk_start * BM).astype(jnp.int32)

    xs, dest = _dispatch(row_start, row_start + counts, nblk * BM - counts, n_used, e_idx, rank, h2t, n_blocks)
    ys = _experts(block_e, n_used, xs, w_gate_e, w_up_e, w_down_e)
    out = _combine(dest, wts, base, mod, ln2_g.reshape(1, d), ln2_b.reshape(1, d), ys, seq)
    return out.reshape(bsz, seq, d)


def kernel(x, c, w_ada, b_ada, w_in, ln_v_g, ln_v_b, w_spatial, b_spatial, w_proj_a, w_proj_b, w_out,
           ln1_g, ln1_b, w_router, router_bias, w_gate_e, w_up_e, w_down_e, w_gate_sh, w_up_sh,
           w_down_sh, ln2_g, ln2_b):
    depth = w_ada.shape[0]
    alpha = (2.0 * depth) ** 0.25
    for l in range(depth):
        x = _layer(x, c, w_ada[l], b_ada[l], w_in[l], ln_v_g[l], ln_v_b[l], w_spatial[l], b_spatial[l],
                   w_proj_a[l], w_proj_b[l], w_out[l], ln1_g[l], ln1_b[l], w_router[l], router_bias[l],
                   w_gate_e[l], w_up_e[l], w_down_e[l], w_gate_sh[l], w_up_sh[l], w_down_sh[l],
                   ln2_g[l], ln2_b[l], alpha)
    return x
```

```python
import functools
import math

import jax
import jax.numpy as jnp
from jax import lax
from jax.experimental import pallas as pl
from jax.experimental.pallas import tpu as pltpu

F32 = jnp.float32
BF16 = jnp.bfloat16

N_HEADS = 8
HEAD_DIM = 64
DILATIONS = (1, 4, 16)
BAND = 128
ROPE_THETA = 10000.0
GMLP_GROUPS = 8
GMLP_CHUNK = 128
N_EXPERT_GROUPS = 8
TOPK_GROUPS = 4
TOP_K = 8
ROUTED_SCALE = 2.5
LN_EPS = 1e-5
LANES = 128
SUBLANES = 8
ROW_TILE = 8
VMEM_LIMIT = 56 * 1024 * 1024

TM_IN = 512
TM_POST = 256
TM_DISPATCH = 512
TM_COMBINE = 128
BM = 256
ATTN_BLOCKS_PER_TRIP = 4


def _cparams(sem):
    return pltpu.CompilerParams(dimension_semantics=sem, vmem_limit_bytes=VMEM_LIMIT)


def _const_spec(shape):
    nd = len(shape)
    return pl.BlockSpec(shape, lambda *_: (0,) * nd)


def _layer_norm(r, g, b):
    mu = jnp.mean(r, axis=-1, keepdims=True)
    d = r - mu
    var = jnp.mean(d * d, axis=-1, keepdims=True)
    return d * lax.rsqrt(var + LN_EPS) * g + b


def _gelu(t):
    return 0.5 * t * (1.0 + lax.erf(t * (1.0 / math.sqrt(2.0))))


def _sigmoid(t):
    return 1.0 / (1.0 + jnp.exp(-t))


def _store_token_rows(ref, v):
    m = v.shape[0]
    for s in range(ROW_TILE):
        ref[pl.ds(s, m, stride=ROW_TILE), :] = v[:, s * LANES:(s + 1) * LANES]


def _load_token_rows(ref, first_row, m):
    return jnp.concatenate([ref[pl.ds(first_row + s, m, stride=ROW_TILE), :] for s in range(ROW_TILE)], axis=1)


def _ada_kernel(c_ref, w_ref, b_ref, o_ref):
    c = c_ref[...]
    o_ref[...] = jnp.dot(c * _sigmoid(c), w_ref[...], preferred_element_type=F32) + b_ref[...]


def _ada(c, w, b):
    bsz, d = c.shape
    n = w.shape[1]
    tn = 1024
    return pl.pallas_call(
        _ada_kernel,
        grid=(n // tn,),
        in_specs=[_const_spec((bsz, d)), pl.BlockSpec((d, tn), lambda j: (0, j)),
                  pl.BlockSpec((1, tn), lambda j: (0, j))],
        out_specs=pl.BlockSpec((bsz, tn), lambda j: (0, j)),
        out_shape=jax.ShapeDtypeStruct((bsz, n), F32),
        compiler_params=_cparams(("arbitrary",)),
        name="ada",
    )(c, w, b.reshape(1, n))


def _inproj_kernel(x_ref, mod_ref, cos_ref, sin_ref, wq_ref, wk_ref, wv_ref, wu_ref, wvg_ref,
                   wga_ref, wgb_ref, lng_ref, lnb_ref, wsp_ref, bsp_ref, wpb_ref,
                   q_ref, k_ref, v_ref, ga_ref, gbyb_ref, u_scr, vv_scr, gated_scr):
    tm = x_ref.shape[0]
    shift = mod_ref[0, 0:1, :]
    scale = mod_ref[0, 1:2, :]
    h = (x_ref[...] * (1.0 + scale) + shift).astype(BF16)

    cos = cos_ref[...]
    sin = sin_ref[...]
    aw = cos.shape[1]
    lane = lax.broadcasted_iota(jnp.int32, (tm, aw), 1)
    first_half = (lane % HEAD_DIM) < (HEAD_DIM // 2)

    def rope(t):
        partner = jnp.where(first_half, pltpu.roll(t, aw - HEAD_DIM // 2, 1),
                            pltpu.roll(t, HEAD_DIM // 2, 1))
        return t * cos + partner * sin

    q = jnp.dot(h, wq_ref[...], preferred_element_type=F32)
    q_ref[...] = (rope(q) * (HEAD_DIM ** -0.5)).astype(BF16)
    k = jnp.dot(h, wk_ref[...], preferred_element_type=F32)
    k_ref[...] = rope(k).astype(BF16)
    v_ref[...] = jnp.dot(h, wv_ref[...], preferred_element_type=F32).astype(BF16)

    u_scr[...] = _gelu(jnp.dot(h, wu_ref[...], preferred_element_type=F32))
    vg = _gelu(jnp.dot(h, wvg_ref[...], preferred_element_type=F32))
    vv_scr[...] = _layer_norm(vg, lng_ref[...], lnb_ref[...]).astype(BF16)

    ti = lax.broadcasted_iota(jnp.int32, (GMLP_CHUNK, GMLP_CHUNK), 0)
    si = lax.broadcasted_iota(jnp.int32, (GMLP_CHUNK, GMLP_CHUNK), 1)
    causal = si <= ti
    gw = u_scr.shape[1] // GMLP_GROUPS
    for g in range(GMLP_GROUPS):
        ws = jnp.where(causal, wsp_ref[g], 0.0).astype(BF16)
        cols = slice(g * gw, (g + 1) * gw)
        for c in range(tm // GMLP_CHUNK):
            rows = slice(c * GMLP_CHUNK, (c + 1) * GMLP_CHUNK)
            sv = jnp.dot(ws, vv_scr[rows, cols], preferred_element_type=F32) + bsp_ref[:, cols]
            gated_scr[rows, cols] = (u_scr[rows, cols] * sv).astype(BF16)

    y_b = jnp.dot(gated_scr[...], wpb_ref[...], preferred_element_type=F32)
    g_b = _sigmoid(jnp.dot(h, wgb_ref[...], preferred_element_type=F32))
    gbyb_ref[...] = (g_b * y_b).astype(BF16)
    ga_ref[...] = _sigmoid(jnp.dot(h, wga_ref[...], preferred_element_type=F32)).astype(BF16)


def _inproj(xf, mod, cos, sin, wq, wk, wv, wu, wvg, wga, wgb, lng, lnb, wsp, bsp, wpb, seq):
    t, d = xf.shape
    aw = wq.shape[1]
    gwid = wu.shape[1]
    tm = TM_IN
    per_seq = seq // tm
    row = lambda i: (i, 0)
    tab = lambda i: (i % per_seq, 0)
    return pl.pallas_call(
        _inproj_kernel,
        grid=(t // tm,),
        in_specs=[pl.BlockSpec((tm, d), row),
                  pl.BlockSpec((1,) + mod.shape[1:], lambda i: (i // per_seq, 0, 0)),
                  pl.BlockSpec((tm, aw), tab), pl.BlockSpec((tm, aw), tab),
                  _const_spec(wq.shape), _const_spec(wk.shape), _const_spec(wv.shape),
                  _const_spec(wu.shape), _const_spec(wvg.shape), _const_spec(wga.shape),
                  _const_spec(wgb.shape), _const_spec(lng.shape), _const_spec(lnb.shape),
                  _const_spec(wsp.shape), _const_spec(bsp.shape), _const_spec(wpb.shape)],
        out_specs=[pl.BlockSpec((tm, aw), row)] * 3 + [pl.BlockSpec((tm, d), row)] * 2,
        out_shape=[jax.ShapeDtypeStruct((t, aw), BF16)] * 3 + [jax.ShapeDtypeStruct((t, d), BF16)] * 2,
        scratch_shapes=[pltpu.VMEM((tm, gwid), F32), pltpu.VMEM((tm, gwid), BF16),
                        pltpu.VMEM((tm, gwid), BF16)],
        compiler_params=_cparams(("arbitrary",)),
        name="inproj",
    )(xf, mod, cos, sin, wq, wk, wv, wu, wvg, wga, wgb, lng, lnb, wsp, bsp, wpb)


def _rows(start, size, stride):
    if stride == 1:
        return pl.ds(start, size)
    return pl.ds(start, size, stride=stride)


def _band_aligned(start):
    return start if isinstance(start, int) else pl.multiple_of(start, BAND)


def _interleaved_loop(count, load, compute, store):
    per_trip = max(u for u in range(1, ATTN_BLOCKS_PER_TRIP + 1) if count % u == 0) if count else 1

    def trip(it, c):
        items = [it * per_trip + u for u in range(per_trip)]
        results = [compute(ops) for ops in [load(i) for i in items]]
        for i, res in zip(items, results):
            store(i, res)
        return c

    if count == per_trip:
        trip(0, 0)
    elif count:
        lax.fori_loop(0, count // per_trip, trip, 0)


def _attn_kernel(*refs):
    npat = len(DILATIONS)
    qkv = [refs[3 * p:3 * p + 3] for p in range(npat)]
    o_ref, out_scr, lse_scr = refs[3 * npat:]
    seq = o_ref.shape[1]

    lane = lax.broadcasted_iota(jnp.int32, (BAND, LANES), 1)
    head0 = lane < HEAD_DIM
    neg = jnp.float32(-1e30)

    qi = lax.broadcasted_iota(jnp.int32, (BAND, 2 * BAND), 0)
    kj = lax.broadcasted_iota(jnp.int32, (BAND, 2 * BAND), 1)

    def load(p, blk, nb):
        q_ref, k_ref, v_ref = qkv[p]
        q_start = blk * BAND
        k_start = jnp.maximum(q_start - BAND, 0)
        off = q_start - k_start
        lowest = jnp.where(blk % nb == 0, off, qi)
        mask = (kj >= lowest) & (kj <= qi + off)
        krows = pl.ds(_band_aligned(k_start), 2 * BAND)
        return (q_ref[0, 0, pl.ds(_band_aligned(q_start), BAND), :], k_ref[0, 0, krows, :],
                v_ref[0, 0, krows, :], mask)

    def attend(ops):
        qb, kb, vb, mask = ops
        zero = jnp.zeros(qb.shape, qb.dtype)
        parts = []
        for hd in range(2):
            qh = jnp.where(head0 if hd == 0 else ~head0, qb, zero)
            s = lax.dot_general(qh, kb, (((1,), (1,)), ((), ())), preferred_element_type=F32)
            s = jnp.where(mask, s, neg)
            m = jnp.max(s, axis=-1, keepdims=True)
            pr = jnp.exp(s - m)
            l = jnp.sum(pr, axis=-1, keepdims=True)
            acc = jnp.dot(pr.astype(BF16), vb, preferred_element_type=F32)
            parts.append((acc / l, m + jnp.log(l)))
        return [jnp.where(head0, a0, a1) for a0, a1 in zip(*parts)]

    def store(p, seq_start, stride, res):
        rows = _rows(seq_start if stride > 1 else _band_aligned(seq_start), BAND, stride)
        out_scr[p, rows, :] = res[0]
        lse_scr[p, rows, :] = res[1]

    for p, dil in enumerate(DILATIONS):
        n = seq // dil
        nb = n // BAND

        _interleaved_loop(
            seq // BAND, lambda blk, p=p, nb=nb: load(p, blk, nb), attend,
            lambda blk, res, p=p, dil=dil, nb=nb: store(p, blk // nb + dil * BAND * (blk % nb), dil, res))

    top = jnp.maximum(jnp.maximum(lse_scr[0], lse_scr[1]), lse_scr[2])
    num = jnp.zeros(top.shape, F32)
    den = jnp.zeros(top.shape, F32)
    for p in range(npat):
        w = jnp.exp(lse_scr[p] - top)
        num = num + w * out_scr[p]
        den = den + w
    o_ref[0] = (num / den).astype(BF16)


def _class_major(t, dil):
    bsz, seq, aw = t.shape
    t = t.reshape(bsz, seq // dil, dil, aw // LANES, LANES)
    return t.transpose(0, 3, 2, 1, 4).reshape(bsz, aw // LANES, seq, LANES)


def _attention(q, k, v):
    bsz, seq, aw = q.shape
    npat = len(DILATIONS)
    blk = (1, 1, seq, LANES)
    operands, specs = [], []
    for dil in DILATIONS:
        for t in (q, k, v):
            if dil == 1:
                operands.append(t.reshape(bsz, 1, seq, aw))
                specs.append(pl.BlockSpec(blk, lambda b, hp: (b, 0, 0, hp)))
            else:
                operands.append(_class_major(t, dil))
                specs.append(pl.BlockSpec(blk, lambda b, hp: (b, hp, 0, 0)))
    return pl.pallas_call(
        _attn_kernel,
        grid=(bsz, aw // LANES),
        in_specs=specs,
        out_specs=pl.BlockSpec((1, seq, LANES), lambda b, hp: (b, 0, hp)),
        out_shape=jax.ShapeDtypeStruct((bsz, seq, aw), BF16),
        scratch_shapes=[pltpu.VMEM((npat, seq, LANES), F32)] * 2,
        compiler_params=_cparams(("arbitrary", "arbitrary")),
        name="attn",
    )(*operands)


def _post_kernel(alpha, oa_ref, ga_ref, gbyb_ref, x_ref, mod_ref, wpa_ref, wo_ref, l1g_ref, l1b_ref,
                 wrt_ref, rb_ref, wgs_ref, wus_ref, wds_ref, tri_ref,
                 h2_ref, base_ref, e_ref, w_ref, r_ref, cnt_ref):
    tm = x_ref.shape[0]
    n_exp = wrt_ref.shape[0]
    per_group = n_exp // N_EXPERT_GROUPS
    gate_a = mod_ref[0, 2:3, :]
    shift_f = mod_ref[0, 3:4, :]
    scale_f = mod_ref[0, 4:5, :]
    gate_f = mod_ref[0, 5:6, :]

    y_a = jnp.dot(oa_ref[...], wpa_ref[...], preferred_element_type=F32)
    merged = ga_ref[...].astype(F32) * y_a + gbyb_ref[...].astype(F32)
    mix = jnp.dot(merged.astype(BF16), wo_ref[...], preferred_element_type=F32)
    x1 = _layer_norm(alpha * x_ref[...] + gate_a * mix, l1g_ref[...], l1b_ref[...])
    h2 = x1 * (1.0 + scale_f) + shift_f
    h2b = h2.astype(BF16)

    _store_token_rows(h2_ref, h2)

    a = jnp.dot(h2b, wgs_ref[...], preferred_element_type=F32)
    b = jnp.dot(h2b, wus_ref[...], preferred_element_type=F32)
    shared = jnp.dot((a * _sigmoid(a) * b).astype(BF16), wds_ref[...], preferred_element_type=F32)
    base_ref[...] = alpha * x1 + gate_f * shared

    logits = lax.dot_general(wrt_ref[...], h2b, (((1,), (1,)), ((), ())), preferred_element_type=F32)
    scores = _sigmoid(logits)
    biased = scores + rb_ref[...]
    ninf = jnp.float32(-jnp.inf)

    sub = lax.broadcasted_iota(jnp.int32, (per_group, tm), 0)
    grp_rows = []
    for g in range(N_EXPERT_GROUPS):
        blk = biased[g * per_group:(g + 1) * per_group, :]
        m1 = jnp.max(blk, axis=0, keepdims=True)
        f1 = jnp.min(jnp.where(blk == m1, sub, per_group), axis=0, keepdims=True)
        m2 = jnp.max(jnp.where(sub == f1, ninf, blk), axis=0, keepdims=True)
        grp_rows.append(m1 + m2)
    grp = jnp.concatenate(grp_rows, axis=0)

    gi = lax.broadcasted_iota(jnp.int32, grp.shape, 0)
    chosen = jnp.zeros(grp.shape, jnp.bool_)
    cur = grp
    for _ in range(TOPK_GROUPS):
        m = jnp.max(cur, axis=0, keepdims=True)
        f = jnp.min(jnp.where(cur == m, gi, N_EXPERT_GROUPS), axis=0, keepdims=True)
        hit = gi == f
        chosen = chosen | hit
        cur = jnp.where(hit, ninf, cur)
    emask = jnp.concatenate(
        [jnp.broadcast_to(chosen[g:g + 1, :], (per_group, tm)) for g in range(N_EXPERT_GROUPS)], axis=0)

    ei = lax.broadcasted_iota(jnp.int32, (n_exp, tm), 0)
    cand = jnp.where(emask, biased, ninf)
    hits, ids, wts = [], [], []
    for _ in range(TOP_K):
        m = jnp.max(cand, axis=0, keepdims=True)
        f = jnp.min(jnp.where(cand == m, ei, n_exp), axis=0, keepdims=True)
        hit = ei == f
        wts.append(jnp.sum(jnp.where(hit, scores, 0.0), axis=0, keepdims=True))
        cand = jnp.where(hit, ninf, cand)
        hits.append(hit)
        ids.append(f)
    onehot = hits[0]
    for hit in hits[1:]:
        onehot = onehot | hit
    onehot = jnp.where(onehot, 1.0, 0.0)

    @pl.when(pl.program_id(0) == 0)
    def _():
        cnt_ref[...] = jnp.zeros(cnt_ref.shape, F32)

    before = jnp.dot(onehot.astype(BF16), tri_ref[...], preferred_element_type=F32) + cnt_ref[...]
    ranks = [jnp.sum(jnp.where(hit, before, 0.0), axis=0, keepdims=True) for hit in hits]
    cnt_ref[...] += jnp.sum(onehot, axis=1, keepdims=True)

    denom = wts[0]
    for wk in wts[1:]:
        denom = denom + wk
    e_ref[...] = jnp.concatenate(ids, axis=0)
    w_ref[...] = jnp.concatenate([ROUTED_SCALE * wk / denom for wk in wts], axis=0)
    r_ref[...] = jnp.concatenate(ranks, axis=0).astype(jnp.int32)


def _post(alpha, oa, ga, gbyb, xf, mod, wpa, wo, l1g, l1b, wrt, rb, wgs, wus, wds, seq):
    t, d = xf.shape
    aw = oa.shape[1]
    n_exp = wrt.shape[0]
    tm = TM_POST
    per_seq = seq // tm
    row = lambda i: (i, 0)
    col = lambda i: (0, i)
    tri = (jnp.arange(tm)[:, None] < jnp.arange(tm)[None, :]).astype(BF16)
    return pl.pallas_call(
        functools.partial(_post_kernel, alpha),
        grid=(t // tm,),
        in_specs=[pl.BlockSpec((tm, aw), row), pl.BlockSpec((tm, d), row), pl.BlockSpec((tm, d), row),
                  pl.BlockSpec((tm, d), row),
                  pl.BlockSpec((1,) + mod.shape[1:], lambda i: (i // per_seq, 0, 0)),
                  _const_spec(wpa.shape), _const_spec(wo.shape), _const_spec(l1g.shape),
                  _const_spec(l1b.shape), _const_spec(wrt.shape), _const_spec(rb.shape),
                  _const_spec(wgs.shape), _const_spec(wus.shape), _const_spec(wds.shape),
                  _const_spec(tri.shape)],
        out_specs=[pl.BlockSpec((tm * ROW_TILE, LANES), row), pl.BlockSpec((tm, d), row),
                   pl.BlockSpec((TOP_K, tm), col), pl.BlockSpec((TOP_K, tm), col),
                   pl.BlockSpec((TOP_K, tm), col), _const_spec((n_exp, 1))],
        out_shape=[jax.ShapeDtypeStruct((t * ROW_TILE, LANES), F32), jax.ShapeDtypeStruct((t, d), F32),
                   jax.ShapeDtypeStruct((TOP_K, t), jnp.int32), jax.ShapeDtypeStruct((TOP_K, t), F32),
                   jax.ShapeDtypeStruct((TOP_K, t), jnp.int32), jax.ShapeDtypeStruct((n_exp, 1), F32)],
        compiler_params=_cparams(("arbitrary",)),
        name="post",
    )(oa, ga, gbyb, xf, mod, wpa, wo, l1g, l1b, wrt, rb, wgs, wus, wds, tri)


def _row_copy(src_ref, src_row, dst_ref, dst_row, sem):
    return pltpu.make_async_copy(
        src_ref.at[pl.ds(pl.multiple_of(src_row * ROW_TILE, ROW_TILE), ROW_TILE), :],
        dst_ref.at[pl.ds(pl.multiple_of(dst_row * ROW_TILE, ROW_TILE), ROW_TILE), :], sem)


def _dispatch_kernel(n_blocks, rs_ref, ps_ref, pl_ref, nu_ref, e_ref, r_ref, h2_ref, xs_ref, dest_ref,
                     zbuf, sem, zsem):
    i = pl.program_id(0)
    n_steps = pl.num_programs(0)
    tm = e_ref.shape[1]
    n_exp = rs_ref.shape[0]
    per_step = -(-n_exp // n_steps)
    zbuf[...] = jnp.zeros(zbuf.shape, F32)

    def zero_fill(act):
        for q in range(per_step):
            e = jnp.minimum(i * per_step + q, n_exp - 1)
            live = i * per_step + q < n_exp
            gap = pl_ref[e]
            for bit in [1 << s for s in range(BM.bit_length() - 1)]:
                @pl.when(live & ((gap & bit) != 0))
                def _(bit=bit, e=e, gap=gap):
                    off = ps_ref[e] + (gap & ~(2 * bit - 1))
                    act(pltpu.make_async_copy(
                        zbuf.at[pl.ds(0, bit * ROW_TILE), :],
                        xs_ref.at[pl.ds(pl.multiple_of(off * ROW_TILE, ROW_TILE), bit * ROW_TILE), :], zsem))
            blk = nu_ref[0] + i * per_step + q
            @pl.when(blk < n_blocks)
            def _(blk=blk):
                rows = BM * ROW_TILE
                act(pltpu.make_async_copy(
                    zbuf, xs_ref.at[pl.ds(pl.multiple_of(blk * rows, rows), rows), :], zsem))

    zero_fill(lambda cp: cp.start(priority=1))

    def issue(j, c):
        for k in range(TOP_K):
            d = rs_ref[e_ref[k, j]] + r_ref[k, j]
            dest_ref[k, j] = d
            _row_copy(h2_ref, j, xs_ref, d, sem).start(priority=k % 2)
        return c

    lax.fori_loop(0, tm, issue, 0)
    n = tm * TOP_K * ROW_TILE
    pltpu.make_async_copy(xs_ref.at[pl.ds(0, n), :], xs_ref.at[pl.ds(0, n), :], sem).wait()
    zero_fill(lambda cp: cp.wait())


def _dispatch(row_start, pad_start, pad_len, n_used, e_idx, rank, h2t, n_blocks):
    t = e_idx.shape[1]
    tm = TM_DISPATCH
    col = lambda i, *_: (0, i)
    smem_blk = pl.BlockSpec((TOP_K, tm), col, memory_space=pltpu.SMEM)
    grid_spec = pltpu.PrefetchScalarGridSpec(
        num_scalar_prefetch=4,
        grid=(t // tm,),
        in_specs=[smem_blk, smem_blk, pl.BlockSpec((tm * ROW_TILE, LANES), lambda i, *_: (i, 0))],
        out_specs=[pl.BlockSpec(memory_space=pl.ANY), smem_blk],
        scratch_shapes=[pltpu.VMEM((BM * ROW_TILE, LANES), F32), pltpu.SemaphoreType.DMA,
                        pltpu.SemaphoreType.DMA],
    )
    return pl.pallas_call(
        functools.partial(_dispatch_kernel, n_blocks),
        grid_spec=grid_spec,
        out_shape=[jax.ShapeDtypeStruct((n_blocks * BM * ROW_TILE, LANES), F32),
                   jax.ShapeDtypeStruct((TOP_K, t), jnp.int32)],
        compiler_params=_cparams(("arbitrary",)),
        name="dispatch",
    )(row_start, pad_start, pad_len, n_used, e_idx, rank, h2t)


def _expert_kernel(be_ref, nu_ref, x_ref, wg_ref, wu_ref, wd_ref, y_ref):
    @pl.when(pl.program_id(0) < nu_ref[0])
    def _():
        x = _load_token_rows(x_ref, 0, BM).astype(BF16)
        a = jnp.dot(x, wg_ref[0].astype(BF16), preferred_element_type=F32)
        b = jnp.dot(x, wu_ref[0].astype(BF16), preferred_element_type=F32)
        act = (a * _sigmoid(a) * b).astype(BF16)
        _store_token_rows(y_ref, jnp.dot(act, wd_ref[0].astype(BF16), preferred_element_type=F32))


def _experts(block_e, n_used, xs, wg, wu, wd):
    n_blocks = block_e.shape[0]
    n_exp, d, de = wg.shape
    blk = lambda i, be, nu: (jnp.minimum(i, nu[0] - 1), 0)
    wsel = lambda i, be, nu: (be[jnp.minimum(i, nu[0] - 1)], 0, 0)
    grid_spec = pltpu.PrefetchScalarGridSpec(
        num_scalar_prefetch=2,
        grid=(n_blocks,),
        in_specs=[pl.BlockSpec((BM * ROW_TILE, LANES), blk),
                  pl.BlockSpec((1, d, de), wsel), pl.BlockSpec((1, d, de), wsel),
                  pl.BlockSpec((1, de, d), wsel)],
        out_specs=pl.BlockSpec((BM * ROW_TILE, LANES), blk),
    )
    return pl.pallas_call(
        _expert_kernel,
        grid_spec=grid_spec,
        out_shape=jax.ShapeDtypeStruct(xs.shape, xs.dtype),
        input_output_aliases={2: 0},
        compiler_params=_cparams(("arbitrary",)),
        name="experts",
    )(block_e, n_used, xs, wg, wu, wd)


def _combine_kernel(dest_ref, next_dest_ref, w_ref, base_ref, mod_ref, g_ref, b_ref, ys_ref, o_ref, buf, sems):
    i = pl.program_id(0)
    tm = dest_ref.shape[1]
    slab = tm * ROW_TILE
    slot_rows = TOP_K * slab

    def gather(idx_ref, slot):
        def issue(j, c):
            for k in range(TOP_K):
                _row_copy(ys_ref, idx_ref[k, j], buf, slot * (TOP_K * tm) + k * tm + j,
                          sems.at[slot]).start(priority=k % 2)
            return c
        lax.fori_loop(0, tm, issue, 0)

    @pl.when(i == 0)
    def _():
        gather(dest_ref, 0)

    @pl.when(i + 1 < pl.num_programs(0))
    def _():
        gather(next_dest_ref, (i + 1) % 2)

    slot = i % 2
    base_row = pl.multiple_of(slot * slot_rows, slot_rows)
    pltpu.make_async_copy(ys_ref.at[pl.ds(0, slot_rows), :], buf.at[pl.ds(base_row, slot_rows), :],
                          sems.at[slot]).wait()

    wpad = jnp.concatenate([w_ref[...], jnp.zeros((LANES - TOP_K, tm), F32)], axis=0)
    wcol = wpad.T
    routed = jnp.zeros(base_ref.shape, F32)
    for k in range(TOP_K):
        routed = routed + wcol[:, k:k + 1] * _load_token_rows(buf, base_row + k * slab, tm)
    gate_f = mod_ref[0, 5:6, :]
    o_ref[...] = _layer_norm(base_ref[...] + gate_f * routed, g_ref[...], b_ref[...])


def _combine(dest, wts, base, mod, g, b, ys, seq):
    t, d = base.shape
    tm = TM_COMBINE
    per_seq = seq // tm
    n_steps = t // tm
    col = lambda i: (0, i)
    row = lambda i: (i, 0)
    return pl.pallas_call(
        _combine_kernel,
        grid=(n_steps,),
        in_specs=[pl.BlockSpec((TOP_K, tm), col, memory_space=pltpu.SMEM),
                  pl.BlockSpec((TOP_K, tm), lambda i: (0, jnp.minimum(i + 1, n_steps - 1)),
                               memory_space=pltpu.SMEM),
                  pl.BlockSpec((TOP_K, tm), col), pl.BlockSpec((tm, d), row),
                  pl.BlockSpec((1,) + mod.shape[1:], lambda i: (i // per_seq, 0, 0)),
                  _const_spec(g.shape), _const_spec(b.shape),
                  pl.BlockSpec(memory_space=pl.ANY)],
        out_specs=pl.BlockSpec((tm, d), row),
        out_shape=jax.ShapeDtypeStruct((t, d), F32),
        scratch_shapes=[pltpu.VMEM((2 * TOP_K * tm * ROW_TILE, LANES), F32), pltpu.SemaphoreType.DMA((2,))],
        compiler_params=_cparams(("arbitrary",)),
        name="combine",
    )(dest, dest, wts, base, mod, g, b, ys)


def _rope_tables(seq):
    half = HEAD_DIM // 2
    inv_freq = ROPE_THETA ** (-jnp.arange(half, dtype=F32) / half)
    ang = jnp.arange(seq, dtype=F32)[:, None] * inv_freq[None, :]
    cos = jnp.concatenate([jnp.cos(ang), jnp.cos(ang)], axis=1)
    sin = jnp.concatenate([-jnp.sin(ang), jnp.sin(ang)], axis=1)
    return jnp.tile(cos, (1, N_HEADS)), jnp.tile(sin, (1, N_HEADS))


def _layer(x, c, w_ada, b_ada, w_in, ln_v_g, ln_v_b, w_spatial, b_spatial, w_proj_a, w_proj_b, w_out,
           ln1_g, ln1_b, w_router, router_bias, w_gate_e, w_up_e, w_down_e, w_gate_sh, w_up_sh,
           w_down_sh, ln2_g, ln2_b, alpha):
    bsz, seq, d = x.shape
    assert seq % (max(DILATIONS) * BAND) == 0 and seq % TM_IN == 0 and d == ROW_TILE * LANES
    t = bsz * seq
    aw = N_HEADS * HEAD_DIM
    gwid = ln_v_g.shape[0]
    n_exp = w_router.shape[1]
    xf = x.reshape(t, d)

    mod = _ada(c, w_ada, b_ada).reshape(bsz, 6, d)

    cuts = [0, aw, 2 * aw, 3 * aw, 3 * aw + gwid, 3 * aw + 2 * gwid, 3 * aw + 2 * gwid + d,
            3 * aw + 2 * gwid + 2 * d]
    wq, wk, wv, wu, wvg, wga, wgb = [w_in[:, a:b].astype(BF16) for a, b in zip(cuts[:-1], cuts[1:])]
    cos, sin = _rope_tables(seq)
    bsp = jnp.repeat(b_spatial.T, gwid // GMLP_GROUPS, axis=1)
    q, k, v, ga, gbyb = _inproj(xf, mod, cos, sin, wq, wk, wv, wu, wvg, wga, wgb,
                                ln_v_g.reshape(1, gwid), ln_v_b.reshape(1, gwid), w_spatial, bsp,
                                w_proj_b.astype(BF16), seq)

    oa = _attention(q.reshape(bsz, seq, aw), k.reshape(bsz, seq, aw), v.reshape(bsz, seq, aw))

    h2t, base, e_idx, wts, rank, cnt = _post(
        alpha, oa.reshape(t, aw), ga, gbyb, xf, mod, w_proj_a.astype(BF16), w_out.astype(BF16),
        ln1_g.reshape(1, d), ln1_b.reshape(1, d), w_router.T.astype(BF16), router_bias.reshape(n_exp, 1),
        w_gate_sh.astype(BF16), w_up_sh.astype(BF16), w_down_sh.astype(BF16), seq)

    counts = cnt[:, 0].astype(jnp.int32)
    nblk = (counts + BM - 1) // BM
    blk_end = jnp.cumsum(nblk)
    blk_start = blk_end - nblk
    n_blocks = (t * TOP_K) // BM + n_exp
    bidx = jnp.arange(n_blocks, dtype=jnp.int32)
    block_e = jnp.minimum(jnp.sum(blk_end[None, :] <= bidx[:, None], axis=1), n_exp - 1).astype(jnp.int32)
    n_used = blk_end[-1:].astype(jnp.int32)
    row_start = (blk_start * BM).astype(jnp.int32)

    xs, dest = _dispatch(row_start, row_start + counts, nblk * BM - counts, n_used, e_idx, rank, h2t, n_blocks)
    ys = _experts(block_e, n_used, xs, w_gate_e, w_up_e, w_down_e)
    out = _combine(dest, wts, base, mod, ln2_g.reshape(1, d), ln2_b.reshape(1, d), ys, seq)
    return out.reshape(bsz, seq, d)


def kernel(x, c, w_ada, b_ada, w_in, ln_v_g, ln_v_b, w_spatial, b_spatial, w_proj_a, w_proj_b, w_out,
           ln1_g, ln1_b, w_router, router_bias, w_gate_e, w_up_e, w_down_e, w_gate_sh, w_up_sh,
           w_down_sh, ln2_g, ln2_b):
    depth = w_ada.shape[0]
    alpha = (2.0 * depth) ** 0.25
    for l in range(depth):
        x = _layer(x, c, w_ada[l], b_ada[l], w_in[l], ln_v_g[l], ln_v_b[l], w_spatial[l], b_spatial[l],
                   w_proj_a[l], w_proj_b[l], w_out[l], ln1_g[l], ln1_b[l], w_router[l], router_bias[l],
                   w_gate_e[l], w_up_e[l], w_down_e[l], w_gate_sh[l], w_up_sh[l], w_down_sh[l],
                   ln2_g[l], ln2_b[l], alpha)
    return x
```

```python
import functools
import math

import jax
import jax.numpy as jnp
from jax import lax
from jax.experimental import pallas as pl
from jax.experimental.pallas import tpu as pltpu

F32 = jnp.float32
BF16 = jnp.bfloat16

N_HEADS = 8
HEAD_DIM = 64
DILATIONS = (1, 4, 16)
BAND = 128
ROPE_THETA = 10000.0
GMLP_GROUPS = 8
GMLP_CHUNK = 128
N_EXPERT_GROUPS = 8
TOPK_GROUPS = 4
TOP_K = 8
ROUTED_SCALE = 2.5
LN_EPS = 1e-5
LANES = 128
SUBLANES = 8
ROW_TILE = 8
VMEM_LIMIT = 56 * 1024 * 1024

TM_IN = 512
TM_POST = 256
TM_DISPATCH = 512
TM_COMBINE = 128
BM = 256
ATTN_BLOCKS_PER_TRIP = 4


def _cparams(sem):
    return pltpu.CompilerParams(dimension_semantics=sem, vmem_limit_bytes=VMEM_LIMIT)


def _const_spec(shape):
    nd = len(shape)
    return pl.BlockSpec(shape, lambda *_: (0,) * nd)


def _layer_norm(r, g, b):
    mu = jnp.mean(r, axis=-1, keepdims=True)
    d = r - mu
    var = jnp.mean(d * d, axis=-1, keepdims=True)
    return d * lax.rsqrt(var + LN_EPS) * g + b


def _gelu(t):
    return 0.5 * t * (1.0 + lax.erf(t * (1.0 / math.sqrt(2.0))))


def _sigmoid(t):
    return 1.0 / (1.0 + jnp.exp(-t))


def _store_token_rows(ref, v):
    m = v.shape[0]
    for s in range(ROW_TILE):
        ref[pl.ds(s, m, stride=ROW_TILE), :] = v[:, s * LANES:(s + 1) * LANES]


def _load_token_rows(ref, first_row, m):
    return jnp.concatenate([ref[pl.ds(first_row + s, m, stride=ROW_TILE), :] for s in range(ROW_TILE)], axis=1)


def _ada_kernel(c_ref, w_ref, b_ref, o_ref):
    c = c_ref[...]
    o_ref[...] = jnp.dot(c * _sigmoid(c), w_ref[...], preferred_element_type=F32) + b_ref[...]


def _ada(c, w, b):
    bsz, d = c.shape
    n = w.shape[1]
    tn = 1024
    return pl.pallas_call(
        _ada_kernel,
        grid=(n // tn,),
        in_specs=[_const_spec((bsz, d)), pl.BlockSpec((d, tn), lambda j: (0, j)),
                  pl.BlockSpec((1, tn), lambda j: (0, j))],
        out_specs=pl.BlockSpec((bsz, tn), lambda j: (0, j)),
        out_shape=jax.ShapeDtypeStruct((bsz, n), F32),
        compiler_params=_cparams(("arbitrary",)),
        name="ada",
    )(c, w, b.reshape(1, n))


def _inproj_kernel(x_ref, mod_ref, cos_ref, sin_ref, wq_ref, wk_ref, wv_ref, wu_ref, wvg_ref,
                   wga_ref, wgb_ref, lng_ref, lnb_ref, wsp_ref, bsp_ref, wpb_ref,
                   q_ref, k_ref, v_ref, ga_ref, gbyb_ref, u_scr, vv_scr, gated_scr):
    tm = x_ref.shape[0]
    shift = mod_ref[0, 0:1, :]
    scale = mod_ref[0, 1:2, :]
    h = (x_ref[...] * (1.0 + scale) + shift).astype(BF16)

    cos = cos_ref[...]
    sin = sin_ref[...]
    aw = cos.shape[1]
    lane = lax.broadcasted_iota(jnp.int32, (tm, aw), 1)
    first_half = (lane % HEAD_DIM) < (HEAD_DIM // 2)

    def rope(t):
        partner = jnp.where(first_half, pltpu.roll(t, aw - HEAD_DIM // 2, 1),
                            pltpu.roll(t, HEAD_DIM // 2, 1))
        return t * cos + partner * sin

    q = jnp.dot(h, wq_ref[...], preferred_element_type=F32)
    q_ref[...] = (rope(q) * (HEAD_DIM ** -0.5)).astype(BF16)
    k = jnp.dot(h, wk_ref[...], preferred_element_type=F32)
    k_ref[...] = rope(k).astype(BF16)
    v_ref[...] = jnp.dot(h, wv_ref[...], preferred_element_type=F32).astype(BF16)

    u_scr[...] = _gelu(jnp.dot(h, wu_ref[...], preferred_element_type=F32))
    vg = _gelu(jnp.dot(h, wvg_ref[...], preferred_element_type=F32))
    vv_scr[...] = _layer_norm(vg, lng_ref[...], lnb_ref[...]).astype(BF16)

    ti = lax.broadcasted_iota(jnp.int32, (GMLP_CHUNK, GMLP_CHUNK), 0)
    si = lax.broadcasted_iota(jnp.int32, (GMLP_CHUNK, GMLP_CHUNK), 1)
    causal = si <= ti
    gw = u_scr.shape[1] // GMLP_GROUPS
    for g in range(GMLP_GROUPS):
        ws = jnp.where(causal, wsp_ref[g], 0.0).astype(BF16)
        cols = slice(g * gw, (g + 1) * gw)
        for c in range(tm // GMLP_CHUNK):
            rows = slice(c * GMLP_CHUNK, (c + 1) * GMLP_CHUNK)
            sv = jnp.dot(ws, vv_scr[rows, cols], preferred_element_type=F32) + bsp_ref[:, cols]
            gated_scr[rows, cols] = (u_scr[rows, cols] * sv).astype(BF16)

    y_b = jnp.dot(gated_scr[...], wpb_ref[...], preferred_element_type=F32)
    g_b = _sigmoid(jnp.dot(h, wgb_ref[...], preferred_element_type=F32))
    gbyb_ref[...] = (g_b * y_b).astype(BF16)
    ga_ref[...] = _sigmoid(jnp.dot(h, wga_ref[...], preferred_element_type=F32)).astype(BF16)


def _inproj(xf, mod, cos, sin, wq, wk, wv, wu, wvg, wga, wgb, lng, lnb, wsp, bsp, wpb, seq):
    t, d = xf.shape
    aw = wq.shape[1]
    gwid = wu.shape[1]
    tm = TM_IN
    per_seq = seq // tm
    row = lambda i: (i, 0)
    tab = lambda i: (i % per_seq, 0)
    return pl.pallas_call(
        _inproj_kernel,
        grid=(t // tm,),
        in_specs=[pl.BlockSpec((tm, d), row),
                  pl.BlockSpec((1,) + mod.shape[1:], lambda i: (i // per_seq, 0, 0)),
                  pl.BlockSpec((tm, aw), tab), pl.BlockSpec((tm, aw), tab),
                  _const_spec(wq.shape), _const_spec(wk.shape), _const_spec(wv.shape),
                  _const_spec(wu.shape), _const_spec(wvg.shape), _const_spec(wga.shape),
                  _const_spec(wgb.shape), _const_spec(lng.shape), _const_spec(lnb.shape),
                  _const_spec(wsp.shape), _const_spec(bsp.shape), _const_spec(wpb.shape)],
        out_specs=[pl.BlockSpec((tm, aw), row)] * 3 + [pl.BlockSpec((tm, d), row)] * 2,
        out_shape=[jax.ShapeDtypeStruct((t, aw), BF16)] * 3 + [jax.ShapeDtypeStruct((t, d), BF16)] * 2,
        scratch_shapes=[pltpu.VMEM((tm, gwid), F32), pltpu.VMEM((tm, gwid), BF16),
                        pltpu.VMEM((tm, gwid), BF16)],
        compiler_params=_cparams(("arbitrary",)),
        name="inproj",
    )(xf, mod, cos, sin, wq, wk, wv, wu, wvg, wga, wgb, lng, lnb, wsp, bsp, wpb)


def _rows(start, size, stride):
    if stride == 1:
        return pl.ds(start, size)
    return pl.ds(start, size, stride=stride)


def _band_aligned(start):
    return start if isinstance(start, int) else pl.multiple_of(start, BAND)


def _interleaved_loop(count, load, compute, store):
    per_trip = max(u for u in range(1, ATTN_BLOCKS_PER_TRIP + 1) if count % u == 0) if count else 1

    def trip(it, c):
        items = [it * per_trip + u for u in range(per_trip)]
        results = [compute(ops) for ops in [load(i) for i in items]]
        for i, res in zip(items, results):
            store(i, res)
        return c

    if count == per_trip:
        trip(0, 0)
    elif count:
        lax.fori_loop(0, count // per_trip, trip, 0)


def _attn_kernel(*refs):
    npat = len(DILATIONS)
    qkv = [refs[3 * p:3 * p + 3] for p in range(npat)]
    o_ref, out_scr, lse_scr = refs[3 * npat:]
    seq = o_ref.shape[1]

    lane = lax.broadcasted_iota(jnp.int32, (BAND, LANES), 1)
    head0 = lane < HEAD_DIM
    neg = jnp.float32(-1e30)

    qi = lax.broadcasted_iota(jnp.int32, (BAND, 2 * BAND), 0)
    kj = lax.broadcasted_iota(jnp.int32, (BAND, 2 * BAND), 1)

    def load(p, blk, nb):
        q_ref, k_ref, v_ref = qkv[p]
        q_start = blk * BAND
        k_start = jnp.maximum(q_start - BAND, 0)
        off = q_start - k_start
        lowest = jnp.where(blk % nb == 0, off, qi)
        mask = (kj >= lowest) & (kj <= qi + off)
        krows = pl.ds(_band_aligned(k_start), 2 * BAND)
        return (q_ref[0, 0, pl.ds(_band_aligned(q_start), BAND), :], k_ref[0, 0, krows, :],
                v_ref[0, 0, krows, :], mask)

    def attend(ops):
        qb, kb, vb, mask = ops
        zero = jnp.zeros(qb.shape, qb.dtype)
        parts = []
        for hd in range(2):
            qh = jnp.where(head0 if hd == 0 else ~head0, qb, zero)
            s = lax.dot_general(qh, kb, (((1,), (1,)), ((), ())), preferred_element_type=F32)
            s = jnp.where(mask, s, neg)
            m = jnp.max(s, axis=-1, keepdims=True)
            pr = jnp.exp(s - m)
            l = jnp.sum(pr, axis=-1, keepdims=True)
            acc = jnp.dot(pr.astype(BF16), vb, preferred_element_type=F32)
            parts.append((acc / l, m + jnp.log(l)))
        return [jnp.where(head0, a0, a1) for a0, a1 in zip(*parts)]

    def store(p, seq_start, stride, res):
        rows = _rows(seq_start if stride > 1 else _band_aligned(seq_start), BAND, stride)
        out_scr[p, rows, :] = res[0]
        lse_scr[p, rows, :] = res[1]

    for p, dil in enumerate(DILATIONS):
        n = seq // dil
        nb = n // BAND

        _interleaved_loop(
            seq // BAND, lambda blk, p=p, nb=nb: load(p, blk, nb), attend,
            lambda blk, res, p=p, dil=dil, nb=nb: store(p, blk // nb + dil * BAND * (blk % nb), dil, res))

    top = jnp.maximum(jnp.maximum(lse_scr[0], lse_scr[1]), lse_scr[2])
    num = jnp.zeros(top.shape, F32)
    den = jnp.zeros(top.shape, F32)
    for p in range(npat):
        w = jnp.exp(lse_scr[p] - top)
        num = num + w * out_scr[p]
        den = den + w
    o_ref[0] = (num / den).astype(BF16)


def _class_major(t, dil):
    bsz, seq, aw = t.shape
    t = t.reshape(bsz, seq // dil, dil, aw // LANES, LANES)
    return t.transpose(0, 3, 2, 1, 4).reshape(bsz, aw // LANES, seq, LANES)


def _attention(q, k, v):
    bsz, seq, aw = q.shape
    npat = len(DILATIONS)
    blk = (1, 1, seq, LANES)
    operands, specs = [], []
    for dil in DILATIONS:
        for t in (q, k, v):
            if dil == 1:
                operands.append(t.reshape(bsz, 1, seq, aw))
                specs.append(pl.BlockSpec(blk, lambda b, hp: (b, 0, 0, hp)))
            else:
                operands.append(_class_major(t, dil))
                specs.append(pl.BlockSpec(blk, lambda b, hp: (b, hp, 0, 0)))
    return pl.pallas_call(
        _attn_kernel,
        grid=(bsz, aw // LANES),
        in_specs=specs,
        out_specs=pl.BlockSpec((1, seq, LANES), lambda b, hp: (b, 0, hp)),
        out_shape=jax.ShapeDtypeStruct((bsz, seq, aw), BF16),
        scratch_shapes=[pltpu.VMEM((npat, seq, LANES), F32)] * 2,
        compiler_params=_cparams(("arbitrary", "arbitrary")),
        name="attn",
    )(*operands)


def _post_kernel(alpha, oa_ref, ga_ref, gbyb_ref, x_ref, mod_ref, wpa_ref, wo_ref, l1g_ref, l1b_ref,
                 wrt_ref, rb_ref, wgs_ref, wus_ref, wds_ref, tri_ref,
                 h2_ref, base_ref, e_ref, w_ref, r_ref, cnt_ref):
    tm = x_ref.shape[0]
    n_exp = wrt_ref.shape[0]
    per_group = n_exp // N_EXPERT_GROUPS
    gate_a = mod_ref[0, 2:3, :]
    shift_f = mod_ref[0, 3:4, :]
    scale_f = mod_ref[0, 4:5, :]
    gate_f = mod_ref[0, 5:6, :]

    y_a = jnp.dot(oa_ref[...], wpa_ref[...], preferred_element_type=F32)
    merged = ga_ref[...].astype(F32) * y_a + gbyb_ref[...].astype(F32)
    mix = jnp.dot(merged.astype(BF16), wo_ref[...], preferred_element_type=F32)
    x1 = _layer_norm(alpha * x_ref[...] + gate_a * mix, l1g_ref[...], l1b_ref[...])
    h2 = x1 * (1.0 + scale_f) + shift_f
    h2b = h2.astype(BF16)

    _store_token_rows(h2_ref, h2)

    a = jnp.dot(h2b, wgs_ref[...], preferred_element_type=F32)
    b = jnp.dot(h2b, wus_ref[...], preferred_element_type=F32)
    shared = jnp.dot((a * _sigmoid(a) * b).astype(BF16), wds_ref[...], preferred_element_type=F32)
    base_ref[...] = alpha * x1 + gate_f * shared

    logits = lax.dot_general(wrt_ref[...], h2b, (((1,), (1,)), ((), ())), preferred_element_type=F32)
    scores = _sigmoid(logits)
    biased = scores + rb_ref[...]
    ninf = jnp.float32(-jnp.inf)

    sub = lax.broadcasted_iota(jnp.int32, (per_group, tm), 0)
    grp_rows = []
    for g in range(N_EXPERT_GROUPS):
        blk = biased[g * per_group:(g + 1) * per_group, :]
        m1 = jnp.max(blk, axis=0, keepdims=True)
        f1 = jnp.min(jnp.where(blk == m1, sub, per_group), axis=0, keepdims=True)
        m2 = jnp.max(jnp.where(sub == f1, ninf, blk), axis=0, keepdims=True)
        grp_rows.append(m1 + m2)
    grp = jnp.concatenate(grp_rows, axis=0)

    gi = lax.broadcasted_iota(jnp.int32, grp.shape, 0)
    chosen = jnp.zeros(grp.shape, jnp.bool_)
    cur = grp
    for _ in range(TOPK_GROUPS):
        m = jnp.max(cur, axis=0, keepdims=True)
        f = jnp.min(jnp.where(cur == m, gi, N_EXPERT_GROUPS), axis=0, keepdims=True)
        hit = gi == f
        chosen = chosen | hit
        cur = jnp.where(hit, ninf, cur)
    emask = jnp.concatenate(
        [jnp.broadcast_to(chosen[g:g + 1, :], (per_group, tm)) for g in range(N_EXPERT_GROUPS)], axis=0)

    ei = lax.broadcasted_iota(jnp.int32, (n_exp, tm), 0)
    cand = jnp.where(emask, biased, ninf)
    hits, ids, wts = [], [], []
    for _ in range(TOP_K):
        m = jnp.max(cand, axis=0, keepdims=True)
        f = jnp.min(jnp.where(cand == m, ei, n_exp), axis=0, keepdims=True)
        hit = ei == f
        wts.append(jnp.sum(jnp.where(hit, scores, 0.0), axis=0, keepdims=True))
        cand = jnp.where(hit, ninf, cand)
        hits.append(hit)
        ids.append(f)
    onehot = hits[0]
    for hit in hits[1:]:
        onehot = onehot | hit
    onehot = jnp.where(onehot, 1.0, 0.0)

    @pl.when(pl.program_id(0) == 0)
    def _():
        cnt_ref[...] = jnp.zeros(cnt_ref.shape, F32)

    before = jnp.dot(onehot.astype(BF16), tri_ref[...], preferred_element_type=F32) + cnt_ref[...]
    ranks = [jnp.sum(jnp.where(hit, before, 0.0), axis=0, keepdims=True) for hit in hits]
    cnt_ref[...] += jnp.sum(onehot, axis=1, keepdims=True)

    denom = wts[0]
    for wk in wts[1:]:
        denom = denom + wk
    e_ref[...] = jnp.concatenate(ids, axis=0)
    w_ref[...] = jnp.concatenate([ROUTED_SCALE * wk / denom for wk in wts], axis=0)
    r_ref[...] = jnp.concatenate(ranks, axis=0).astype(jnp.int32)


def _post(alpha, oa, ga, gbyb, xf, mod, wpa, wo, l1g, l1b, wrt, rb, wgs, wus, wds, seq):
    t, d = xf.shape
    aw = oa.shape[1]
    n_exp = wrt.shape[0]
    tm = TM_POST
    per_seq = seq // tm
    row = lambda i: (i, 0)
    col = lambda i: (0, i)
    tri = (jnp.arange(tm)[:, None] < jnp.arange(tm)[None, :]).astype(BF16)
    return pl.pallas_call(
        functools.partial(_post_kernel, alpha),
        grid=(t // tm,),
        in_specs=[pl.BlockSpec((tm, aw), row), pl.BlockSpec((tm, d), row), pl.BlockSpec((tm, d), row),
                  pl.BlockSpec((tm, d), row),
                  pl.BlockSpec((1,) + mod.shape[1:], lambda i: (i // per_seq, 0, 0)),
                  _const_spec(wpa.shape), _const_spec(wo.shape), _const_spec(l1g.shape),
                  _const_spec(l1b.shape), _const_spec(wrt.shape), _const_spec(rb.shape),
                  _const_spec(wgs.shape), _const_spec(wus.shape), _const_spec(wds.shape),
                  _const_spec(tri.shape)],
        out_specs=[pl.BlockSpec((tm * ROW_TILE, LANES), row), pl.BlockSpec((tm, d), row),
                   pl.BlockSpec((TOP_K, tm), col), pl.BlockSpec((TOP_K, tm), col),
                   pl.BlockSpec((TOP_K, tm), col), _const_spec((n_exp, 1))],
        out_shape=[jax.ShapeDtypeStruct((t * ROW_TILE, LANES), F32), jax.ShapeDtypeStruct((t, d), F32),
                   jax.ShapeDtypeStruct((TOP_K, t), jnp.int32), jax.ShapeDtypeStruct((TOP_K, t), F32),
                   jax.ShapeDtypeStruct((TOP_K, t), jnp.int32), jax.ShapeDtypeStruct((n_exp, 1), F32)],
        compiler_params=_cparams(("arbitrary",)),
        name="post",
    )(oa, ga, gbyb, xf, mod, wpa, wo, l1g, l1b, wrt, rb, wgs, wus, wds, tri)


def _row_copy(src_ref, src_row, dst_ref, dst_row, sem):
    return pltpu.make_async_copy(
        src_ref.at[pl.ds(pl.multiple_of(src_row * ROW_TILE, ROW_TILE), ROW_TILE), :],
        dst_ref.at[pl.ds(pl.multiple_of(dst_row * ROW_TILE, ROW_TILE), ROW_TILE), :], sem)


def _dest_kernel(rs_ref, e_ref, r_ref, o_ref):
    e = e_ref[...]
    start = lax.fori_loop(0, rs_ref.shape[0], lambda x, acc: jnp.where(e == x, rs_ref[x], acc),
                          jnp.zeros(e.shape, jnp.int32))
    o_ref[...] = start + r_ref[...]


def _dest(row_start, e_idx, rank):
    t = e_idx.shape[1]
    tm = min(t, 2048)
    blk = pl.BlockSpec((TOP_K, tm), lambda i, *_: (0, i))
    return pl.pallas_call(
        _dest_kernel,
        grid_spec=pltpu.PrefetchScalarGridSpec(num_scalar_prefetch=1, grid=(t // tm,), in_specs=[blk, blk],
                                               out_specs=blk),
        out_shape=jax.ShapeDtypeStruct((TOP_K, t), jnp.int32),
        compiler_params=_cparams(("arbitrary",)),
        name="dest",
    )(row_start, e_idx, rank)


def _dispatch_kernel(n_blocks, ps_ref, pl_ref, nu_ref, dest_ref, h2_ref, xs_ref, zbuf, sem, zsem):
    i = pl.program_id(0)
    n_steps = pl.num_programs(0)
    tm = dest_ref.shape[1]
    n_exp = ps_ref.shape[0]
    per_step = -(-n_exp // n_steps)
    zbuf[...] = jnp.zeros(zbuf.shape, F32)

    def zero_fill(act):
        for q in range(per_step):
            e = jnp.minimum(i * per_step + q, n_exp - 1)
            live = i * per_step + q < n_exp
            gap = pl_ref[e]
            for bit in [1 << s for s in range(BM.bit_length() - 1)]:
                @pl.when(live & ((gap & bit) != 0))
                def _(bit=bit, e=e, gap=gap):
                    off = ps_ref[e] + (gap & ~(2 * bit - 1))
                    act(pltpu.make_async_copy(
                        zbuf.at[pl.ds(0, bit * ROW_TILE), :],
                        xs_ref.at[pl.ds(pl.multiple_of(off * ROW_TILE, ROW_TILE), bit * ROW_TILE), :], zsem))
            blk = nu_ref[0] + i * per_step + q
            @pl.when(blk < n_blocks)
            def _(blk=blk):
                rows = BM * ROW_TILE
                act(pltpu.make_async_copy(
                    zbuf, xs_ref.at[pl.ds(pl.multiple_of(blk * rows, rows), rows), :], zsem))

    zero_fill(lambda cp: cp.start(priority=1))

    def issue(j, c):
        for k in range(TOP_K):
            _row_copy(h2_ref, j, xs_ref, dest_ref[k, j], sem).start(priority=k % 2)
        return c

    lax.fori_loop(0, tm, issue, 0)
    n = tm * TOP_K * ROW_TILE
    pltpu.make_async_copy(xs_ref.at[pl.ds(0, n), :], xs_ref.at[pl.ds(0, n), :], sem).wait()
    zero_fill(lambda cp: cp.wait())


def _dispatch(pad_start, pad_len, n_used, dest, h2t, n_blocks):
    t = dest.shape[1]
    tm = TM_DISPATCH
    grid_spec = pltpu.PrefetchScalarGridSpec(
        num_scalar_prefetch=3,
        grid=(t // tm,),
        in_specs=[pl.BlockSpec((TOP_K, tm), lambda i, *_: (0, i), memory_space=pltpu.SMEM),
                  pl.BlockSpec((tm * ROW_TILE, LANES), lambda i, *_: (i, 0))],
        out_specs=pl.BlockSpec(memory_space=pl.ANY),
        scratch_shapes=[pltpu.VMEM((BM * ROW_TILE, LANES), F32), pltpu.SemaphoreType.DMA,
                        pltpu.SemaphoreType.DMA],
    )
    return pl.pallas_call(
        functools.partial(_dispatch_kernel, n_blocks),
        grid_spec=grid_spec,
        out_shape=jax.ShapeDtypeStruct((n_blocks * BM * ROW_TILE, LANES), F32),
        compiler_params=_cparams(("arbitrary",)),
        name="dispatch",
    )(pad_start, pad_len, n_used, dest, h2t)


def _expert_kernel(nblk_ref, bstart_ref, widx_ref, nu_ref, xs_ref, wg_ref, wu_ref, wd_ref, ys_ref,
                   xbuf, ybuf, wg_b, wu_b, wd_b, xsem, ysem):
    e = pl.program_id(0)
    n_used = nu_ref[0]
    rows = BM * ROW_TILE

    def slot_base(g):
        return pl.multiple_of((g % 2) * rows, rows)

    def hbm_rows(ref, g):
        return ref.at[pl.ds(pl.multiple_of(g * rows, rows), rows), :]

    def x_copy(g):
        return pltpu.make_async_copy(hbm_rows(xs_ref, g), xbuf.at[pl.ds(slot_base(g), rows), :], xsem.at[g % 2])

    def y_copy(g):
        return pltpu.make_async_copy(ybuf.at[pl.ds(slot_base(g), rows), :], hbm_rows(ys_ref, g), ysem.at[g % 2])

    @pl.when((e == 0) & (n_used > 0))
    def _():
        x_copy(0).start()

    wg_b[...] = wg_ref[0].astype(BF16)
    wu_b[...] = wu_ref[0].astype(BF16)
    wd_b[...] = wd_ref[0].astype(BF16)

    def block(b, c):
        g = bstart_ref[e] + b
        x_copy(g).wait()

        @pl.when(g + 1 < n_used)
        def _():
            x_copy(g + 1).start()

        @pl.when(g >= 2)
        def _():
            y_copy(g - 2).wait()

        x = _load_token_rows(xbuf, slot_base(g), BM).astype(BF16)
        a = jnp.dot(x, wg_b[...], preferred_element_type=F32)
        u = jnp.dot(x, wu_b[...], preferred_element_type=F32)
        act = (a * _sigmoid(a) * u).astype(BF16)
        y = jnp.dot(act, wd_b[...], preferred_element_type=F32)
        for s in range(ROW_TILE):
            ybuf[pl.ds(slot_base(g) + s, BM, stride=ROW_TILE), :] = y[:, s * LANES:(s + 1) * LANES]
        y_copy(g).start()
        return c

    lax.fori_loop(0, nblk_ref[e], block, 0)

    @pl.when(e == pl.num_programs(0) - 1)
    def _():
        for back in (2, 1):
            @pl.when(n_used >= back)
            def _(back=back):
                y_copy(n_used - back).wait()


def _experts(nblk, blk_start, w_idx, n_used, xs, wg, wu, wd):
    n_exp, d, de = wg.shape
    rows = BM * ROW_TILE
    wsel = lambda e, nb, bs, wi, nu: (wi[e], 0, 0)
    grid_spec = pltpu.PrefetchScalarGridSpec(
        num_scalar_prefetch=4,
        grid=(n_exp,),
        in_specs=[pl.BlockSpec(memory_space=pl.ANY),
                  pl.BlockSpec((1, d, de), wsel), pl.BlockSpec((1, d, de), wsel),
                  pl.BlockSpec((1, de, d), wsel)],
        out_specs=pl.BlockSpec(memory_space=pl.ANY),
        scratch_shapes=[pltpu.VMEM((2 * rows, LANES), F32), pltpu.VMEM((2 * rows, LANES), F32),
                        pltpu.VMEM((d, de), BF16), pltpu.VMEM((d, de), BF16), pltpu.VMEM((de, d), BF16),
                        pltpu.SemaphoreType.DMA((2,)), pltpu.SemaphoreType.DMA((2,))],
    )
    return pl.pallas_call(
        _expert_kernel,
        grid_spec=grid_spec,
        out_shape=jax.ShapeDtypeStruct(xs.shape, xs.dtype),
        input_output_aliases={4: 0},
        compiler_params=_cparams(("arbitrary",)),
        name="experts",
    )(nblk, blk_start, w_idx, n_used, xs, wg, wu, wd)


def _combine_kernel(dest_ref, next_dest_ref, w_ref, base_ref, mod_ref, g_ref, b_ref, ys_ref, o_ref, buf, sems):
    i = pl.program_id(0)
    tm = dest_ref.shape[1]
    slab = tm * ROW_TILE
    slot_rows = TOP_K * slab

    def gather(idx_ref, slot):
        def issue(j, c):
            for k in range(TOP_K):
                _row_copy(ys_ref, idx_ref[k, j], buf, slot * (TOP_K * tm) + k * tm + j,
                          sems.at[slot]).start(priority=k % 2)
            return c
        lax.fori_loop(0, tm, issue, 0)

    @pl.when(i == 0)
    def _():
        gather(dest_ref, 0)

    @pl.when(i + 1 < pl.num_programs(0))
    def _():
        gather(next_dest_ref, (i + 1) % 2)

    slot = i % 2
    base_row = pl.multiple_of(slot * slot_rows, slot_rows)
    pltpu.make_async_copy(ys_ref.at[pl.ds(0, slot_rows), :], buf.at[pl.ds(base_row, slot_rows), :],
                          sems.at[slot]).wait()

    wpad = jnp.concatenate([w_ref[...], jnp.zeros((LANES - TOP_K, tm), F32)], axis=0)
    wcol = wpad.T
    routed = jnp.zeros(base_ref.shape, F32)
    for k in range(TOP_K):
        routed = routed + wcol[:, k:k + 1] * _load_token_rows(buf, base_row + k * slab, tm)
    gate_f = mod_ref[0, 5:6, :]
    o_ref[...] = _layer_norm(base_ref[...] + gate_f * routed, g_ref[...], b_ref[...])


def _combine(dest, wts, base, mod, g, b, ys, seq):
    t, d = base.shape
    tm = TM_COMBINE
    per_seq = seq // tm
    n_steps = t // tm
    col = lambda i: (0, i)
    row = lambda i: (i, 0)
    return pl.pallas_call(
        _combine_kernel,
        grid=(n_steps,),
        in_specs=[pl.BlockSpec((TOP_K, tm), col, memory_space=pltpu.SMEM),
                  pl.BlockSpec((TOP_K, tm), lambda i: (0, jnp.minimum(i + 1, n_steps - 1)),
                               memory_space=pltpu.SMEM),
                  pl.BlockSpec((TOP_K, tm), col), pl.BlockSpec((tm, d), row),
                  pl.BlockSpec((1,) + mod.shape[1:], lambda i: (i // per_seq, 0, 0)),
                  _const_spec(g.shape), _const_spec(b.shape),
                  pl.BlockSpec(memory_space=pl.ANY)],
        out_specs=pl.BlockSpec((tm, d), row),
        out_shape=jax.ShapeDtypeStruct((t, d), F32),
        scratch_shapes=[pltpu.VMEM((2 * TOP_K * tm * ROW_TILE, LANES), F32), pltpu.SemaphoreType.DMA((2,))],
        compiler_params=_cparams(("arbitrary",)),
        name="combine",
    )(dest, dest, wts, base, mod, g, b, ys)


def _rope_tables(seq):
    half = HEAD_DIM // 2
    inv_freq = ROPE_THETA ** (-jnp.arange(half, dtype=F32) / half)
    ang = jnp.arange(seq, dtype=F32)[:, None] * inv_freq[None, :]
    cos = jnp.concatenate([jnp.cos(ang), jnp.cos(ang)], axis=1)
    sin = jnp.concatenate([-jnp.sin(ang), jnp.sin(ang)], axis=1)
    return jnp.tile(cos, (1, N_HEADS)), jnp.tile(sin, (1, N_HEADS))


def _layer(x, c, w_ada, b_ada, w_in, ln_v_g, ln_v_b, w_spatial, b_spatial, w_proj_a, w_proj_b, w_out,
           ln1_g, ln1_b, w_router, router_bias, w_gate_e, w_up_e, w_down_e, w_gate_sh, w_up_sh,
           w_down_sh, ln2_g, ln2_b, alpha):
    bsz, seq, d = x.shape
    assert seq % (max(DILATIONS) * BAND) == 0 and seq % TM_IN == 0 and d == ROW_TILE * LANES
    t = bsz * seq
    aw = N_HEADS * HEAD_DIM
    gwid = ln_v_g.shape[0]
    n_exp = w_router.shape[1]
    xf = x.reshape(t, d)

    mod = _ada(c, w_ada, b_ada).reshape(bsz, 6, d)

    cuts = [0, aw, 2 * aw, 3 * aw, 3 * aw + gwid, 3 * aw + 2 * gwid, 3 * aw + 2 * gwid + d,
            3 * aw + 2 * gwid + 2 * d]
    wq, wk, wv, wu, wvg, wga, wgb = [w_in[:, a:b].astype(BF16) for a, b in zip(cuts[:-1], cuts[1:])]
    cos, sin = _rope_tables(seq)
    bsp = jnp.repeat(b_spatial.T, gwid // GMLP_GROUPS, axis=1)
    q, k, v, ga, gbyb = _inproj(xf, mod, cos, sin, wq, wk, wv, wu, wvg, wga, wgb,
                                ln_v_g.reshape(1, gwid), ln_v_b.reshape(1, gwid), w_spatial, bsp,
                                w_proj_b.astype(BF16), seq)

    oa = _attention(q.reshape(bsz, seq, aw), k.reshape(bsz, seq, aw), v.reshape(bsz, seq, aw))

    h2t, base, e_idx, wts, rank, cnt = _post(
        alpha, oa.reshape(t, aw), ga, gbyb, xf, mod, w_proj_a.astype(BF16), w_out.astype(BF16),
        ln1_g.reshape(1, d), ln1_b.reshape(1, d), w_router.T.astype(BF16), router_bias.reshape(n_exp, 1),
        w_gate_sh.astype(BF16), w_up_sh.astype(BF16), w_down_sh.astype(BF16), seq)

    counts = cnt[:, 0].astype(jnp.int32)
    nblk = ((counts + BM - 1) // BM).astype(jnp.int32)
    blk_end = jnp.cumsum(nblk).astype(jnp.int32)
    blk_start = blk_end - nblk
    n_blocks = (t * TOP_K) // BM + n_exp
    n_used = blk_end[-1:]
    row_start = blk_start * BM
    ids = jnp.arange(n_exp, dtype=jnp.int32)
    seen = jnp.where((ids[None, :] <= ids[:, None]) & (nblk[None, :] > 0), ids[None, :], -1).max(axis=1)
    w_idx = jnp.where(seen >= 0, seen, jnp.where(nblk > 0, ids, n_exp - 1).min()).astype(jnp.int32)

    dest = _dest(row_start, e_idx, rank)
    xs = _dispatch(row_start + counts, nblk * BM - counts, n_used, dest, h2t, n_blocks)
    ys = _experts(nblk, blk_start, w_idx, n_used, xs, w_gate_e, w_up_e, w_down_e)
    out = _combine(dest, wts, base, mod, ln2_g.reshape(1, d), ln2_b.reshape(1, d), ys, seq)
    return out.reshape(bsz, seq, d)


def kernel(x, c, w_ada, b_ada, w_in, ln_v_g, ln_v_b, w_spatial, b_spatial, w_proj_a, w_proj_b, w_out,
           ln1_g, ln1_b, w_router, router_bias, w_gate_e, w_up_e, w_down_e, w_gate_sh, w_up_sh,
           w_down_sh, ln2_g, ln2_b):
    depth = w_ada.shape[0]
    alpha = (2.0 * depth) ** 0.25
    for l in range(depth):
        x = _layer(x, c, w_ada[l], b_ada[l], w_in[l], ln_v_g[l], ln_v_b[l], w_spatial[l], b_spatial[l],
                   w_proj_a[l], w_proj_b[l], w_out[l], ln1_g[l], ln1_b[l], w_router[l], router_bias[l],
                   w_gate_e[l], w_up_e[l], w_down_e[l], w_gate_sh[l], w_up_sh[l], w_down_sh[l],
                   ln2_g[l], ln2_b[l], alpha)
    return x
```

```python
import functools
import math

import jax
import jax.numpy as jnp
from jax import lax
from jax.experimental import pallas as pl
from jax.experimental.pallas import tpu as pltpu

F32 = jnp.float32
BF16 = jnp.bfloat16

N_HEADS = 8
HEAD_DIM = 64
DILATIONS = (1, 4, 16)
BAND = 128
ROPE_THETA = 10000.0
GMLP_GROUPS = 8
GMLP_CHUNK = 128
N_EXPERT_GROUPS = 8
TOPK_GROUPS = 4
TOP_K = 8
ROUTED_SCALE = 2.5
LN_EPS = 1e-5
LANES = 128
SUBLANES = 8
VMEM_LIMIT = 56 * 1024 * 1024

TM_IN = 512
TM_POST = 256
TM_DISPATCH = 512
TM_COMBINE = 128
BM = 256
ATTN_BLOCKS_PER_TRIP = 4


def _cparams(sem):
    return pltpu.CompilerParams(dimension_semantics=sem, vmem_limit_bytes=VMEM_LIMIT)


def _const_spec(shape):
    nd = len(shape)
    return pl.BlockSpec(shape, lambda *_: (0,) * nd)


def _layer_norm(r, g, b):
    mu = jnp.mean(r, axis=-1, keepdims=True)
    d = r - mu
    var = jnp.mean(d * d, axis=-1, keepdims=True)
    return d * lax.rsqrt(var + LN_EPS) * g + b


def _gelu(t):
    return 0.5 * t * (1.0 + lax.erf(t * (1.0 / math.sqrt(2.0))))


def _sigmoid(t):
    return 1.0 / (1.0 + jnp.exp(-t))


def _ada_kernel(c_ref, w_ref, b_ref, o_ref):
    c = c_ref[...]
    o_ref[...] = jnp.dot(c * _sigmoid(c), w_ref[...], preferred_element_type=F32) + b_ref[...]


def _ada(c, w, b):
    bsz, d = c.shape
    n = w.shape[1]
    tn = 1024
    return pl.pallas_call(
        _ada_kernel,
        grid=(n // tn,),
        in_specs=[_const_spec((bsz, d)), pl.BlockSpec((d, tn), lambda j: (0, j)),
                  pl.BlockSpec((1, tn), lambda j: (0, j))],
        out_specs=pl.BlockSpec((bsz, tn), lambda j: (0, j)),
        out_shape=jax.ShapeDtypeStruct((bsz, n), F32),
        compiler_params=_cparams(("arbitrary",)),
        name="ada",
    )(c, w, b.reshape(1, n))


def _inproj_kernel(x_ref, mod_ref, cos_ref, sin_ref, wq_ref, wk_ref, wv_ref, wu_ref, wvg_ref,
                   wga_ref, wgb_ref, lng_ref, lnb_ref, wsp_ref, bsp_ref, wpb_ref,
                   q_ref, k_ref, v_ref, ga_ref, gbyb_ref, u_scr, vv_scr, gated_scr):
    tm = x_ref.shape[0]
    shift = mod_ref[0, 0:1, :]
    scale = mod_ref[0, 1:2, :]
    h = (x_ref[...] * (1.0 + scale) + shift).astype(BF16)

    cos = cos_ref[...]
    sin = sin_ref[...]
    aw = cos.shape[1]
    lane = lax.broadcasted_iota(jnp.int32, (tm, aw), 1)
    first_half = (lane % HEAD_DIM) < (HEAD_DIM // 2)

    def rope(t):
        partner = jnp.where(first_half, pltpu.roll(t, aw - HEAD_DIM // 2, 1),
                            pltpu.roll(t, HEAD_DIM // 2, 1))
        return t * cos + partner * sin

    q = jnp.dot(h, wq_ref[...], preferred_element_type=F32)
    q_ref[...] = (rope(q) * (HEAD_DIM ** -0.5)).astype(BF16)
    k = jnp.dot(h, wk_ref[...], preferred_element_type=F32)
    k_ref[...] = rope(k).astype(BF16)
    v_ref[...] = jnp.dot(h, wv_ref[...], preferred_element_type=F32).astype(BF16)

    u_scr[...] = _gelu(jnp.dot(h, wu_ref[...], preferred_element_type=F32))
    vg = _gelu(jnp.dot(h, wvg_ref[...], preferred_element_type=F32))
    vv_scr[...] = _layer_norm(vg, lng_ref[...], lnb_ref[...]).astype(BF16)

    ti = lax.broadcasted_iota(jnp.int32, (GMLP_CHUNK, GMLP_CHUNK), 0)
    si = lax.broadcasted_iota(jnp.int32, (GMLP_CHUNK, GMLP_CHUNK), 1)
    causal = si <= ti
    gw = u_scr.shape[1] // GMLP_GROUPS
    for g in range(GMLP_GROUPS):
        ws = jnp.where(causal, wsp_ref[g], 0.0).astype(BF16)
        cols = slice(g * gw, (g + 1) * gw)
        for c in range(tm // GMLP_CHUNK):
            rows = slice(c * GMLP_CHUNK, (c + 1) * GMLP_CHUNK)
            sv = jnp.dot(ws, vv_scr[rows, cols], preferred_element_type=F32) + bsp_ref[:, cols]
            gated_scr[rows, cols] = (u_scr[rows, cols] * sv).astype(BF16)

    y_b = jnp.dot(gated_scr[...], wpb_ref[...], preferred_element_type=F32)
    g_b = _sigmoid(jnp.dot(h, wgb_ref[...], preferred_element_type=F32))
    gbyb_ref[...] = (g_b * y_b).astype(BF16)
    ga_ref[...] = _sigmoid(jnp.dot(h, wga_ref[...], preferred_element_type=F32)).astype(BF16)


def _inproj(xf, mod, cos, sin, wq, wk, wv, wu, wvg, wga, wgb, lng, lnb, wsp, bsp, wpb, seq):
    t, d = xf.shape
    aw = wq.shape[1]
    gwid = wu.shape[1]
    tm = TM_IN
    per_seq = seq // tm
    row = lambda i: (i, 0)
    tab = lambda i: (i % per_seq, 0)
    return pl.pallas_call(
        _inproj_kernel,
        grid=(t // tm,),
        in_specs=[pl.BlockSpec((tm, d), row),
                  pl.BlockSpec((1,) + mod.shape[1:], lambda i: (i // per_seq, 0, 0)),
                  pl.BlockSpec((tm, aw), tab), pl.BlockSpec((tm, aw), tab),
                  _const_spec(wq.shape), _const_spec(wk.shape), _const_spec(wv.shape),
                  _const_spec(wu.shape), _const_spec(wvg.shape), _const_spec(wga.shape),
                  _const_spec(wgb.shape), _const_spec(lng.shape), _const_spec(lnb.shape),
                  _const_spec(wsp.shape), _const_spec(bsp.shape), _const_spec(wpb.shape)],
        out_specs=[pl.BlockSpec((tm, aw), row)] * 3 + [pl.BlockSpec((tm, d), row)] * 2,
        out_shape=[jax.ShapeDtypeStruct((t, aw), BF16)] * 3 + [jax.ShapeDtypeStruct((t, d), BF16)] * 2,
        scratch_shapes=[pltpu.VMEM((tm, gwid), F32), pltpu.VMEM((tm, gwid), BF16),
                        pltpu.VMEM((tm, gwid), BF16)],
        compiler_params=_cparams(("arbitrary",)),
        name="inproj",
    )(xf, mod, cos, sin, wq, wk, wv, wu, wvg, wga, wgb, lng, lnb, wsp, bsp, wpb)


def _rows(start, size, stride):
    if stride == 1:
        return pl.ds(start, size)
    return pl.ds(start, size, stride=stride)


def _band_aligned(start):
    return start if isinstance(start, int) else pl.multiple_of(start, BAND)


def _interleaved_loop(count, load, compute, store):
    per_trip = max(u for u in range(1, ATTN_BLOCKS_PER_TRIP + 1) if count % u == 0) if count else 1

    def trip(it, c):
        items = [it * per_trip + u for u in range(per_trip)]
        results = [compute(ops) for ops in [load(i) for i in items]]
        for i, res in zip(items, results):
            store(i, res)
        return c

    if count == per_trip:
        trip(0, 0)
    elif count:
        lax.fori_loop(0, count // per_trip, trip, 0)


def _attn_kernel(*refs):
    npat = len(DILATIONS)
    qkv = [refs[3 * p:3 * p + 3] for p in range(npat)]
    o_ref, out_scr, lse_scr = refs[3 * npat:]
    seq = o_ref.shape[1]

    lane = lax.broadcasted_iota(jnp.int32, (BAND, LANES), 1)
    head0 = lane < HEAD_DIM
    neg = jnp.float32(-1e30)

    qi = lax.broadcasted_iota(jnp.int32, (BAND, 2 * BAND), 0)
    kj = lax.broadcasted_iota(jnp.int32, (BAND, 2 * BAND), 1)

    def load(p, blk, nb):
        q_ref, k_ref, v_ref = qkv[p]
        q_start = blk * BAND
        k_start = jnp.maximum(q_start - BAND, 0)
        off = q_start - k_start
        lowest = jnp.where(blk % nb == 0, off, qi)
        mask = (kj >= lowest) & (kj <= qi + off)
        krows = pl.ds(_band_aligned(k_start), 2 * BAND)
        return (q_ref[0, 0, pl.ds(_band_aligned(q_start), BAND), :], k_ref[0, 0, krows, :],
                v_ref[0, 0, krows, :], mask)

    def attend(ops):
        qb, kb, vb, mask = ops
        zero = jnp.zeros(qb.shape, qb.dtype)
        parts = []
        for hd in range(2):
            qh = jnp.where(head0 if hd == 0 else ~head0, qb, zero)
            s = lax.dot_general(qh, kb, (((1,), (1,)), ((), ())), preferred_element_type=F32)
            s = jnp.where(mask, s, neg)
            m = jnp.max(s, axis=-1, keepdims=True)
            pr = jnp.exp(s - m)
            l = jnp.sum(pr, axis=-1, keepdims=True)
            acc = jnp.dot(pr.astype(BF16), vb, preferred_element_type=F32)
            parts.append((acc / l, m + jnp.log(l)))
        return [jnp.where(head0, a0, a1) for a0, a1 in zip(*parts)]

    def store(p, seq_start, stride, res):
        rows = _rows(seq_start if stride > 1 else _band_aligned(seq_start), BAND, stride)
        out_scr[p, rows, :] = res[0]
        lse_scr[p, rows, :] = res[1]

    for p, dil in enumerate(DILATIONS):
        n = seq // dil
        nb = n // BAND

        _interleaved_loop(
            seq // BAND, lambda blk, p=p, nb=nb: load(p, blk, nb), attend,
            lambda blk, res, p=p, dil=dil, nb=nb: store(p, blk // nb + dil * BAND * (blk % nb), dil, res))

    top = jnp.maximum(jnp.maximum(lse_scr[0], lse_scr[1]), lse_scr[2])
    num = jnp.zeros(top.shape, F32)
    den = jnp.zeros(top.shape, F32)
    for p in range(npat):
        w = jnp.exp(lse_scr[p] - top)
        num = num + w * out_scr[p]
        den = den + w
    o_ref[0] = (num / den).astype(BF16)


def _class_major(t, dil):
    bsz, seq, aw = t.shape
    t = t.reshape(bsz, seq // dil, dil, aw // LANES, LANES)
    return t.transpose(0, 3, 2, 1, 4).reshape(bsz, aw // LANES, seq, LANES)


def _attention(q, k, v):
    bsz, seq, aw = q.shape
    npat = len(DILATIONS)
    blk = (1, 1, seq, LANES)
    operands, specs = [], []
    for dil in DILATIONS:
        for t in (q, k, v):
            if dil == 1:
                operands.append(t.reshape(bsz, 1, seq, aw))
                specs.append(pl.BlockSpec(blk, lambda b, hp: (b, 0, 0, hp)))
            else:
                operands.append(_class_major(t, dil))
                specs.append(pl.BlockSpec(blk, lambda b, hp: (b, hp, 0, 0)))
    return pl.pallas_call(
        _attn_kernel,
        grid=(bsz, aw // LANES),
        in_specs=specs,
        out_specs=pl.BlockSpec((1, seq, LANES), lambda b, hp: (b, 0, hp)),
        out_shape=jax.ShapeDtypeStruct((bsz, seq, aw), BF16),
        scratch_shapes=[pltpu.VMEM((npat, seq, LANES), F32)] * 2,
        compiler_params=_cparams(("arbitrary", "arbitrary")),
        name="attn",
    )(*operands)


def _post_kernel(alpha, oa_ref, ga_ref, gbyb_ref, x_ref, mod_ref, wpa_ref, wo_ref, l1g_ref, l1b_ref,
                 wrt_ref, rb_ref, wgs_ref, wus_ref, wds_ref, tri_ref,
                 h2_ref, base_ref, e_ref, w_ref, r_ref, cnt_ref):
    tm = x_ref.shape[0]
    n_exp = wrt_ref.shape[0]
    per_group = n_exp // N_EXPERT_GROUPS
    gate_a = mod_ref[0, 2:3, :]
    shift_f = mod_ref[0, 3:4, :]
    scale_f = mod_ref[0, 4:5, :]
    gate_f = mod_ref[0, 5:6, :]

    y_a = jnp.dot(oa_ref[...], wpa_ref[...], preferred_element_type=F32)
    merged = ga_ref[...].astype(F32) * y_a + gbyb_ref[...].astype(F32)
    mix = jnp.dot(merged.astype(BF16), wo_ref[...], preferred_element_type=F32)
    x1 = _layer_norm(alpha * x_ref[...] + gate_a * mix, l1g_ref[...], l1b_ref[...])
    h2 = x1 * (1.0 + scale_f) + shift_f
    h2b = h2.astype(BF16)

    h2_ref[...] = h2

    a = jnp.dot(h2b, wgs_ref[...], preferred_element_type=F32)
    b = jnp.dot(h2b, wus_ref[...], preferred_element_type=F32)
    shared = jnp.dot((a * _sigmoid(a) * b).astype(BF16), wds_ref[...], preferred_element_type=F32)
    base_ref[...] = alpha * x1 + gate_f * shared

    logits = lax.dot_general(wrt_ref[...], h2b, (((1,), (1,)), ((), ())), preferred_element_type=F32)
    scores = _sigmoid(logits)
    biased = scores + rb_ref[...]
    ninf = jnp.float32(-jnp.inf)

    sub = lax.broadcasted_iota(jnp.int32, (per_group, tm), 0)
    grp_rows = []
    for g in range(N_EXPERT_GROUPS):
        blk = biased[g * per_group:(g + 1) * per_group, :]
        m1 = jnp.max(blk, axis=0, keepdims=True)
        f1 = jnp.min(jnp.where(blk == m1, sub, per_group), axis=0, keepdims=True)
        m2 = jnp.max(jnp.where(sub == f1, ninf, blk), axis=0, keepdims=True)
        grp_rows.append(m1 + m2)
    grp = jnp.concatenate(grp_rows, axis=0)

    gi = lax.broadcasted_iota(jnp.int32, grp.shape, 0)
    chosen = jnp.zeros(grp.shape, jnp.bool_)
    cur = grp
    for _ in range(TOPK_GROUPS):
        m = jnp.max(cur, axis=0, keepdims=True)
        f = jnp.min(jnp.where(cur == m, gi, N_EXPERT_GROUPS), axis=0, keepdims=True)
        hit = gi == f
        chosen = chosen | hit
        cur = jnp.where(hit, ninf, cur)
    emask = jnp.concatenate(
        [jnp.broadcast_to(chosen[g:g + 1, :], (per_group, tm)) for g in range(N_EXPERT_GROUPS)], axis=0)

    ei = lax.broadcasted_iota(jnp.int32, (n_exp, tm), 0)
    cand = jnp.where(emask, biased, ninf)
    hits, ids, wts = [], [], []
    for _ in range(TOP_K):
        m = jnp.max(cand, axis=0, keepdims=True)
        f = jnp.min(jnp.where(cand == m, ei, n_exp), axis=0, keepdims=True)
        hit = ei == f
        wts.append(jnp.sum(jnp.where(hit, scores, 0.0), axis=0, keepdims=True))
        cand = jnp.where(hit, ninf, cand)
        hits.append(hit)
        ids.append(f)
    onehot = hits[0]
    for hit in hits[1:]:
        onehot = onehot | hit
    onehot = jnp.where(onehot, 1.0, 0.0)

    @pl.when(pl.program_id(0) == 0)
    def _():
        cnt_ref[...] = jnp.zeros(cnt_ref.shape, F32)

    before = jnp.dot(onehot.astype(BF16), tri_ref[...], preferred_element_type=F32) + cnt_ref[...]
    ranks = [jnp.sum(jnp.where(hit, before, 0.0), axis=0, keepdims=True) for hit in hits]
    cnt_ref[...] += jnp.sum(onehot, axis=1, keepdims=True)

    denom = wts[0]
    for wk in wts[1:]:
        denom = denom + wk
    e_ref[...] = jnp.concatenate(ids, axis=0)
    w_ref[...] = jnp.concatenate([ROUTED_SCALE * wk / denom for wk in wts], axis=0)
    r_ref[...] = jnp.concatenate(ranks, axis=0).astype(jnp.int32)


def _post(alpha, oa, ga, gbyb, xf, mod, wpa, wo, l1g, l1b, wrt, rb, wgs, wus, wds, seq):
    t, d = xf.shape
    aw = oa.shape[1]
    n_exp = wrt.shape[0]
    tm = TM_POST
    per_seq = seq // tm
    row = lambda i: (i, 0)
    col = lambda i: (0, i)
    tri = (jnp.arange(tm)[:, None] < jnp.arange(tm)[None, :]).astype(BF16)
    return pl.pallas_call(
        functools.partial(_post_kernel, alpha),
        grid=(t // tm,),
        in_specs=[pl.BlockSpec((tm, aw), row), pl.BlockSpec((tm, d), row), pl.BlockSpec((tm, d), row),
                  pl.BlockSpec((tm, d), row),
                  pl.BlockSpec((1,) + mod.shape[1:], lambda i: (i // per_seq, 0, 0)),
                  _const_spec(wpa.shape), _const_spec(wo.shape), _const_spec(l1g.shape),
                  _const_spec(l1b.shape), _const_spec(wrt.shape), _const_spec(rb.shape),
                  _const_spec(wgs.shape), _const_spec(wus.shape), _const_spec(wds.shape),
                  _const_spec(tri.shape)],
        out_specs=[pl.BlockSpec((tm, d), row), pl.BlockSpec((tm, d), row),
                   pl.BlockSpec((TOP_K, tm), col), pl.BlockSpec((TOP_K, tm), col),
                   pl.BlockSpec((TOP_K, tm), col), _const_spec((n_exp, 1))],
        out_shape=[jax.ShapeDtypeStruct((t, d), F32), jax.ShapeDtypeStruct((t, d), F32),
                   jax.ShapeDtypeStruct((TOP_K, t), jnp.int32), jax.ShapeDtypeStruct((TOP_K, t), F32),
                   jax.ShapeDtypeStruct((TOP_K, t), jnp.int32), jax.ShapeDtypeStruct((n_exp, 1), F32)],
        compiler_params=_cparams(("arbitrary",)),
        name="post",
    )(oa, ga, gbyb, xf, mod, wpa, wo, l1g, l1b, wrt, rb, wgs, wus, wds, tri)


def _row_copy(src_ref, src_row, dst_ref, dst_row, sem):
    return pltpu.make_async_copy(src_ref.at[pl.ds(src_row, 1), :], dst_ref.at[pl.ds(dst_row, 1), :], sem)


def _dest_kernel(rs_ref, e_ref, r_ref, o_ref):
    e = e_ref[...]
    start = lax.fori_loop(0, rs_ref.shape[0], lambda x, acc: jnp.where(e == x, rs_ref[x], acc),
                          jnp.zeros(e.shape, jnp.int32))
    o_ref[...] = start + r_ref[...]


def _dest(row_start, e_idx, rank):
    t = e_idx.shape[1]
    tm = min(t, 2048)
    blk = pl.BlockSpec((TOP_K, tm), lambda i, *_: (0, i))
    return pl.pallas_call(
        _dest_kernel,
        grid_spec=pltpu.PrefetchScalarGridSpec(num_scalar_prefetch=1, grid=(t // tm,), in_specs=[blk, blk],
                                               out_specs=blk),
        out_shape=jax.ShapeDtypeStruct((TOP_K, t), jnp.int32),
        compiler_params=_cparams(("arbitrary",)),
        name="dest",
    )(row_start, e_idx, rank)


def _dispatch_kernel(n_blocks, ps_ref, pl_ref, nu_ref, dest_ref, h2_ref, xs_ref, zbuf, sem, zsem):
    i = pl.program_id(0)
    n_steps = pl.num_programs(0)
    tm = dest_ref.shape[1]
    n_exp = ps_ref.shape[0]
    per_step = -(-n_exp // n_steps)
    zbuf[...] = jnp.zeros(zbuf.shape, F32)

    def zero_fill(act):
        for q in range(per_step):
            e = jnp.minimum(i * per_step + q, n_exp - 1)
            live = i * per_step + q < n_exp
            gap = pl_ref[e]
            head = gap & (SUBLANES - 1)
            for h in range(SUBLANES - 1):
                @pl.when(live & (h < head))
                def _(h=h, e=e):
                    act(_row_copy(zbuf, 0, xs_ref, ps_ref[e] + h, zsem))
            for bit in [1 << s for s in range(SUBLANES.bit_length() - 1, BM.bit_length() - 1)]:
                @pl.when(live & ((gap & bit) != 0))
                def _(bit=bit, e=e, gap=gap, head=head):
                    off = pl.multiple_of(ps_ref[e] + head + ((gap - head) & ~(2 * bit - 1)), SUBLANES)
                    act(pltpu.make_async_copy(zbuf.at[pl.ds(0, bit), :], xs_ref.at[pl.ds(off, bit), :], zsem))
            blk = nu_ref[0] + i * per_step + q
            @pl.when(blk < n_blocks)
            def _(blk=blk):
                act(pltpu.make_async_copy(zbuf, xs_ref.at[pl.ds(pl.multiple_of(blk * BM, BM), BM), :], zsem))

    zero_fill(lambda cp: cp.start(priority=1))

    def issue(j, c):
        for k in range(TOP_K):
            _row_copy(h2_ref, j, xs_ref, dest_ref[k, j], sem).start(priority=k % 2)
        return c

    lax.fori_loop(0, tm, issue, 0)
    n = tm * TOP_K
    pltpu.make_async_copy(xs_ref.at[pl.ds(0, n), :], xs_ref.at[pl.ds(0, n), :], sem).wait()
    zero_fill(lambda cp: cp.wait())


def _dispatch(pad_start, pad_len, n_used, dest, h2, n_blocks):
    t, d = h2.shape
    tm = TM_DISPATCH
    grid_spec = pltpu.PrefetchScalarGridSpec(
        num_scalar_prefetch=3,
        grid=(t // tm,),
        in_specs=[pl.BlockSpec((TOP_K, tm), lambda i, *_: (0, i), memory_space=pltpu.SMEM),
                  pl.BlockSpec((tm, d), lambda i, *_: (i, 0))],
        out_specs=pl.BlockSpec(memory_space=pl.ANY),
        scratch_shapes=[pltpu.VMEM((BM, d), F32), pltpu.SemaphoreType.DMA, pltpu.SemaphoreType.DMA],
    )
    return pl.pallas_call(
        functools.partial(_dispatch_kernel, n_blocks),
        grid_spec=grid_spec,
        out_shape=jax.ShapeDtypeStruct((n_blocks * BM, d), F32),
        compiler_params=_cparams(("arbitrary",)),
        name="dispatch",
    )(pad_start, pad_len, n_used, dest, h2)


def _expert_kernel(nblk_ref, bstart_ref, widx_ref, nu_ref, xs_ref, wg_ref, wu_ref, wd_ref, ys_ref,
                   xbuf, ybuf, wg_b, wu_b, wd_b, xsem, ysem):
    e = pl.program_id(0)
    n_used = nu_ref[0]
    rows = BM

    def slot_base(g):
        return pl.multiple_of((g % 2) * rows, rows)

    def hbm_rows(ref, g):
        return ref.at[pl.ds(pl.multiple_of(g * rows, rows), rows), :]

    def x_copy(g):
        return pltpu.make_async_copy(hbm_rows(xs_ref, g), xbuf.at[pl.ds(slot_base(g), rows), :], xsem.at[g % 2])

    def y_copy(g):
        return pltpu.make_async_copy(ybuf.at[pl.ds(slot_base(g), rows), :], hbm_rows(ys_ref, g), ysem.at[g % 2])

    @pl.when((e == 0) & (n_used > 0))
    def _():
        x_copy(0).start()

    wg_b[...] = wg_ref[0].astype(BF16)
    wu_b[...] = wu_ref[0].astype(BF16)
    wd_b[...] = wd_ref[0].astype(BF16)

    def block(b, c):
        g = bstart_ref[e] + b
        x_copy(g).wait()

        @pl.when(g + 1 < n_used)
        def _():
            x_copy(g + 1).start()

        @pl.when(g >= 2)
        def _():
            y_copy(g - 2).wait()

        x = xbuf[pl.ds(slot_base(g), BM), :].astype(BF16)
        a = jnp.dot(x, wg_b[...], preferred_element_type=F32)
        u = jnp.dot(x, wu_b[...], preferred_element_type=F32)
        act = (a * _sigmoid(a) * u).astype(BF16)
        ybuf[pl.ds(slot_base(g), BM), :] = jnp.dot(act, wd_b[...], preferred_element_type=F32)
        y_copy(g).start()
        return c

    lax.fori_loop(0, nblk_ref[e], block, 0)

    @pl.when(e == pl.num_programs(0) - 1)
    def _():
        for back in (2, 1):
            @pl.when(n_used >= back)
            def _(back=back):
                y_copy(n_used - back).wait()


def _experts(nblk, blk_start, w_idx, n_used, xs, wg, wu, wd):
    n_exp, d, de = wg.shape
    wsel = lambda e, nb, bs, wi, nu: (wi[e], 0, 0)
    grid_spec = pltpu.PrefetchScalarGridSpec(
        num_scalar_prefetch=4,
        grid=(n_exp,),
        in_specs=[pl.BlockSpec(memory_space=pl.ANY),
                  pl.BlockSpec((1, d, de), wsel), pl.BlockSpec((1, d, de), wsel),
                  pl.BlockSpec((1, de, d), wsel)],
        out_specs=pl.BlockSpec(memory_space=pl.ANY),
        scratch_shapes=[pltpu.VMEM((2 * BM, d), F32), pltpu.VMEM((2 * BM, d), F32),
                        pltpu.VMEM((d, de), BF16), pltpu.VMEM((d, de), BF16), pltpu.VMEM((de, d), BF16),
                        pltpu.SemaphoreType.DMA((2,)), pltpu.SemaphoreType.DMA((2,))],
    )
    return pl.pallas_call(
        _expert_kernel,
        grid_spec=grid_spec,
        out_shape=jax.ShapeDtypeStruct(xs.shape, xs.dtype),
        input_output_aliases={4: 0},
        compiler_params=_cparams(("arbitrary",)),
        name="experts",
    )(nblk, blk_start, w_idx, n_used, xs, wg, wu, wd)


def _combine_kernel(dest_ref, next_dest_ref, w_ref, base_ref, mod_ref, g_ref, b_ref, ys_ref, o_ref, buf, sems):
    i = pl.program_id(0)
    tm = dest_ref.shape[1]
    slot_rows = TOP_K * tm

    def gather(idx_ref, slot):
        def issue(j, c):
            for k in range(TOP_K):
                _row_copy(ys_ref, idx_ref[k, j], buf, slot * slot_rows + k * tm + j,
                          sems.at[slot]).start(priority=k % 2)
            return c
        lax.fori_loop(0, tm, issue, 0)

    @pl.when(i == 0)
    def _():
        gather(dest_ref, 0)

    @pl.when(i + 1 < pl.num_programs(0))
    def _():
        gather(next_dest_ref, (i + 1) % 2)

    slot = i % 2
    base_row = pl.multiple_of(slot * slot_rows, slot_rows)
    pltpu.make_async_copy(ys_ref.at[pl.ds(0, slot_rows), :], buf.at[pl.ds(base_row, slot_rows), :],
                          sems.at[slot]).wait()

    wpad = jnp.concatenate([w_ref[...], jnp.zeros((LANES - TOP_K, tm), F32)], axis=0)
    wcol = wpad.T
    routed = jnp.zeros(base_ref.shape, F32)
    for k in range(TOP_K):
        routed = routed + wcol[:, k:k + 1] * buf[pl.ds(pl.multiple_of(base_row + k * tm, tm), tm), :]
    gate_f = mod_ref[0, 5:6, :]
    o_ref[...] = _layer_norm(base_ref[...] + gate_f * routed, g_ref[...], b_ref[...])


def _combine(dest, wts, base, mod, g, b, ys, seq):
    t, d = base.shape
    tm = TM_COMBINE
    per_seq = seq // tm
    n_steps = t // tm
    col = lambda i: (0, i)
    row = lambda i: (i, 0)
    return pl.pallas_call(
        _combine_kernel,
        grid=(n_steps,),
        in_specs=[pl.BlockSpec((TOP_K, tm), col, memory_space=pltpu.SMEM),
                  pl.BlockSpec((TOP_K, tm), lambda i: (0, jnp.minimum(i + 1, n_steps - 1)),
                               memory_space=pltpu.SMEM),
                  pl.BlockSpec((TOP_K, tm), col), pl.BlockSpec((tm, d), row),
                  pl.BlockSpec((1,) + mod.shape[1:], lambda i: (i // per_seq, 0, 0)),
                  _const_spec(g.shape), _const_spec(b.shape),
                  pl.BlockSpec(memory_space=pl.ANY)],
        out_specs=pl.BlockSpec((tm, d), row),
        out_shape=jax.ShapeDtypeStruct((t, d), F32),
        scratch_shapes=[pltpu.VMEM((2 * TOP_K * tm, d), F32), pltpu.SemaphoreType.DMA((2,))],
        compiler_params=_cparams(("arbitrary",)),
        name="combine",
    )(dest, dest, wts, base, mod, g, b, ys)


def _rope_tables(seq):
    half = HEAD_DIM // 2
    inv_freq = ROPE_THETA ** (-jnp.arange(half, dtype=F32) / half)
    ang = jnp.arange(seq, dtype=F32)[:, None] * inv_freq[None, :]
    cos = jnp.concatenate([jnp.cos(ang), jnp.cos(ang)], axis=1)
    sin = jnp.concatenate([-jnp.sin(ang), jnp.sin(ang)], axis=1)
    return jnp.tile(cos, (1, N_HEADS)), jnp.tile(sin, (1, N_HEADS))


def _layer(x, c, w_ada, b_ada, w_in, ln_v_g, ln_v_b, w_spatial, b_spatial, w_proj_a, w_proj_b, w_out,
           ln1_g, ln1_b, w_router, router_bias, w_gate_e, w_up_e, w_down_e, w_gate_sh, w_up_sh,
           w_down_sh, ln2_g, ln2_b, alpha):
    bsz, seq, d = x.shape
    assert seq % (max(DILATIONS) * BAND) == 0 and seq % TM_IN == 0
    t = bsz * seq
    aw = N_HEADS * HEAD_DIM
    gwid = ln_v_g.shape[0]
    n_exp = w_router.shape[1]
    xf = x.reshape(t, d)

    mod = _ada(c, w_ada, b_ada).reshape(bsz, 6, d)

    cuts = [0, aw, 2 * aw, 3 * aw, 3 * aw + gwid, 3 * aw + 2 * gwid, 3 * aw + 2 * gwid + d,
            3 * aw + 2 * gwid + 2 * d]
    wq, wk, wv, wu, wvg, wga, wgb = [w_in[:, a:b].astype(BF16) for a, b in zip(cuts[:-1], cuts[1:])]
    cos, sin = _rope_tables(seq)
    bsp = jnp.repeat(b_spatial.T, gwid // GMLP_GROUPS, axis=1)
    q, k, v, ga, gbyb = _inproj(xf, mod, cos, sin, wq, wk, wv, wu, wvg, wga, wgb,
                                ln_v_g.reshape(1, gwid), ln_v_b.reshape(1, gwid), w_spatial, bsp,
                                w_proj_b.astype(BF16), seq)

    oa = _attention(q.reshape(bsz, seq, aw), k.reshape(bsz, seq, aw), v.reshape(bsz, seq, aw))

    h2t, base, e_idx, wts, rank, cnt = _post(
        alpha, oa.reshape(t, aw), ga, gbyb, xf, mod, w_proj_a.astype(BF16), w_out.astype(BF16),
        ln1_g.reshape(1, d), ln1_b.reshape(1, d), w_router.T.astype(BF16), router_bias.reshape(n_exp, 1),
        w_gate_sh.astype(BF16), w_up_sh.astype(BF16), w_down_sh.astype(BF16), seq)

    counts = cnt[:, 0].astype(jnp.int32)
    nblk = ((counts + BM - 1) // BM).astype(jnp.int32)
    blk_end = jnp.cumsum(nblk).astype(jnp.int32)
    blk_start = blk_end - nblk
    n_blocks = (t * TOP_K) // BM + n_exp
    n_used = blk_end[-1:]
    row_start = blk_start * BM
    ids = jnp.arange(n_exp, dtype=jnp.int32)
    seen = jnp.where((ids[None, :] <= ids[:, None]) & (nblk[None, :] > 0), ids[None, :], -1).max(axis=1)
    w_idx = jnp.where(seen >= 0, seen, jnp.where(nblk > 0, ids, n_exp - 1).min()).astype(jnp.int32)

    dest = _dest(row_start, e_idx, rank)
    xs = _dispatch(row_start + counts, nblk * BM - counts, n_used, dest, h2t, n_blocks)
    ys = _experts(nblk, blk_start, w_idx, n_used, xs, w_gate_e, w_up_e, w_down_e)
    out = _combine(dest, wts, base, mod, ln2_g.reshape(1, d), ln2_b.reshape(1, d), ys, seq)
    return out.reshape(bsz, seq, d)


def kernel(x, c, w_ada, b_ada, w_in, ln_v_g, ln_v_b, w_spatial, b_spatial, w_proj_a, w_proj_b, w_out,
           ln1_g, ln1_b, w_router, router_bias, w_gate_e, w_up_e, w_down_e, w_gate_sh, w_up_sh,
           w_down_sh, ln2_g, ln2_b):
    depth = w_ada.shape[0]
    alpha = (2.0 * depth) ** 0.25
    for l in range(depth):
        x = _layer(x, c, w_ada[l], b_ada[l], w_in[l], ln_v_g[l], ln_v_b[l], w_spatial[l], b_spatial[l],
                   w_proj_a[l], w_proj_b[l], w_out[l], ln1_g[l], ln1_b[l], w_router[l], router_bias[l],
                   w_gate_e[l], w_up_e[l], w_down_e[l], w_gate_sh[l], w_up_sh[l], w_down_sh[l],
                   ln2_g[l], ln2_b[l], alpha)
    return x
```

```python
import functools
import math

import jax
import jax.numpy as jnp
from jax import lax
from jax.experimental import pallas as pl
from jax.experimental.pallas import tpu as pltpu

F32 = jnp.float32
BF16 = jnp.bfloat16

N_HEADS = 8
HEAD_DIM = 64
DILATIONS = (1, 4, 16)
BAND = 128
ROPE_THETA = 10000.0
GMLP_GROUPS = 8
GMLP_CHUNK = 128
N_EXPERT_GROUPS = 8
TOPK_GROUPS = 4
TOP_K = 8
ROUTED_SCALE = 2.5
LN_EPS = 1e-5
LANES = 128
SUBLANES = 8
ROW_TILE = 8
VMEM_LIMIT = 56 * 1024 * 1024

TM_IN = 512
TM_POST = 256
TM_DISPATCH = 512
TM_COMBINE = 128
BM = 256
ATTN_BLOCKS_PER_TRIP = 4


def _cparams(sem):
    return pltpu.CompilerParams(dimension_semantics=sem, vmem_limit_bytes=VMEM_LIMIT)


def _const_spec(shape):
    nd = len(shape)
    return pl.BlockSpec(shape, lambda *_: (0,) * nd)


def _layer_norm(r, g, b):
    mu = jnp.mean(r, axis=-1, keepdims=True)
    d = r - mu
    var = jnp.mean(d * d, axis=-1, keepdims=True)
    return d * lax.rsqrt(var + LN_EPS) * g + b


def _gelu(t):
    return 0.5 * t * (1.0 + lax.erf(t * (1.0 / math.sqrt(2.0))))


def _sigmoid(t):
    return 1.0 / (1.0 + jnp.exp(-t))


def _store_token_rows(ref, first_row, v):
    m = v.shape[0]
    for s in range(ROW_TILE):
        ref[pl.ds(first_row + s, m, stride=ROW_TILE), :] = v[:, s * LANES:(s + 1) * LANES]


def _load_token_rows(ref, first_row, m):
    return jnp.concatenate([ref[pl.ds(first_row + s, m, stride=ROW_TILE), :] for s in range(ROW_TILE)], axis=1)


def _ada_kernel(c_ref, w_ref, b_ref, o_ref):
    c = c_ref[...]
    o_ref[...] = jnp.dot(c * _sigmoid(c), w_ref[...], preferred_element_type=F32) + b_ref[...]


def _ada(c, w, b):
    bsz, d = c.shape
    n = w.shape[1]
    tn = 1024
    return pl.pallas_call(
        _ada_kernel,
        grid=(n // tn,),
        in_specs=[_const_spec((bsz, d)), pl.BlockSpec((d, tn), lambda j: (0, j)),
                  pl.BlockSpec((1, tn), lambda j: (0, j))],
        out_specs=pl.BlockSpec((bsz, tn), lambda j: (0, j)),
        out_shape=jax.ShapeDtypeStruct((bsz, n), F32),
        compiler_params=_cparams(("arbitrary",)),
        name="ada",
    )(c, w, b.reshape(1, n))


def _inproj_kernel(x_ref, mod_ref, cos_ref, sin_ref, wq_ref, wk_ref, wv_ref, wu_ref, wvg_ref,
                   wga_ref, wgb_ref, lng_ref, lnb_ref, wsp_ref, bsp_ref, wpb_ref,
                   q_ref, k_ref, v_ref, ga_ref, gbyb_ref, u_scr, vv_scr, gated_scr):
    tm = x_ref.shape[0]
    shift = mod_ref[0, 0:1, :]
    scale = mod_ref[0, 1:2, :]
    h = (x_ref[...] * (1.0 + scale) + shift).astype(BF16)

    cos = cos_ref[...]
    sin = sin_ref[...]
    aw = cos.shape[1]
    lane = lax.broadcasted_iota(jnp.int32, (tm, aw), 1)
    first_half = (lane % HEAD_DIM) < (HEAD_DIM // 2)

    def rope(t):
        partner = jnp.where(first_half, pltpu.roll(t, aw - HEAD_DIM // 2, 1),
                            pltpu.roll(t, HEAD_DIM // 2, 1))
        return t * cos + partner * sin

    q = jnp.dot(h, wq_ref[...], preferred_element_type=F32)
    q_ref[...] = (rope(q) * (HEAD_DIM ** -0.5)).astype(BF16)
    k = jnp.dot(h, wk_ref[...], preferred_element_type=F32)
    k_ref[...] = rope(k).astype(BF16)
    v_ref[...] = jnp.dot(h, wv_ref[...], preferred_element_type=F32).astype(BF16)

    u_scr[...] = _gelu(jnp.dot(h, wu_ref[...], preferred_element_type=F32))
    vg = _gelu(jnp.dot(h, wvg_ref[...], preferred_element_type=F32))
    vv_scr[...] = _layer_norm(vg, lng_ref[...], lnb_ref[...]).astype(BF16)

    ti = lax.broadcasted_iota(jnp.int32, (GMLP_CHUNK, GMLP_CHUNK), 0)
    si = lax.broadcasted_iota(jnp.int32, (GMLP_CHUNK, GMLP_CHUNK), 1)
    causal = si <= ti
    gw = u_scr.shape[1] // GMLP_GROUPS
    for g in range(GMLP_GROUPS):
        ws = jnp.where(causal, wsp_ref[g], 0.0).astype(BF16)
        cols = slice(g * gw, (g + 1) * gw)
        for c in range(tm // GMLP_CHUNK):
            rows = slice(c * GMLP_CHUNK, (c + 1) * GMLP_CHUNK)
            sv = jnp.dot(ws, vv_scr[rows, cols], preferred_element_type=F32) + bsp_ref[:, cols]
            gated_scr[rows, cols] = (u_scr[rows, cols] * sv).astype(BF16)

    y_b = jnp.dot(gated_scr[...], wpb_ref[...], preferred_element_type=F32)
    g_b = _sigmoid(jnp.dot(h, wgb_ref[...], preferred_element_type=F32))
    gbyb_ref[...] = (g_b * y_b).astype(BF16)
    ga_ref[...] = _sigmoid(jnp.dot(h, wga_ref[...], preferred_element_type=F32)).astype(BF16)


def _inproj(xf, mod, cos, sin, wq, wk, wv, wu, wvg, wga, wgb, lng, lnb, wsp, bsp, wpb, seq):
    t, d = xf.shape
    aw = wq.shape[1]
    gwid = wu.shape[1]
    tm = TM_IN
    per_seq = seq // tm
    row = lambda i: (i, 0)
    tab = lambda i: (i % per_seq, 0)
    return pl.pallas_call(
        _inproj_kernel,
        grid=(t // tm,),
        in_specs=[pl.BlockSpec((tm, d), row),
                  pl.BlockSpec((1,) + mod.shape[1:], lambda i: (i // per_seq, 0, 0)),
                  pl.BlockSpec((tm, aw), tab), pl.BlockSpec((tm, aw), tab),
                  _const_spec(wq.shape), _const_spec(wk.shape), _const_spec(wv.shape),
                  _const_spec(wu.shape), _const_spec(wvg.shape), _const_spec(wga.shape),
                  _const_spec(wgb.shape), _const_spec(lng.shape), _const_spec(lnb.shape),
                  _const_spec(wsp.shape), _const_spec(bsp.shape), _const_spec(wpb.shape)],
        out_specs=[pl.BlockSpec((tm, aw), row)] * 3 + [pl.BlockSpec((tm, d), row)] * 2,
        out_shape=[jax.ShapeDtypeStruct((t, aw), BF16)] * 3 + [jax.ShapeDtypeStruct((t, d), BF16)] * 2,
        scratch_shapes=[pltpu.VMEM((tm, gwid), F32), pltpu.VMEM((tm, gwid), BF16),
                        pltpu.VMEM((tm, gwid), BF16)],
        compiler_params=_cparams(("arbitrary",)),
        name="inproj",
    )(xf, mod, cos, sin, wq, wk, wv, wu, wvg, wga, wgb, lng, lnb, wsp, bsp, wpb)


def _rows(start, size, stride):
    if stride == 1:
        return pl.ds(start, size)
    return pl.ds(start, size, stride=stride)


def _band_aligned(start):
    return start if isinstance(start, int) else pl.multiple_of(start, BAND)


def _interleaved_loop(count, load, compute, store):
    per_trip = max(u for u in range(1, ATTN_BLOCKS_PER_TRIP + 1) if count % u == 0) if count else 1

    def trip(it, c):
        items = [it * per_trip + u for u in range(per_trip)]
        results = [compute(ops) for ops in [load(i) for i in items]]
        for i, res in zip(items, results):
            store(i, res)
        return c

    if count == per_trip:
        trip(0, 0)
    elif count:
        lax.fori_loop(0, count // per_trip, trip, 0)


def _attn_kernel(*refs):
    npat = len(DILATIONS)
    qkv = [refs[3 * p:3 * p + 3] for p in range(npat)]
    o_ref, out_scr, lse_scr = refs[3 * npat:]
    seq = o_ref.shape[1]

    lane = lax.broadcasted_iota(jnp.int32, (BAND, LANES), 1)
    head0 = lane < HEAD_DIM
    neg = jnp.float32(-1e30)

    qi = lax.broadcasted_iota(jnp.int32, (BAND, 2 * BAND), 0)
    kj = lax.broadcasted_iota(jnp.int32, (BAND, 2 * BAND), 1)

    def load(p, blk, nb):
        q_ref, k_ref, v_ref = qkv[p]
        q_start = blk * BAND
        k_start = jnp.maximum(q_start - BAND, 0)
        off = q_start - k_start
        lowest = jnp.where(blk % nb == 0, off, qi)
        mask = (kj >= lowest) & (kj <= qi + off)
        krows = pl.ds(_band_aligned(k_start), 2 * BAND)
        return (q_ref[0, 0, pl.ds(_band_aligned(q_start), BAND), :], k_ref[0, 0, krows, :],
                v_ref[0, 0, krows, :], mask)

    def attend(ops):
        qb, kb, vb, mask = ops
        zero = jnp.zeros(qb.shape, qb.dtype)
        parts = []
        for hd in range(2):
            qh = jnp.where(head0 if hd == 0 else ~head0, qb, zero)
            s = lax.dot_general(qh, kb, (((1,), (1,)), ((), ())), preferred_element_type=F32)
            s = jnp.where(mask, s, neg)
            m = jnp.max(s, axis=-1, keepdims=True)
            pr = jnp.exp(s - m)
            l = jnp.sum(pr, axis=-1, keepdims=True)
            acc = jnp.dot(pr.astype(BF16), vb, preferred_element_type=F32)
            parts.append((acc / l, m + jnp.log(l)))
        return [jnp.where(head0, a0, a1) for a0, a1 in zip(*parts)]

    def store(p, seq_start, stride, res):
        rows = _rows(seq_start if stride > 1 else _band_aligned(seq_start), BAND, stride)
        out_scr[p, rows, :] = res[0]
        lse_scr[p, rows, :] = res[1]

    for p, dil in enumerate(DILATIONS):
        n = seq // dil
        nb = n // BAND

        _interleaved_loop(
            seq // BAND, lambda blk, p=p, nb=nb: load(p, blk, nb), attend,
            lambda blk, res, p=p, dil=dil, nb=nb: store(p, blk // nb + dil * BAND * (blk % nb), dil, res))

    top = jnp.maximum(jnp.maximum(lse_scr[0], lse_scr[1]), lse_scr[2])
    num = jnp.zeros(top.shape, F32)
    den = jnp.zeros(top.shape, F32)
    for p in range(npat):
        w = jnp.exp(lse_scr[p] - top)
        num = num + w * out_scr[p]
        den = den + w
    o_ref[0] = (num / den).astype(BF16)


def _class_major(t, dil):
    bsz, seq, aw = t.shape
    t = t.reshape(bsz, seq // dil, dil, aw // LANES, LANES)
    return t.transpose(0, 3, 2, 1, 4).reshape(bsz, aw // LANES, seq, LANES)


def _attention(q, k, v):
    bsz, seq, aw = q.shape
    npat = len(DILATIONS)
    blk = (1, 1, seq, LANES)
    operands, specs = [], []
    for dil in DILATIONS:
        for t in (q, k, v):
            if dil == 1:
                operands.append(t.reshape(bsz, 1, seq, aw))
                specs.append(pl.BlockSpec(blk, lambda b, hp: (b, 0, 0, hp)))
            else:
                operands.append(_class_major(t, dil))
                specs.append(pl.BlockSpec(blk, lambda b, hp: (b, hp, 0, 0)))
    return pl.pallas_call(
        _attn_kernel,
        grid=(bsz, aw // LANES),
        in_specs=specs,
        out_specs=pl.BlockSpec((1, seq, LANES), lambda b, hp: (b, 0, hp)),
        out_shape=jax.ShapeDtypeStruct((bsz, seq, aw), BF16),
        scratch_shapes=[pltpu.VMEM((npat, seq, LANES), F32)] * 2,
        compiler_params=_cparams(("arbitrary", "arbitrary")),
        name="attn",
    )(*operands)


def _post_kernel(alpha, oa_ref, ga_ref, gbyb_ref, x_ref, mod_ref, wpa_ref, wo_ref, l1g_ref, l1b_ref,
                 wrt_ref, rb_ref, wgs_ref, wus_ref, wds_ref, tri_ref,
                 h2_ref, base_ref, e_ref, w_ref, r_ref, cnt_ref):
    tm = x_ref.shape[0]
    n_exp = wrt_ref.shape[0]
    per_group = n_exp // N_EXPERT_GROUPS
    gate_a = mod_ref[0, 2:3, :]
    shift_f = mod_ref[0, 3:4, :]
    scale_f = mod_ref[0, 4:5, :]
    gate_f = mod_ref[0, 5:6, :]

    y_a = jnp.dot(oa_ref[...], wpa_ref[...], preferred_element_type=F32)
    merged = ga_ref[...].astype(F32) * y_a + gbyb_ref[...].astype(F32)
    mix = jnp.dot(merged.astype(BF16), wo_ref[...], preferred_element_type=F32)
    x1 = _layer_norm(alpha * x_ref[...] + gate_a * mix, l1g_ref[...], l1b_ref[...])
    h2 = x1 * (1.0 + scale_f) + shift_f
    h2b = h2.astype(BF16)

    _store_token_rows(h2_ref, 0, h2)

    a = jnp.dot(h2b, wgs_ref[...], preferred_element_type=F32)
    b = jnp.dot(h2b, wus_ref[...], preferred_element_type=F32)
    shared = jnp.dot((a * _sigmoid(a) * b).astype(BF16), wds_ref[...], preferred_element_type=F32)
    base_ref[...] = alpha * x1 + gate_f * shared

    logits = lax.dot_general(wrt_ref[...], h2b, (((1,), (1,)), ((), ())), preferred_element_type=F32)
    scores = _sigmoid(logits)
    biased = scores + rb_ref[...]
    ninf = jnp.float32(-jnp.inf)

    sub = lax.broadcasted_iota(jnp.int32, (per_group, tm), 0)
    grp_rows = []
    for g in range(N_EXPERT_GROUPS):
        blk = biased[g * per_group:(g + 1) * per_group, :]
        m1 = jnp.max(blk, axis=0, keepdims=True)
        f1 = jnp.min(jnp.where(blk == m1, sub, per_group), axis=0, keepdims=True)
        m2 = jnp.max(jnp.where(sub == f1, ninf, blk), axis=0, keepdims=True)
        grp_rows.append(m1 + m2)
    grp = jnp.concatenate(grp_rows, axis=0)

    gi = lax.broadcasted_iota(jnp.int32, grp.shape, 0)
    chosen = jnp.zeros(grp.shape, jnp.bool_)
    cur = grp
    for _ in range(TOPK_GROUPS):
        m = jnp.max(cur, axis=0, keepdims=True)
        f = jnp.min(jnp.where(cur == m, gi, N_EXPERT_GROUPS), axis=0, keepdims=True)
        hit = gi == f
        chosen = chosen | hit
        cur = jnp.where(hit, ninf, cur)
    emask = jnp.concatenate(
        [jnp.broadcast_to(chosen[g:g + 1, :], (per_group, tm)) for g in range(N_EXPERT_GROUPS)], axis=0)

    ei = lax.broadcasted_iota(jnp.int32, (n_exp, tm), 0)
    cand = jnp.where(emask, biased, ninf)
    hits, ids, wts = [], [], []
    for _ in range(TOP_K):
        m = jnp.max(cand, axis=0, keepdims=True)
        f = jnp.min(jnp.where(cand == m, ei, n_exp), axis=0, keepdims=True)
        hit = ei == f
        wts.append(jnp.sum(jnp.where(hit, scores, 0.0), axis=0, keepdims=True))
        cand = jnp.where(hit, ninf, cand)
        hits.append(hit)
        ids.append(f)
    onehot = hits[0]
    for hit in hits[1:]:
        onehot = onehot | hit
    onehot = jnp.where(onehot, 1.0, 0.0)

    @pl.when(pl.program_id(0) == 0)
    def _():
        cnt_ref[...] = jnp.zeros(cnt_ref.shape, F32)

    before = jnp.dot(onehot.astype(BF16), tri_ref[...], preferred_element_type=F32) + cnt_ref[...]
    ranks = [jnp.sum(jnp.where(hit, before, 0.0), axis=0, keepdims=True) for hit in hits]
    cnt_ref[...] += jnp.sum(onehot, axis=1, keepdims=True)

    denom = wts[0]
    for wk in wts[1:]:
        denom = denom + wk
    e_ref[...] = jnp.concatenate(ids, axis=0)
    w_ref[...] = jnp.concatenate([ROUTED_SCALE * wk / denom for wk in wts], axis=0)
    r_ref[...] = jnp.concatenate(ranks, axis=0).astype(jnp.int32)


def _post(alpha, oa, ga, gbyb, xf, mod, wpa, wo, l1g, l1b, wrt, rb, wgs, wus, wds, seq):
    t, d = xf.shape
    aw = oa.shape[1]
    n_exp = wrt.shape[0]
    tm = TM_POST
    per_seq = seq // tm
    row = lambda i: (i, 0)
    col = lambda i: (0, i)
    tri = (jnp.arange(tm)[:, None] < jnp.arange(tm)[None, :]).astype(BF16)
    return pl.pallas_call(
        functools.partial(_post_kernel, alpha),
        grid=(t // tm,),
        in_specs=[pl.BlockSpec((tm, aw), row), pl.BlockSpec((tm, d), row), pl.BlockSpec((tm, d), row),
                  pl.BlockSpec((tm, d), row),
                  pl.BlockSpec((1,) + mod.shape[1:], lambda i: (i // per_seq, 0, 0)),
                  _const_spec(wpa.shape), _const_spec(wo.shape), _const_spec(l1g.shape),
                  _const_spec(l1b.shape), _const_spec(wrt.shape), _const_spec(rb.shape),
                  _const_spec(wgs.shape), _const_spec(wus.shape), _const_spec(wds.shape),
                  _const_spec(tri.shape)],
        out_specs=[pl.BlockSpec((tm * ROW_TILE, LANES), row), pl.BlockSpec((tm, d), row),
                   pl.BlockSpec((TOP_K, tm), col), pl.BlockSpec((TOP_K, tm), col),
                   pl.BlockSpec((TOP_K, tm), col), _const_spec((n_exp, 1))],
        out_shape=[jax.ShapeDtypeStruct((t * ROW_TILE, LANES), F32), jax.ShapeDtypeStruct((t, d), F32),
                   jax.ShapeDtypeStruct((TOP_K, t), jnp.int32), jax.ShapeDtypeStruct((TOP_K, t), F32),
                   jax.ShapeDtypeStruct((TOP_K, t), jnp.int32), jax.ShapeDtypeStruct((n_exp, 1), F32)],
        compiler_params=_cparams(("arbitrary",)),
        name="post",
    )(oa, ga, gbyb, xf, mod, wpa, wo, l1g, l1b, wrt, rb, wgs, wus, wds, tri)


def _row_copy(src_ref, src_row, dst_ref, dst_row, sem):
    return pltpu.make_async_copy(
        src_ref.at[pl.ds(pl.multiple_of(src_row * ROW_TILE, ROW_TILE), ROW_TILE), :],
        dst_ref.at[pl.ds(pl.multiple_of(dst_row * ROW_TILE, ROW_TILE), ROW_TILE), :], sem)


def _dest_kernel(rs_ref, e_ref, r_ref, o_ref):
    e = e_ref[...]
    start = lax.fori_loop(0, rs_ref.shape[0], lambda x, acc: jnp.where(e == x, rs_ref[x], acc),
                          jnp.zeros(e.shape, jnp.int32))
    o_ref[...] = start + r_ref[...]


def _dest(row_start, e_idx, rank):
    t = e_idx.shape[1]
    tm = min(t, 2048)
    blk = pl.BlockSpec((TOP_K, tm), lambda i, *_: (0, i))
    return pl.pallas_call(
        _dest_kernel,
        grid_spec=pltpu.PrefetchScalarGridSpec(num_scalar_prefetch=1, grid=(t // tm,), in_specs=[blk, blk],
                                               out_specs=blk),
        out_shape=jax.ShapeDtypeStruct((TOP_K, t), jnp.int32),
        compiler_params=_cparams(("arbitrary",)),
        name="dest",
    )(row_start, e_idx, rank)


def _dispatch_kernel(n_blocks, ps_ref, pl_ref, nu_ref, dest_ref, h2_ref, xs_ref, zbuf, sem, zsem):
    i = pl.program_id(0)
    n_steps = pl.num_programs(0)
    tm = dest_ref.shape[1]
    n_exp = ps_ref.shape[0]
    per_step = -(-n_exp // n_steps)
    zbuf[...] = jnp.zeros(zbuf.shape, F32)

    def zero_fill(act):
        for q in range(per_step):
            e = jnp.minimum(i * per_step + q, n_exp - 1)
            live = i * per_step + q < n_exp
            gap = pl_ref[e]
            for bit in [1 << s for s in range(BM.bit_length() - 1)]:
                @pl.when(live & ((gap & bit) != 0))
                def _(bit=bit, e=e, gap=gap):
                    off = ps_ref[e] + (gap & ~(2 * bit - 1))
                    act(pltpu.make_async_copy(
                        zbuf.at[pl.ds(0, bit * ROW_TILE), :],
                        xs_ref.at[pl.ds(pl.multiple_of(off * ROW_TILE, ROW_TILE), bit * ROW_TILE), :], zsem))
            blk = nu_ref[0] + i * per_step + q
            @pl.when(blk < n_blocks)
            def _(blk=blk):
                rows = BM * ROW_TILE
                act(pltpu.make_async_copy(
                    zbuf, xs_ref.at[pl.ds(pl.multiple_of(blk * rows, rows), rows), :], zsem))

    zero_fill(lambda cp: cp.start(priority=1))

    def issue(j, c):
        for k in range(TOP_K):
            _row_copy(h2_ref, j, xs_ref, dest_ref[k, j], sem).start(priority=k % 2)
        return c

    lax.fori_loop(0, tm, issue, 0)
    n = tm * TOP_K * ROW_TILE
    pltpu.make_async_copy(xs_ref.at[pl.ds(0, n), :], xs_ref.at[pl.ds(0, n), :], sem).wait()
    zero_fill(lambda cp: cp.wait())


def _dispatch(pad_start, pad_len, n_used, dest, h2t, n_blocks):
    t = dest.shape[1]
    tm = TM_DISPATCH
    grid_spec = pltpu.PrefetchScalarGridSpec(
        num_scalar_prefetch=3,
        grid=(t // tm,),
        in_specs=[pl.BlockSpec((TOP_K, tm), lambda i, *_: (0, i), memory_space=pltpu.SMEM),
                  pl.BlockSpec((tm * ROW_TILE, LANES), lambda i, *_: (i, 0))],
        out_specs=pl.BlockSpec(memory_space=pl.ANY),
        scratch_shapes=[pltpu.VMEM((BM * ROW_TILE, LANES), F32), pltpu.SemaphoreType.DMA,
                        pltpu.SemaphoreType.DMA],
    )
    return pl.pallas_call(
        functools.partial(_dispatch_kernel, n_blocks),
        grid_spec=grid_spec,
        out_shape=jax.ShapeDtypeStruct((n_blocks * BM * ROW_TILE, LANES), F32),
        compiler_params=_cparams(("arbitrary",)),
        name="dispatch",
    )(pad_start, pad_len, n_used, dest, h2t)


def _expert_kernel(nblk_ref, bstart_ref, widx_ref, nu_ref, xs_ref, wg_ref, wu_ref, wd_ref, ys_ref,
                   xbuf, ybuf, wg_b, wu_b, wd_b, xsem, ysem):
    e = pl.program_id(0)
    n_used = nu_ref[0]
    rows = BM * ROW_TILE

    def slot_base(g):
        return pl.multiple_of((g % 2) * rows, rows)

    def hbm_rows(ref, g):
        return ref.at[pl.ds(pl.multiple_of(g * rows, rows), rows), :]

    def x_copy(g):
        return pltpu.make_async_copy(hbm_rows(xs_ref, g), xbuf.at[pl.ds(slot_base(g), rows), :], xsem.at[g % 2])

    def y_copy(g):
        return pltpu.make_async_copy(ybuf.at[pl.ds(slot_base(g), rows), :], hbm_rows(ys_ref, g), ysem.at[g % 2])

    @pl.when((e == 0) & (n_used > 0))
    def _():
        x_copy(0).start()

    wg_b[...] = wg_ref[0].astype(BF16)
    wu_b[...] = wu_ref[0].astype(BF16)
    wd_b[...] = wd_ref[0].astype(BF16)

    def block(b, c):
        g = bstart_ref[e] + b
        x_copy(g).wait()

        @pl.when(g + 1 < n_used)
        def _():
            x_copy(g + 1).start()

        @pl.when(g >= 2)
        def _():
            y_copy(g - 2).wait()

        x = _load_token_rows(xbuf, slot_base(g), BM).astype(BF16)
        a = jnp.dot(x, wg_b[...], preferred_element_type=F32)
        u = jnp.dot(x, wu_b[...], preferred_element_type=F32)
        act = (a * _sigmoid(a) * u).astype(BF16)
        _store_token_rows(ybuf, slot_base(g), jnp.dot(act, wd_b[...], preferred_element_type=F32))
        y_copy(g).start(priority=1)
        return c

    lax.fori_loop(0, nblk_ref[e], block, 0)

    @pl.when(e == pl.num_programs(0) - 1)
    def _():
        for back in (2, 1):
            @pl.when(n_used >= back)
            def _(back=back):
                y_copy(n_used - back).wait()


def _experts(nblk, blk_start, w_idx, n_used, xs, wg, wu, wd):
    n_exp, d, de = wg.shape
    rows = BM * ROW_TILE
    wsel = lambda e, nb, bs, wi, nu: (wi[e], 0, 0)
    grid_spec = pltpu.PrefetchScalarGridSpec(
        num_scalar_prefetch=4,
        grid=(n_exp,),
        in_specs=[pl.BlockSpec(memory_space=pl.ANY),
                  pl.BlockSpec((1, d, de), wsel), pl.BlockSpec((1, d, de), wsel),
                  pl.BlockSpec((1, de, d), wsel)],
        out_specs=pl.BlockSpec(memory_space=pl.ANY),
        scratch_shapes=[pltpu.VMEM((2 * rows, LANES), F32), pltpu.VMEM((2 * rows, LANES), F32),
                        pltpu.VMEM((d, de), BF16), pltpu.VMEM((d, de), BF16), pltpu.VMEM((de, d), BF16),
                        pltpu.SemaphoreType.DMA((2,)), pltpu.SemaphoreType.DMA((2,))],
    )
    return pl.pallas_call(
        _expert_kernel,
        grid_spec=grid_spec,
        out_shape=jax.ShapeDtypeStruct(xs.shape, xs.dtype),
        input_output_aliases={4: 0},
        compiler_params=_cparams(("arbitrary",)),
        name="experts",
    )(nblk, blk_start, w_idx, n_used, xs, wg, wu, wd)


def _combine_kernel(dest_ref, next_dest_ref, w_ref, base_ref, mod_ref, g_ref, b_ref, ys_ref, o_ref, buf, sems):
    i = pl.program_id(0)
    tm = dest_ref.shape[1]
    slab = tm * ROW_TILE
    slot_rows = TOP_K * slab

    def gather(idx_ref, slot):
        def issue(j, c):
            for k in range(TOP_K):
                _row_copy(ys_ref, idx_ref[k, j], buf, slot * (TOP_K * tm) + k * tm + j,
                          sems.at[slot]).start(priority=k % 2)
            return c
        lax.fori_loop(0, tm, issue, 0)

    @pl.when(i == 0)
    def _():
        gather(dest_ref, 0)

    @pl.when(i + 1 < pl.num_programs(0))
    def _():
        gather(next_dest_ref, (i + 1) % 2)

    slot = i % 2
    base_row = pl.multiple_of(slot * slot_rows, slot_rows)
    pltpu.make_async_copy(ys_ref.at[pl.ds(0, slot_rows), :], buf.at[pl.ds(base_row, slot_rows), :],
                          sems.at[slot]).wait()

    wpad = jnp.concatenate([w_ref[...], jnp.zeros((LANES - TOP_K, tm), F32)], axis=0)
    wcol = wpad.T
    routed = jnp.zeros(base_ref.shape, F32)
    for k in range(TOP_K):
        routed = routed + wcol[:, k:k + 1] * _load_token_rows(buf, base_row + k * slab, tm)
    gate_f = mod_ref[0, 5:6, :]
    o_ref[...] = _layer_norm(base_ref[...] + gate_f * routed, g_ref[...], b_ref[...])


def _combine(dest, wts, base, mod, g, b, ys, seq):
    t, d = base.shape
    tm = TM_COMBINE
    per_seq = seq // tm
    n_steps = t // tm
    col = lambda i: (0, i)
    row = lambda i: (i, 0)
    return pl.pallas_call(
        _combine_kernel,
        grid=(n_steps,),
        in_specs=[pl.BlockSpec((TOP_K, tm), col, memory_space=pltpu.SMEM),
                  pl.BlockSpec((TOP_K, tm), lambda i: (0, jnp.minimum(i + 1, n_steps - 1)),
                               memory_space=pltpu.SMEM),
                  pl.BlockSpec((TOP_K, tm), col), pl.BlockSpec((tm, d), row),
                  pl.BlockSpec((1,) + mod.shape[1:], lambda i: (i // per_seq, 0, 0)),
                  _const_spec(g.shape), _const_spec(b.shape),
                  pl.BlockSpec(memory_space=pl.ANY)],
        out_specs=pl.BlockSpec((tm, d), row),
        out_shape=jax.ShapeDtypeStruct((t, d), F32),
        scratch_shapes=[pltpu.VMEM((2 * TOP_K * tm * ROW_TILE, LANES), F32), pltpu.SemaphoreType.DMA((2,))],
        compiler_params=_cparams(("arbitrary",)),
        name="combine",
    )(dest, dest, wts, base, mod, g, b, ys)


def _rope_tables(seq):
    half = HEAD_DIM // 2
    inv_freq = ROPE_THETA ** (-jnp.arange(half, dtype=F32) / half)
    ang = jnp.arange(seq, dtype=F32)[:, None] * inv_freq[None, :]
    cos = jnp.concatenate([jnp.cos(ang), jnp.cos(ang)], axis=1)
    sin = jnp.concatenate([-jnp.sin(ang), jnp.sin(ang)], axis=1)
    return jnp.tile(cos, (1, N_HEADS)), jnp.tile(sin, (1, N_HEADS))


def _layer(x, c, w_ada, b_ada, w_in, ln_v_g, ln_v_b, w_spatial, b_spatial, w_proj_a, w_proj_b, w_out,
           ln1_g, ln1_b, w_router, router_bias, w_gate_e, w_up_e, w_down_e, w_gate_sh, w_up_sh,
           w_down_sh, ln2_g, ln2_b, alpha):
    bsz, seq, d = x.shape
    assert seq % (max(DILATIONS) * BAND) == 0 and seq % TM_IN == 0 and d == ROW_TILE * LANES
    t = bsz * seq
    aw = N_HEADS * HEAD_DIM
    gwid = ln_v_g.shape[0]
    n_exp = w_router.shape[1]
    xf = x.reshape(t, d)

    mod = _ada(c, w_ada, b_ada).reshape(bsz, 6, d)

    cuts = [0, aw, 2 * aw, 3 * aw, 3 * aw + gwid, 3 * aw + 2 * gwid, 3 * aw + 2 * gwid + d,
            3 * aw + 2 * gwid + 2 * d]
    wq, wk, wv, wu, wvg, wga, wgb = [w_in[:, a:b].astype(BF16) for a, b in zip(cuts[:-1], cuts[1:])]
    cos, sin = _rope_tables(seq)
    bsp = jnp.repeat(b_spatial.T, gwid // GMLP_GROUPS, axis=1)
    q, k, v, ga, gbyb = _inproj(xf, mod, cos, sin, wq, wk, wv, wu, wvg, wga, wgb,
                                ln_v_g.reshape(1, gwid), ln_v_b.reshape(1, gwid), w_spatial, bsp,
                                w_proj_b.astype(BF16), seq)

    oa = _attention(q.reshape(bsz, seq, aw), k.reshape(bsz, seq, aw), v.reshape(bsz, seq, aw))

    h2t, base, e_idx, wts, rank, cnt = _post(
        alpha, oa.reshape(t, aw), ga, gbyb, xf, mod, w_proj_a.astype(BF16), w_out.astype(BF16),
        ln1_g.reshape(1, d), ln1_b.reshape(1, d), w_router.T.astype(BF16), router_bias.reshape(n_exp, 1),
        w_gate_sh.astype(BF16), w_up_sh.astype(BF16), w_down_sh.astype(BF16), seq)

    counts = cnt[:, 0].astype(jnp.int32)
    nblk = ((counts + BM - 1) // BM).astype(jnp.int32)
    blk_end = jnp.cumsum(nblk).astype(jnp.int32)
    blk_start = blk_end - nblk
    n_blocks = (t * TOP_K) // BM + n_exp
    n_used = blk_end[-1:]
    row_start = blk_start * BM
    ids = jnp.arange(n_exp, dtype=jnp.int32)
    seen = jnp.where((ids[None, :] <= ids[:, None]) & (nblk[None, :] > 0), ids[None, :], -1).max(axis=1)
    w_idx = jnp.where(seen >= 0, seen, jnp.where(nblk > 0, ids, n_exp - 1).min()).astype(jnp.int32)

    dest = _dest(row_start, e_idx, rank)
    xs = _dispatch(row_start + counts, nblk * BM - counts, n_used, dest, h2t, n_blocks)
    ys = _experts(nblk, blk_start, w_idx, n_used, xs, w_gate_e, w_up_e, w_down_e)
    out = _combine(dest, wts, base, mod, ln2_g.reshape(1, d), ln2_b.reshape(1, d), ys, seq)
    return out.reshape(bsz, seq, d)


def kernel(x, c, w_ada, b_ada, w_in, ln_v_g, ln_v_b, w_spatial, b_spatial, w_proj_a, w_proj_b, w_out,
           ln1_g, ln1_b, w_router, router_bias, w_gate_e, w_up_e, w_down_e, w_gate_sh, w_up_sh,
           w_down_sh, ln2_g, ln2_b):
    depth = w_ada.shape[0]
    alpha = (2.0 * depth) ** 0.25
    for l in range(depth):
        x = _layer(x, c, w_ada[l], b_ada[l], w_in[l], ln_v_g[l], ln_v_b[l], w_spatial[l], b_spatial[l],
                   w_proj_a[l], w_proj_b[l], w_out[l], ln1_g[l], ln1_b[l], w_router[l], router_bias[l],
                   w_gate_e[l], w_up_e[l], w_down_e[l], w_gate_sh[l], w_up_sh[l], w_down_sh[l],
                   ln2_g[l], ln2_b[l], alpha)
    return x
```

```python
import functools
import math

import jax
import jax.numpy as jnp
from jax import lax
from jax.experimental import pallas as pl
from jax.experimental.pallas import tpu as pltpu

F32 = jnp.float32
BF16 = jnp.bfloat16

N_HEADS = 8
HEAD_DIM = 64
DILATIONS = (1, 4, 16)
BAND = 128
ROPE_THETA = 10000.0
GMLP_GROUPS = 8
GMLP_CHUNK = 128
N_EXPERT_GROUPS = 8
TOPK_GROUPS = 4
TOP_K = 8
ROUTED_SCALE = 2.5
LN_EPS = 1e-5
LANES = 128
SUBLANES = 8
ROW_TILE = 8
VMEM_LIMIT = 56 * 1024 * 1024

TM_IN = 512
TM_POST = 256
TM_DISPATCH = 512
TM_COMBINE = 128
BM = 256
EXPERT_RING = 4
ATTN_BLOCKS_PER_TRIP = 4


def _cparams(sem):
    return pltpu.CompilerParams(dimension_semantics=sem, vmem_limit_bytes=VMEM_LIMIT)


def _const_spec(shape):
    nd = len(shape)
    return pl.BlockSpec(shape, lambda *_: (0,) * nd)


def _layer_norm(r, g, b):
    mu = jnp.mean(r, axis=-1, keepdims=True)
    d = r - mu
    var = jnp.mean(d * d, axis=-1, keepdims=True)
    return d * lax.rsqrt(var + LN_EPS) * g + b


def _gelu(t):
    return 0.5 * t * (1.0 + lax.erf(t * (1.0 / math.sqrt(2.0))))


def _sigmoid(t):
    return 1.0 / (1.0 + jnp.exp(-t))


def _store_token_rows(ref, first_row, v):
    m = v.shape[0]
    for s in range(ROW_TILE):
        ref[pl.ds(first_row + s, m, stride=ROW_TILE), :] = v[:, s * LANES:(s + 1) * LANES]


def _load_token_rows(ref, first_row, m):
    return jnp.concatenate([ref[pl.ds(first_row + s, m, stride=ROW_TILE), :] for s in range(ROW_TILE)], axis=1)


def _ada_kernel(c_ref, w_ref, b_ref, o_ref):
    c = c_ref[...]
    o_ref[...] = jnp.dot(c * _sigmoid(c), w_ref[...], preferred_element_type=F32) + b_ref[...]


def _ada(c, w, b):
    bsz, d = c.shape
    n = w.shape[1]
    tn = 1024
    return pl.pallas_call(
        _ada_kernel,
        grid=(n // tn,),
        in_specs=[_const_spec((bsz, d)), pl.BlockSpec((d, tn), lambda j: (0, j)),
                  pl.BlockSpec((1, tn), lambda j: (0, j))],
        out_specs=pl.BlockSpec((bsz, tn), lambda j: (0, j)),
        out_shape=jax.ShapeDtypeStruct((bsz, n), F32),
        compiler_params=_cparams(("arbitrary",)),
        name="ada",
    )(c, w, b.reshape(1, n))


def _inproj_kernel(x_ref, mod_ref, cos_ref, sin_ref, wq_ref, wk_ref, wv_ref, wu_ref, wvg_ref,
                   wga_ref, wgb_ref, lng_ref, lnb_ref, wsp_ref, bsp_ref, wpb_ref,
                   q_ref, k_ref, v_ref, ga_ref, gbyb_ref, u_scr, vv_scr, gated_scr):
    tm = x_ref.shape[0]
    shift = mod_ref[0, 0:1, :]
    scale = mod_ref[0, 1:2, :]
    h = (x_ref[...] * (1.0 + scale) + shift).astype(BF16)

    cos = cos_ref[...]
    sin = sin_ref[...]
    aw = cos.shape[1]
    lane = lax.broadcasted_iota(jnp.int32, (tm, aw), 1)
    first_half = (lane % HEAD_DIM) < (HEAD_DIM // 2)

    def rope(t):
        partner = jnp.where(first_half, pltpu.roll(t, aw - HEAD_DIM // 2, 1),
                            pltpu.roll(t, HEAD_DIM // 2, 1))
        return t * cos + partner * sin

    q = jnp.dot(h, wq_ref[...], preferred_element_type=F32)
    q_ref[...] = (rope(q) * (HEAD_DIM ** -0.5)).astype(BF16)
    k = jnp.dot(h, wk_ref[...], preferred_element_type=F32)
    k_ref[...] = rope(k).astype(BF16)
    v_ref[...] = jnp.dot(h, wv_ref[...], preferred_element_type=F32).astype(BF16)

    u_scr[...] = _gelu(jnp.dot(h, wu_ref[...], preferred_element_type=F32))
    vg = _gelu(jnp.dot(h, wvg_ref[...], preferred_element_type=F32))
    vv_scr[...] = _layer_norm(vg, lng_ref[...], lnb_ref[...]).astype(BF16)

    ti = lax.broadcasted_iota(jnp.int32, (GMLP_CHUNK, GMLP_CHUNK), 0)
    si = lax.broadcasted_iota(jnp.int32, (GMLP_CHUNK, GMLP_CHUNK), 1)
    causal = si <= ti
    gw = u_scr.shape[1] // GMLP_GROUPS
    for g in range(GMLP_GROUPS):
        ws = jnp.where(causal, wsp_ref[g], 0.0).astype(BF16)
        cols = slice(g * gw, (g + 1) * gw)
        for c in range(tm // GMLP_CHUNK):
            rows = slice(c * GMLP_CHUNK, (c + 1) * GMLP_CHUNK)
            sv = jnp.dot(ws, vv_scr[rows, cols], preferred_element_type=F32) + bsp_ref[:, cols]
            gated_scr[rows, cols] = (u_scr[rows, cols] * sv).astype(BF16)

    y_b = jnp.dot(gated_scr[...], wpb_ref[...], preferred_element_type=F32)
    g_b = _sigmoid(jnp.dot(h, wgb_ref[...], preferred_element_type=F32))
    gbyb_ref[...] = (g_b * y_b).astype(BF16)
    ga_ref[...] = _sigmoid(jnp.dot(h, wga_ref[...], preferred_element_type=F32)).astype(BF16)


def _inproj(xf, mod, cos, sin, wq, wk, wv, wu, wvg, wga, wgb, lng, lnb, wsp, bsp, wpb, seq):
    t, d = xf.shape
    aw = wq.shape[1]
    gwid = wu.shape[1]
    tm = TM_IN
    per_seq = seq // tm
    row = lambda i: (i, 0)
    tab = lambda i: (i % per_seq, 0)
    return pl.pallas_call(
        _inproj_kernel,
        grid=(t // tm,),
        in_specs=[pl.BlockSpec((tm, d), row),
                  pl.BlockSpec((1,) + mod.shape[1:], lambda i: (i // per_seq, 0, 0)),
                  pl.BlockSpec((tm, aw), tab), pl.BlockSpec((tm, aw), tab),
                  _const_spec(wq.shape), _const_spec(wk.shape), _const_spec(wv.shape),
                  _const_spec(wu.shape), _const_spec(wvg.shape), _const_spec(wga.shape),
                  _const_spec(wgb.shape), _const_spec(lng.shape), _const_spec(lnb.shape),
                  _const_spec(wsp.shape), _const_spec(bsp.shape), _const_spec(wpb.shape)],
        out_specs=[pl.BlockSpec((tm, aw), row)] * 3 + [pl.BlockSpec((tm, d), row)] * 2,
        out_shape=[jax.ShapeDtypeStruct((t, aw), BF16)] * 3 + [jax.ShapeDtypeStruct((t, d), BF16)] * 2,
        scratch_shapes=[pltpu.VMEM((tm, gwid), F32), pltpu.VMEM((tm, gwid), BF16),
                        pltpu.VMEM((tm, gwid), BF16)],
        compiler_params=_cparams(("arbitrary",)),
        name="inproj",
    )(xf, mod, cos, sin, wq, wk, wv, wu, wvg, wga, wgb, lng, lnb, wsp, bsp, wpb)


def _rows(start, size, stride):
    if stride == 1:
        return pl.ds(start, size)
    return pl.ds(start, size, stride=stride)


def _band_aligned(start):
    return start if isinstance(start, int) else pl.multiple_of(start, BAND)


def _interleaved_loop(count, load, compute, store):
    per_trip = max(u for u in range(1, ATTN_BLOCKS_PER_TRIP + 1) if count % u == 0) if count else 1

    def trip(it, c):
        items = [it * per_trip + u for u in range(per_trip)]
        results = [compute(ops) for ops in [load(i) for i in items]]
        for i, res in zip(items, results):
            store(i, res)
        return c

    if count == per_trip:
        trip(0, 0)
    elif count:
        lax.fori_loop(0, count // per_trip, trip, 0)


def _attn_kernel(*refs):
    npat = len(DILATIONS)
    qkv = [refs[3 * p:3 * p + 3] for p in range(npat)]
    o_ref, out_scr, lse_scr = refs[3 * npat:]
    seq = o_ref.shape[1]

    lane = lax.broadcasted_iota(jnp.int32, (BAND, LANES), 1)
    head0 = lane < HEAD_DIM
    neg = jnp.float32(-1e30)

    qi = lax.broadcasted_iota(jnp.int32, (BAND, 2 * BAND), 0)
    kj = lax.broadcasted_iota(jnp.int32, (BAND, 2 * BAND), 1)

    def load(p, blk, nb):
        q_ref, k_ref, v_ref = qkv[p]
        q_start = blk * BAND
        k_start = jnp.maximum(q_start - BAND, 0)
        off = q_start - k_start
        lowest = jnp.where(blk % nb == 0, off, qi)
        mask = (kj >= lowest) & (kj <= qi + off)
        krows = pl.ds(_band_aligned(k_start), 2 * BAND)
        return (q_ref[0, 0, pl.ds(_band_aligned(q_start), BAND), :], k_ref[0, 0, krows, :],
                v_ref[0, 0, krows, :], mask)

    def attend(ops):
        qb, kb, vb, mask = ops
        zero = jnp.zeros(qb.shape, qb.dtype)
        parts = []
        for hd in range(2):
            qh = jnp.where(head0 if hd == 0 else ~head0, qb, zero)
            s = lax.dot_general(qh, kb, (((1,), (1,)), ((), ())), preferred_element_type=F32)
            s = jnp.where(mask, s, neg)
            m = jnp.max(s, axis=-1, keepdims=True)
            pr = jnp.exp(s - m)
            l = jnp.sum(pr, axis=-1, keepdims=True)
            acc = jnp.dot(pr.astype(BF16), vb, preferred_element_type=F32)
            parts.append((acc / l, m + jnp.log(l)))
        return [jnp.where(head0, a0, a1) for a0, a1 in zip(*parts)]

    def store(p, seq_start, stride, res):
        rows = _rows(seq_start if stride > 1 else _band_aligned(seq_start), BAND, stride)
        out_scr[p, rows, :] = res[0]
        lse_scr[p, rows, :] = res[1]

    for p, dil in enumerate(DILATIONS):
        n = seq // dil
        nb = n // BAND

        _interleaved_loop(
            seq // BAND, lambda blk, p=p, nb=nb: load(p, blk, nb), attend,
            lambda blk, res, p=p, dil=dil, nb=nb: store(p, blk // nb + dil * BAND * (blk % nb), dil, res))

    top = jnp.maximum(jnp.maximum(lse_scr[0], lse_scr[1]), lse_scr[2])
    num = jnp.zeros(top.shape, F32)
    den = jnp.zeros(top.shape, F32)
    for p in range(npat):
        w = jnp.exp(lse_scr[p] - top)
        num = num + w * out_scr[p]
        den = den + w
    o_ref[0] = (num / den).astype(BF16)


def _class_major(t, dil):
    bsz, seq, aw = t.shape
    t = t.reshape(bsz, seq // dil, dil, aw // LANES, LANES)
    return t.transpose(0, 3, 2, 1, 4).reshape(bsz, aw // LANES, seq, LANES)


def _attention(q, k, v):
    bsz, seq, aw = q.shape
    npat = len(DILATIONS)
    blk = (1, 1, seq, LANES)
    operands, specs = [], []
    for dil in DILATIONS:
        for t in (q, k, v):
            if dil == 1:
                operands.append(t.reshape(bsz, 1, seq, aw))
                specs.append(pl.BlockSpec(blk, lambda b, hp: (b, 0, 0, hp)))
            else:
                operands.append(_class_major(t, dil))
                specs.append(pl.BlockSpec(blk, lambda b, hp: (b, hp, 0, 0)))
    return pl.pallas_call(
        _attn_kernel,
        grid=(bsz, aw // LANES),
        in_specs=specs,
        out_specs=pl.BlockSpec((1, seq, LANES), lambda b, hp: (b, 0, hp)),
        out_shape=jax.ShapeDtypeStruct((bsz, seq, aw), BF16),
        scratch_shapes=[pltpu.VMEM((npat, seq, LANES), F32)] * 2,
        compiler_params=_cparams(("arbitrary", "arbitrary")),
        name="attn",
    )(*operands)


def _post_kernel(alpha, oa_ref, ga_ref, gbyb_ref, x_ref, mod_ref, wpa_ref, wo_ref, l1g_ref, l1b_ref,
                 wrt_ref, rb_ref, wgs_ref, wus_ref, wds_ref, tri_ref,
                 h2_ref, base_ref, e_ref, w_ref, r_ref, cnt_ref):
    tm = x_ref.shape[0]
    n_exp = wrt_ref.shape[0]
    per_group = n_exp // N_EXPERT_GROUPS
    gate_a = mod_ref[0, 2:3, :]
    shift_f = mod_ref[0, 3:4, :]
    scale_f = mod_ref[0, 4:5, :]
    gate_f = mod_ref[0, 5:6, :]

    y_a = jnp.dot(oa_ref[...], wpa_ref[...], preferred_element_type=F32)
    merged = ga_ref[...].astype(F32) * y_a + gbyb_ref[...].astype(F32)
    mix = jnp.dot(merged.astype(BF16), wo_ref[...], preferred_element_type=F32)
    x1 = _layer_norm(alpha * x_ref[...] + gate_a * mix, l1g_ref[...], l1b_ref[...])
    h2 = x1 * (1.0 + scale_f) + shift_f
    h2b = h2.astype(BF16)

    _store_token_rows(h2_ref, 0, h2)

    a = jnp.dot(h2b, wgs_ref[...], preferred_element_type=F32)
    b = jnp.dot(h2b, wus_ref[...], preferred_element_type=F32)
    shared = jnp.dot((a * _sigmoid(a) * b).astype(BF16), wds_ref[...], preferred_element_type=F32)
    base_ref[...] = alpha * x1 + gate_f * shared

    logits = lax.dot_general(wrt_ref[...], h2b, (((1,), (1,)), ((), ())), preferred_element_type=F32)
    scores = _sigmoid(logits)
    biased = scores + rb_ref[...]
    ninf = jnp.float32(-jnp.inf)

    sub = lax.broadcasted_iota(jnp.int32, (per_group, tm), 0)
    grp_rows = []
    for g in range(N_EXPERT_GROUPS):
        blk = biased[g * per_group:(g + 1) * per_group, :]
        m1 = jnp.max(blk, axis=0, keepdims=True)
        f1 = jnp.min(jnp.where(blk == m1, sub, per_group), axis=0, keepdims=True)
        m2 = jnp.max(jnp.where(sub == f1, ninf, blk), axis=0, keepdims=True)
        grp_rows.append(m1 + m2)
    grp = jnp.concatenate(grp_rows, axis=0)

    gi = lax.broadcasted_iota(jnp.int32, grp.shape, 0)
    chosen = jnp.zeros(grp.shape, jnp.bool_)
    cur = grp
    for _ in range(TOPK_GROUPS):
        m = jnp.max(cur, axis=0, keepdims=True)
        f = jnp.min(jnp.where(cur == m, gi, N_EXPERT_GROUPS), axis=0, keepdims=True)
        hit = gi == f
        chosen = chosen | hit
        cur = jnp.where(hit, ninf, cur)
    emask = jnp.concatenate(
        [jnp.broadcast_to(chosen[g:g + 1, :], (per_group, tm)) for g in range(N_EXPERT_GROUPS)], axis=0)

    ei = lax.broadcasted_iota(jnp.int32, (n_exp, tm), 0)
    cand = jnp.where(emask, biased, ninf)
    hits, ids, wts = [], [], []
    for _ in range(TOP_K):
        m = jnp.max(cand, axis=0, keepdims=True)
        f = jnp.min(jnp.where(cand == m, ei, n_exp), axis=0, keepdims=True)
        hit = ei == f
        wts.append(jnp.sum(jnp.where(hit, scores, 0.0), axis=0, keepdims=True))
        cand = jnp.where(hit, ninf, cand)
        hits.append(hit)
        ids.append(f)
    onehot = hits[0]
    for hit in hits[1:]:
        onehot = onehot | hit
    onehot = jnp.where(onehot, 1.0, 0.0)

    @pl.when(pl.program_id(0) == 0)
    def _():
        cnt_ref[...] = jnp.zeros(cnt_ref.shape, F32)

    before = jnp.dot(onehot.astype(BF16), tri_ref[...], preferred_element_type=F32) + cnt_ref[...]
    ranks = [jnp.sum(jnp.where(hit, before, 0.0), axis=0, keepdims=True) for hit in hits]
    cnt_ref[...] += jnp.sum(onehot, axis=1, keepdims=True)

    denom = wts[0]
    for wk in wts[1:]:
        denom = denom + wk
    e_ref[...] = jnp.concatenate(ids, axis=0)
    w_ref[...] = jnp.concatenate([ROUTED_SCALE * wk / denom for wk in wts], axis=0)
    r_ref[...] = jnp.concatenate(ranks, axis=0).astype(jnp.int32)


def _post(alpha, oa, ga, gbyb, xf, mod, wpa, wo, l1g, l1b, wrt, rb, wgs, wus, wds, seq):
    t, d = xf.shape
    aw = oa.shape[1]
    n_exp = wrt.shape[0]
    tm = TM_POST
    per_seq = seq // tm
    row = lambda i: (i, 0)
    col = lambda i: (0, i)
    tri = (jnp.arange(tm)[:, None] < jnp.arange(tm)[None, :]).astype(BF16)
    return pl.pallas_call(
        functools.partial(_post_kernel, alpha),
        grid=(t // tm,),
        in_specs=[pl.BlockSpec((tm, aw), row), pl.BlockSpec((tm, d), row), pl.BlockSpec((tm, d), row),
                  pl.BlockSpec((tm, d), row),
                  pl.BlockSpec((1,) + mod.shape[1:], lambda i: (i // per_seq, 0, 0)),
                  _const_spec(wpa.shape), _const_spec(wo.shape), _const_spec(l1g.shape),
                  _const_spec(l1b.shape), _const_spec(wrt.shape), _const_spec(rb.shape),
                  _const_spec(wgs.shape), _const_spec(wus.shape), _const_spec(wds.shape),
                  _const_spec(tri.shape)],
        out_specs=[pl.BlockSpec((tm * ROW_TILE, LANES), row), pl.BlockSpec((tm, d), row),
                   pl.BlockSpec((TOP_K, tm), col), pl.BlockSpec((TOP_K, tm), col),
                   pl.BlockSpec((TOP_K, tm), col), _const_spec((n_exp, 1))],
        out_shape=[jax.ShapeDtypeStruct((t * ROW_TILE, LANES), F32), jax.ShapeDtypeStruct((t, d), F32),
                   jax.ShapeDtypeStruct((TOP_K, t), jnp.int32), jax.ShapeDtypeStruct((TOP_K, t), F32),
                   jax.ShapeDtypeStruct((TOP_K, t), jnp.int32), jax.ShapeDtypeStruct((n_exp, 1), F32)],
        compiler_params=_cparams(("arbitrary",)),
        name="post",
    )(oa, ga, gbyb, xf, mod, wpa, wo, l1g, l1b, wrt, rb, wgs, wus, wds, tri)


def _row_copy(src_ref, src_row, dst_ref, dst_row, sem):
    return pltpu.make_async_copy(
        src_ref.at[pl.ds(pl.multiple_of(src_row * ROW_TILE, ROW_TILE), ROW_TILE), :],
        dst_ref.at[pl.ds(pl.multiple_of(dst_row * ROW_TILE, ROW_TILE), ROW_TILE), :], sem)


def _dest_kernel(rs_ref, e_ref, r_ref, o_ref):
    e = e_ref[...]
    start = lax.fori_loop(0, rs_ref.shape[0], lambda x, acc: jnp.where(e == x, rs_ref[x], acc),
                          jnp.zeros(e.shape, jnp.int32))
    o_ref[...] = start + r_ref[...]


def _dest(row_start, e_idx, rank):
    t = e_idx.shape[1]
    tm = min(t, 2048)
    blk = pl.BlockSpec((TOP_K, tm), lambda i, *_: (0, i))
    return pl.pallas_call(
        _dest_kernel,
        grid_spec=pltpu.PrefetchScalarGridSpec(num_scalar_prefetch=1, grid=(t // tm,), in_specs=[blk, blk],
                                               out_specs=blk),
        out_shape=jax.ShapeDtypeStruct((TOP_K, t), jnp.int32),
        compiler_params=_cparams(("arbitrary",)),
        name="dest",
    )(row_start, e_idx, rank)


def _dispatch_kernel(n_blocks, ps_ref, pl_ref, nu_ref, dest_ref, h2_ref, xs_ref, zbuf, sem, zsem):
    i = pl.program_id(0)
    n_steps = pl.num_programs(0)
    tm = dest_ref.shape[1]
    n_exp = ps_ref.shape[0]
    per_step = -(-n_exp // n_steps)
    zbuf[...] = jnp.zeros(zbuf.shape, F32)

    def zero_fill(act):
        for q in range(per_step):
            e = jnp.minimum(i * per_step + q, n_exp - 1)
            live = i * per_step + q < n_exp
            gap = pl_ref[e]
            for bit in [1 << s for s in range(BM.bit_length() - 1)]:
                @pl.when(live & ((gap & bit) != 0))
                def _(bit=bit, e=e, gap=gap):
                    off = ps_ref[e] + (gap & ~(2 * bit - 1))
                    act(pltpu.make_async_copy(
                        zbuf.at[pl.ds(0, bit * ROW_TILE), :],
                        xs_ref.at[pl.ds(pl.multiple_of(off * ROW_TILE, ROW_TILE), bit * ROW_TILE), :], zsem))
            blk = nu_ref[0] + i * per_step + q
            @pl.when(blk < n_blocks)
            def _(blk=blk):
                rows = BM * ROW_TILE
                act(pltpu.make_async_copy(
                    zbuf, xs_ref.at[pl.ds(pl.multiple_of(blk * rows, rows), rows), :], zsem))

    zero_fill(lambda cp: cp.start(priority=1))

    def issue(j, c):
        for k in range(TOP_K):
            _row_copy(h2_ref, j, xs_ref, dest_ref[k, j], sem).start(priority=k % 2)
        return c

    lax.fori_loop(0, tm, issue, 0)
    n = tm * TOP_K * ROW_TILE
    pltpu.make_async_copy(xs_ref.at[pl.ds(0, n), :], xs_ref.at[pl.ds(0, n), :], sem).wait()
    zero_fill(lambda cp: cp.wait())


def _dispatch(pad_start, pad_len, n_used, dest, h2t, n_blocks):
    t = dest.shape[1]
    tm = TM_DISPATCH
    grid_spec = pltpu.PrefetchScalarGridSpec(
        num_scalar_prefetch=3,
        grid=(t // tm,),
        in_specs=[pl.BlockSpec((TOP_K, tm), lambda i, *_: (0, i), memory_space=pltpu.SMEM),
                  pl.BlockSpec((tm * ROW_TILE, LANES), lambda i, *_: (i, 0))],
        out_specs=pl.BlockSpec(memory_space=pl.ANY),
        scratch_shapes=[pltpu.VMEM((BM * ROW_TILE, LANES), F32), pltpu.SemaphoreType.DMA,
                        pltpu.SemaphoreType.DMA],
    )
    return pl.pallas_call(
        functools.partial(_dispatch_kernel, n_blocks),
        grid_spec=grid_spec,
        out_shape=jax.ShapeDtypeStruct((n_blocks * BM * ROW_TILE, LANES), F32),
        compiler_params=_cparams(("arbitrary",)),
        name="dispatch",
    )(pad_start, pad_len, n_used, dest, h2t)


def _expert_kernel(nblk_ref, bstart_ref, widx_ref, nu_ref, xs_ref, wg_ref, wu_ref, wd_ref, ys_ref,
                   xbuf, ybuf, wg_b, wu_b, wd_b, xsem, ysem):
    e = pl.program_id(0)
    n_used = nu_ref[0]
    rows = BM * ROW_TILE
    ring = EXPERT_RING

    def slot_base(g):
        return pl.multiple_of((g % ring) * rows, rows)

    def hbm_rows(ref, g):
        return ref.at[pl.ds(pl.multiple_of(g * rows, rows), rows), :]

    def x_copy(g):
        return pltpu.make_async_copy(hbm_rows(xs_ref, g), xbuf.at[pl.ds(slot_base(g), rows), :], xsem.at[g % ring])

    def y_copy(g):
        return pltpu.make_async_copy(ybuf.at[pl.ds(slot_base(g), rows), :], hbm_rows(ys_ref, g), ysem.at[g % ring])

    @pl.when(e == 0)
    def _():
        for g0 in range(ring - 1):
            @pl.when(g0 < n_used)
            def _(g0=g0):
                x_copy(g0).start()

    wg_b[...] = wg_ref[0].astype(BF16)
    wu_b[...] = wu_ref[0].astype(BF16)
    wd_b[...] = wd_ref[0].astype(BF16)

    def block(b, c):
        g = bstart_ref[e] + b
        x_copy(g).wait()

        @pl.when(g + ring - 1 < n_used)
        def _():
            x_copy(g + ring - 1).start()

        @pl.when(g >= ring)
        def _():
            y_copy(g - ring).wait()

        x = _load_token_rows(xbuf, slot_base(g), BM).astype(BF16)
        a = jnp.dot(x, wg_b[...], preferred_element_type=F32)
        u = jnp.dot(x, wu_b[...], preferred_element_type=F32)
        act = (a * _sigmoid(a) * u).astype(BF16)
        _store_token_rows(ybuf, slot_base(g), jnp.dot(act, wd_b[...], preferred_element_type=F32))
        y_copy(g).start(priority=1)
        return c

    lax.fori_loop(0, nblk_ref[e], block, 0)

    @pl.when(e == pl.num_programs(0) - 1)
    def _():
        for back in range(ring, 0, -1):
            @pl.when(n_used >= back)
            def _(back=back):
                y_copy(n_used - back).wait()


def _experts(nblk, blk_start, w_idx, n_used, xs, wg, wu, wd):
    n_exp, d, de = wg.shape
    rows = BM * ROW_TILE
    wsel = lambda e, nb, bs, wi, nu: (wi[e], 0, 0)
    grid_spec = pltpu.PrefetchScalarGridSpec(
        num_scalar_prefetch=4,
        grid=(n_exp,),
        in_specs=[pl.BlockSpec(memory_space=pl.ANY),
                  pl.BlockSpec((1, d, de), wsel), pl.BlockSpec((1, d, de), wsel),
                  pl.BlockSpec((1, de, d), wsel)],
        out_specs=pl.BlockSpec(memory_space=pl.ANY),
        scratch_shapes=[pltpu.VMEM((EXPERT_RING * rows, LANES), F32), pltpu.VMEM((EXPERT_RING * rows, LANES), F32),
                        pltpu.VMEM((d, de), BF16), pltpu.VMEM((d, de), BF16), pltpu.VMEM((de, d), BF16),
                        pltpu.SemaphoreType.DMA((EXPERT_RING,)), pltpu.SemaphoreType.DMA((EXPERT_RING,))],
    )
    return pl.pallas_call(
        _expert_kernel,
        grid_spec=grid_spec,
        out_shape=jax.ShapeDtypeStruct(xs.shape, xs.dtype),
        input_output_aliases={4: 0},
        compiler_params=_cparams(("arbitrary",)),
        name="experts",
    )(nblk, blk_start, w_idx, n_used, xs, wg, wu, wd)


def _combine_kernel(dest_ref, next_dest_ref, w_ref, base_ref, mod_ref, g_ref, b_ref, ys_ref, o_ref, buf, sems):
    i = pl.program_id(0)
    tm = dest_ref.shape[1]
    slab = tm * ROW_TILE
    slot_rows = TOP_K * slab

    def gather(idx_ref, slot):
        def issue(j, c):
            for k in range(TOP_K):
                _row_copy(ys_ref, idx_ref[k, j], buf, slot * (TOP_K * tm) + k * tm + j,
                          sems.at[slot]).start(priority=k % 2)
            return c
        lax.fori_loop(0, tm, issue, 0)

    @pl.when(i == 0)
    def _():
        gather(dest_ref, 0)

    @pl.when(i + 1 < pl.num_programs(0))
    def _():
        gather(next_dest_ref, (i + 1) % 2)

    slot = i % 2
    base_row = pl.multiple_of(slot * slot_rows, slot_rows)
    pltpu.make_async_copy(ys_ref.at[pl.ds(0, slot_rows), :], buf.at[pl.ds(base_row, slot_rows), :],
                          sems.at[slot]).wait()

    wpad = jnp.concatenate([w_ref[...], jnp.zeros((LANES - TOP_K, tm), F32)], axis=0)
    wcol = wpad.T
    routed = jnp.zeros(base_ref.shape, F32)
    for k in range(TOP_K):
        routed = routed + wcol[:, k:k + 1] * _load_token_rows(buf, base_row + k * slab, tm)
    gate_f = mod_ref[0, 5:6, :]
    o_ref[...] = _layer_norm(base_ref[...] + gate_f * routed, g_ref[...], b_ref[...])


def _combine(dest, wts, base, mod, g, b, ys, seq):
    t, d = base.shape
    tm = TM_COMBINE
    per_seq = seq // tm
    n_steps = t // tm
    col = lambda i: (0, i)
    row = lambda i: (i, 0)
    return pl.pallas_call(
        _combine_kernel,
        grid=(n_steps,),
        in_specs=[pl.BlockSpec((TOP_K, tm), col, memory_space=pltpu.SMEM),
                  pl.BlockSpec((TOP_K, tm), lambda i: (0, jnp.minimum(i + 1, n_steps - 1)),
                               memory_space=pltpu.SMEM),
                  pl.BlockSpec((TOP_K, tm), col), pl.BlockSpec((tm, d), row),
                  pl.BlockSpec((1,) + mod.shape[1:], lambda i: (i // per_seq, 0, 0)),
                  _const_spec(g.shape), _const_spec(b.shape),
                  pl.BlockSpec(memory_space=pl.ANY)],
        out_specs=pl.BlockSpec((tm, d), row),
        out_shape=jax.ShapeDtypeStruct((t, d), F32),
        scratch_shapes=[pltpu.VMEM((2 * TOP_K * tm * ROW_TILE, LANES), F32), pltpu.SemaphoreType.DMA((2,))],
        compiler_params=_cparams(("arbitrary",)),
        name="combine",
    )(dest, dest, wts, base, mod, g, b, ys)


def _rope_tables(seq):
    half = HEAD_DIM // 2
    inv_freq = ROPE_THETA ** (-jnp.arange(half, dtype=F32) / half)
    ang = jnp.arange(seq, dtype=F32)[:, None] * inv_freq[None, :]
    cos = jnp.concatenate([jnp.cos(ang), jnp.cos(ang)], axis=1)
    sin = jnp.concatenate([-jnp.sin(ang), jnp.sin(ang)], axis=1)
    return jnp.tile(cos, (1, N_HEADS)), jnp.tile(sin, (1, N_HEADS))


def _layer(x, c, w_ada, b_ada, w_in, ln_v_g, ln_v_b, w_spatial, b_spatial, w_proj_a, w_proj_b, w_out,
           ln1_g, ln1_b, w_router, router_bias, w_gate_e, w_up_e, w_down_e, w_gate_sh, w_up_sh,
           w_down_sh, ln2_g, ln2_b, alpha):
    bsz, seq, d = x.shape
    assert seq % (max(DILATIONS) * BAND) == 0 and seq % TM_IN == 0 and d == ROW_TILE * LANES
    t = bsz * seq
    aw = N_HEADS * HEAD_DIM
    gwid = ln_v_g.shape[0]
    n_exp = w_router.shape[1]
    xf = x.reshape(t, d)

    mod = _ada(c, w_ada, b_ada).reshape(bsz, 6, d)

    cuts = [0, aw, 2 * aw, 3 * aw, 3 * aw + gwid, 3 * aw + 2 * gwid, 3 * aw + 2 * gwid + d,
            3 * aw + 2 * gwid + 2 * d]
    wq, wk, wv, wu, wvg, wga, wgb = [w_in[:, a:b].astype(BF16) for a, b in zip(cuts[:-1], cuts[1:])]
    cos, sin = _rope_tables(seq)
    bsp = jnp.repeat(b_spatial.T, gwid // GMLP_GROUPS, axis=1)
    q, k, v, ga, gbyb = _inproj(xf, mod, cos, sin, wq, wk, wv, wu, wvg, wga, wgb,
                                ln_v_g.reshape(1, gwid), ln_v_b.reshape(1, gwid), w_spatial, bsp,
                                w_proj_b.astype(BF16), seq)

    oa = _attention(q.reshape(bsz, seq, aw), k.reshape(bsz, seq, aw), v.reshape(bsz, seq, aw))

    h2t, base, e_idx, wts, rank, cnt = _post(
        alpha, oa.reshape(t, aw), ga, gbyb, xf, mod, w_proj_a.astype(BF16), w_out.astype(BF16),
        ln1_g.reshape(1, d), ln1_b.reshape(1, d), w_router.T.astype(BF16), router_bias.reshape(n_exp, 1),
        w_gate_sh.astype(BF16), w_up_sh.astype(BF16), w_down_sh.astype(BF16), seq)

    counts = cnt[:, 0].astype(jnp.int32)
    nblk = ((counts + BM - 1) // BM).astype(jnp.int32)
    blk_end = jnp.cumsum(nblk).astype(jnp.int32)
    blk_start = blk_end - nblk
    n_blocks = (t * TOP_K) // BM + n_exp
    n_used = blk_end[-1:]
    row_start = blk_start * BM
    ids = jnp.arange(n_exp, dtype=jnp.int32)
    seen = jnp.where((ids[None, :] <= ids[:, None]) & (nblk[None, :] > 0), ids[None, :], -1).max(axis=1)
    w_idx = jnp.where(seen >= 0, seen, jnp.where(nblk > 0, ids, n_exp - 1).min()).astype(jnp.int32)

    dest = _dest(row_start, e_idx, rank)
    xs = _dispatch(row_start + counts, nblk * BM - counts, n_used, dest, h2t, n_blocks)
    ys = _experts(nblk, blk_start, w_idx, n_used, xs, w_gate_e, w_up_e, w_down_e)
    out = _combine(dest, wts, base, mod, ln2_g.reshape(1, d), ln2_b.reshape(1, d), ys, seq)
    return out.reshape(bsz, seq, d)


def kernel(x, c, w_ada, b_ada, w_in, ln_v_g, ln_v_b, w_spatial, b_spatial, w_proj_a, w_proj_b, w_out,
           ln1_g, ln1_b, w_router, router_bias, w_gate_e, w_up_e, w_down_e, w_gate_sh, w_up_sh,
           w_down_sh, ln2_g, ln2_b):
    depth = w_ada.shape[0]
    alpha = (2.0 * depth) ** 0.25
    for l in range(depth):
        x = _layer(x, c, w_ada[l], b_ada[l], w_in[l], ln_v_g[l], ln_v_b[l], w_spatial[l], b_spatial[l],
                   w_proj_a[l], w_proj_b[l], w_out[l], ln1_g[l], ln1_b[l], w_router[l], router_bias[l],
                   w_gate_e[l], w_up_e[l], w_down_e[l], w_gate_sh[l], w_up_sh[l], w_down_sh[l],
                   ln2_g[l], ln2_b[l], alpha)
    return x
```

```python
import functools
import math

import jax
import jax.numpy as jnp
from jax import lax
from jax.experimental import pallas as pl
from jax.experimental.pallas import tpu as pltpu

F32 = jnp.float32
BF16 = jnp.bfloat16

N_HEADS = 8
HEAD_DIM = 64
DILATIONS = (1, 4, 16)
BAND = 128
ROPE_THETA = 10000.0
GMLP_GROUPS = 8
GMLP_CHUNK = 128
N_EXPERT_GROUPS = 8
TOPK_GROUPS = 4
TOP_K = 8
ROUTED_SCALE = 2.5
LN_EPS = 1e-5
LANES = 128
SUBLANES = 8
ROW_TILE = 8
VMEM_LIMIT = 56 * 1024 * 1024

TM_IN = 512
TM_POST = 256
TM_DISPATCH = 512
TM_COMBINE = 128
BM = 256
EXPERT_RING = 4
ATTN_BLOCKS_PER_TRIP = 4


def _cparams(sem):
    return pltpu.CompilerParams(dimension_semantics=sem, vmem_limit_bytes=VMEM_LIMIT)


def _const_spec(shape):
    nd = len(shape)
    return pl.BlockSpec(shape, lambda *_: (0,) * nd)


def _layer_norm(r, g, b):
    mu = jnp.mean(r, axis=-1, keepdims=True)
    d = r - mu
    var = jnp.mean(d * d, axis=-1, keepdims=True)
    return d * lax.rsqrt(var + LN_EPS) * g + b


def _gelu(t):
    return 0.5 * t * (1.0 + lax.erf(t * (1.0 / math.sqrt(2.0))))


def _sigmoid(t):
    return 1.0 / (1.0 + jnp.exp(-t))


def _store_token_rows(ref, first_row, v):
    m = v.shape[0]
    for s in range(ROW_TILE):
        ref[pl.ds(first_row + s, m, stride=ROW_TILE), :] = v[:, s * LANES:(s + 1) * LANES]


def _load_token_rows(ref, first_row, m):
    return jnp.concatenate([ref[pl.ds(first_row + s, m, stride=ROW_TILE), :] for s in range(ROW_TILE)], axis=1)


def _ada_kernel(c_ref, w_ref, b_ref, o_ref):
    c = c_ref[...]
    o_ref[...] = jnp.dot(c * _sigmoid(c), w_ref[...], preferred_element_type=F32) + b_ref[...]


def _ada(c, w, b):
    bsz, d = c.shape
    n = w.shape[1]
    tn = 1024
    return pl.pallas_call(
        _ada_kernel,
        grid=(n // tn,),
        in_specs=[_const_spec((bsz, d)), pl.BlockSpec((d, tn), lambda j: (0, j)),
                  pl.BlockSpec((1, tn), lambda j: (0, j))],
        out_specs=pl.BlockSpec((bsz, tn), lambda j: (0, j)),
        out_shape=jax.ShapeDtypeStruct((bsz, n), F32),
        compiler_params=_cparams(("arbitrary",)),
        name="ada",
    )(c, w, b.reshape(1, n))


def _inproj_kernel(x_ref, mod_ref, cos_ref, sin_ref, wq_ref, wk_ref, wv_ref, wu_ref, wvg_ref,
                   wga_ref, wgb_ref, lng_ref, lnb_ref, wsp_ref, bsp_ref, wpb_ref, *rest):
    n_cm = len(DILATIONS) - 1
    q_refs, k_refs, v_refs = [rest[(1 + n_cm) * a:(1 + n_cm) * (a + 1)] for a in range(3)]
    ga_ref, gbyb_ref, u_scr, vv_scr, gated_scr, q_stage, k_stage, v_stage = rest[3 * (1 + n_cm):]
    tm = x_ref.shape[0]

    def emit(val, refs, stage):
        refs[0][...] = val.astype(BF16)
        for hp in range(val.shape[1] // LANES):
            stage[pl.ds(hp * tm, tm), :] = val[:, hp * LANES:(hp + 1) * LANES]
        for ref, dil in zip(refs[1:], DILATIONS[1:]):
            for hp in range(val.shape[1] // LANES):
                for r in range(dil):
                    ref[0, hp, r, :, :] = stage[pl.ds(hp * tm + r, tm // dil, stride=dil), :].astype(BF16)
    shift = mod_ref[0, 0:1, :]
    scale = mod_ref[0, 1:2, :]
    h = (x_ref[...] * (1.0 + scale) + shift).astype(BF16)

    cos = cos_ref[...]
    sin = sin_ref[...]
    aw = cos.shape[1]
    lane = lax.broadcasted_iota(jnp.int32, (tm, aw), 1)
    first_half = (lane % HEAD_DIM) < (HEAD_DIM // 2)

    def rope(t):
        partner = jnp.where(first_half, pltpu.roll(t, aw - HEAD_DIM // 2, 1),
                            pltpu.roll(t, HEAD_DIM // 2, 1))
        return t * cos + partner * sin

    emit(rope(jnp.dot(h, wq_ref[...], preferred_element_type=F32)) * (HEAD_DIM ** -0.5), q_refs, q_stage)
    emit(rope(jnp.dot(h, wk_ref[...], preferred_element_type=F32)), k_refs, k_stage)
    emit(jnp.dot(h, wv_ref[...], preferred_element_type=F32), v_refs, v_stage)

    u_scr[...] = _gelu(jnp.dot(h, wu_ref[...], preferred_element_type=F32))
    vg = _gelu(jnp.dot(h, wvg_ref[...], preferred_element_type=F32))
    vv_scr[...] = _layer_norm(vg, lng_ref[...], lnb_ref[...]).astype(BF16)

    ti = lax.broadcasted_iota(jnp.int32, (GMLP_CHUNK, GMLP_CHUNK), 0)
    si = lax.broadcasted_iota(jnp.int32, (GMLP_CHUNK, GMLP_CHUNK), 1)
    causal = si <= ti
    gw = u_scr.shape[1] // GMLP_GROUPS
    for g in range(GMLP_GROUPS):
        ws = jnp.where(causal, wsp_ref[g], 0.0).astype(BF16)
        cols = slice(g * gw, (g + 1) * gw)
        for c in range(tm // GMLP_CHUNK):
            rows = slice(c * GMLP_CHUNK, (c + 1) * GMLP_CHUNK)
            sv = jnp.dot(ws, vv_scr[rows, cols], preferred_element_type=F32) + bsp_ref[:, cols]
            gated_scr[rows, cols] = (u_scr[rows, cols] * sv).astype(BF16)

    y_b = jnp.dot(gated_scr[...], wpb_ref[...], preferred_element_type=F32)
    g_b = _sigmoid(jnp.dot(h, wgb_ref[...], preferred_element_type=F32))
    gbyb_ref[...] = (g_b * y_b).astype(BF16)
    ga_ref[...] = _sigmoid(jnp.dot(h, wga_ref[...], preferred_element_type=F32)).astype(BF16)


def _inproj(xf, mod, cos, sin, wq, wk, wv, wu, wvg, wga, wgb, lng, lnb, wsp, bsp, wpb, seq):
    t, d = xf.shape
    aw = wq.shape[1]
    gwid = wu.shape[1]
    tm = TM_IN
    per_seq = seq // tm
    row = lambda i: (i, 0)
    tab = lambda i: (i % per_seq, 0)
    bsz = t // seq
    qkv_specs = [pl.BlockSpec((tm, aw), row)] + [
        pl.BlockSpec((1, aw // LANES, dil, tm // dil, LANES), lambda i: (i // per_seq, 0, 0, i % per_seq, 0))
        for dil in DILATIONS[1:]]
    qkv_shapes = [jax.ShapeDtypeStruct((t, aw), BF16)] + [
        jax.ShapeDtypeStruct((bsz, aw // LANES, dil, seq // dil, LANES), BF16) for dil in DILATIONS[1:]]
    outs = pl.pallas_call(
        _inproj_kernel,
        grid=(t // tm,),
        in_specs=[pl.BlockSpec((tm, d), row),
                  pl.BlockSpec((1,) + mod.shape[1:], lambda i: (i // per_seq, 0, 0)),
                  pl.BlockSpec((tm, aw), tab), pl.BlockSpec((tm, aw), tab),
                  _const_spec(wq.shape), _const_spec(wk.shape), _const_spec(wv.shape),
                  _const_spec(wu.shape), _const_spec(wvg.shape), _const_spec(wga.shape),
                  _const_spec(wgb.shape), _const_spec(lng.shape), _const_spec(lnb.shape),
                  _const_spec(wsp.shape), _const_spec(bsp.shape), _const_spec(wpb.shape)],
        out_specs=qkv_specs * 3 + [pl.BlockSpec((tm, d), row)] * 2,
        out_shape=qkv_shapes * 3 + [jax.ShapeDtypeStruct((t, d), BF16)] * 2,
        scratch_shapes=[pltpu.VMEM((tm, gwid), F32), pltpu.VMEM((tm, gwid), BF16),
                        pltpu.VMEM((tm, gwid), BF16)] + [pltpu.VMEM((tm * aw // LANES, LANES), F32)] * 3,
        compiler_params=_cparams(("arbitrary",)),
        name="inproj",
    )(xf, mod, cos, sin, wq, wk, wv, wu, wvg, wga, wgb, lng, lnb, wsp, bsp, wpb)
    n_lay = len(DILATIONS)
    return [outs[a * n_lay:(a + 1) * n_lay] for a in range(3)], outs[3 * n_lay], outs[3 * n_lay + 1]


def _rows(start, size, stride):
    if stride == 1:
        return pl.ds(start, size)
    return pl.ds(start, size, stride=stride)


def _band_aligned(start):
    return start if isinstance(start, int) else pl.multiple_of(start, BAND)


def _interleaved_loop(count, load, compute, store):
    per_trip = max(u for u in range(1, ATTN_BLOCKS_PER_TRIP + 1) if count % u == 0) if count else 1

    def trip(it, c):
        items = [it * per_trip + u for u in range(per_trip)]
        results = [compute(ops) for ops in [load(i) for i in items]]
        for i, res in zip(items, results):
            store(i, res)
        return c

    if count == per_trip:
        trip(0, 0)
    elif count:
        lax.fori_loop(0, count // per_trip, trip, 0)


def _attn_kernel(*refs):
    npat = len(DILATIONS)
    qkv = [refs[3 * p:3 * p + 3] for p in range(npat)]
    o_ref, out_scr, lse_scr = refs[3 * npat:]
    seq = o_ref.shape[1]

    lane = lax.broadcasted_iota(jnp.int32, (BAND, LANES), 1)
    head0 = lane < HEAD_DIM
    neg = jnp.float32(-1e30)

    qi = lax.broadcasted_iota(jnp.int32, (BAND, 2 * BAND), 0)
    kj = lax.broadcasted_iota(jnp.int32, (BAND, 2 * BAND), 1)

    def load(p, blk, nb):
        q_ref, k_ref, v_ref = qkv[p]
        q_start = blk * BAND
        k_start = jnp.maximum(q_start - BAND, 0)
        off = q_start - k_start
        lowest = jnp.where(blk % nb == 0, off, qi)
        mask = (kj >= lowest) & (kj <= qi + off)
        krows = pl.ds(_band_aligned(k_start), 2 * BAND)
        return (q_ref[0, 0, pl.ds(_band_aligned(q_start), BAND), :], k_ref[0, 0, krows, :],
                v_ref[0, 0, krows, :], mask)

    def attend(ops):
        qb, kb, vb, mask = ops
        zero = jnp.zeros(qb.shape, qb.dtype)
        parts = []
        for hd in range(2):
            qh = jnp.where(head0 if hd == 0 else ~head0, qb, zero)
            s = lax.dot_general(qh, kb, (((1,), (1,)), ((), ())), preferred_element_type=F32)
            s = jnp.where(mask, s, neg)
            m = jnp.max(s, axis=-1, keepdims=True)
            pr = jnp.exp(s - m)
            l = jnp.sum(pr, axis=-1, keepdims=True)
            acc = jnp.dot(pr.astype(BF16), vb, preferred_element_type=F32)
            parts.append((acc / l, m + jnp.log(l)))
        return [jnp.where(head0, a0, a1) for a0, a1 in zip(*parts)]

    def store(p, seq_start, stride, res):
        rows = _rows(seq_start if stride > 1 else _band_aligned(seq_start), BAND, stride)
        out_scr[p, rows, :] = res[0]
        lse_scr[p, rows, :] = res[1]

    for p, dil in enumerate(DILATIONS):
        n = seq // dil
        nb = n // BAND

        _interleaved_loop(
            seq // BAND, lambda blk, p=p, nb=nb: load(p, blk, nb), attend,
            lambda blk, res, p=p, dil=dil, nb=nb: store(p, blk // nb + dil * BAND * (blk % nb), dil, res))

    top = jnp.maximum(jnp.maximum(lse_scr[0], lse_scr[1]), lse_scr[2])
    num = jnp.zeros(top.shape, F32)
    den = jnp.zeros(top.shape, F32)
    for p in range(npat):
        w = jnp.exp(lse_scr[p] - top)
        num = num + w * out_scr[p]
        den = den + w
    o_ref[0] = (num / den).astype(BF16)


def _attention(qkv, bsz, seq):
    aw = qkv[0][0].shape[1]
    npat = len(DILATIONS)
    blk = (1, 1, seq, LANES)
    operands, specs = [], []
    for p, dil in enumerate(DILATIONS):
        for layouts in qkv:
            if dil == 1:
                operands.append(layouts[p].reshape(bsz, 1, seq, aw))
                specs.append(pl.BlockSpec(blk, lambda b, hp: (b, 0, 0, hp)))
            else:
                operands.append(layouts[p].reshape(bsz, aw // LANES, seq, LANES))
                specs.append(pl.BlockSpec(blk, lambda b, hp: (b, hp, 0, 0)))
    return pl.pallas_call(
        _attn_kernel,
        grid=(bsz, aw // LANES),
        in_specs=specs,
        out_specs=pl.BlockSpec((1, seq, LANES), lambda b, hp: (b, 0, hp)),
        out_shape=jax.ShapeDtypeStruct((bsz, seq, aw), BF16),
        scratch_shapes=[pltpu.VMEM((npat, seq, LANES), F32)] * 2,
        compiler_params=_cparams(("arbitrary", "arbitrary")),
        name="attn",
    )(*operands)


def _post_kernel(alpha, oa_ref, ga_ref, gbyb_ref, x_ref, mod_ref, wpa_ref, wo_ref, l1g_ref, l1b_ref,
                 wrt_ref, rb_ref, wgs_ref, wus_ref, wds_ref, tri_ref,
                 h2_ref, base_ref, e_ref, w_ref, r_ref, cnt_ref):
    tm = x_ref.shape[0]
    n_exp = wrt_ref.shape[0]
    per_group = n_exp // N_EXPERT_GROUPS
    gate_a = mod_ref[0, 2:3, :]
    shift_f = mod_ref[0, 3:4, :]
    scale_f = mod_ref[0, 4:5, :]
    gate_f = mod_ref[0, 5:6, :]

    y_a = jnp.dot(oa_ref[...], wpa_ref[...], preferred_element_type=F32)
    merged = ga_ref[...].astype(F32) * y_a + gbyb_ref[...].astype(F32)
    mix = jnp.dot(merged.astype(BF16), wo_ref[...], preferred_element_type=F32)
    x1 = _layer_norm(alpha * x_ref[...] + gate_a * mix, l1g_ref[...], l1b_ref[...])
    h2 = x1 * (1.0 + scale_f) + shift_f
    h2b = h2.astype(BF16)

    _store_token_rows(h2_ref, 0, h2)

    a = jnp.dot(h2b, wgs_ref[...], preferred_element_type=F32)
    b = jnp.dot(h2b, wus_ref[...], preferred_element_type=F32)
    shared = jnp.dot((a * _sigmoid(a) * b).astype(BF16), wds_ref[...], preferred_element_type=F32)
    base_ref[...] = alpha * x1 + gate_f * shared

    logits = lax.dot_general(wrt_ref[...], h2b, (((1,), (1,)), ((), ())), preferred_element_type=F32)
    scores = _sigmoid(logits)
    biased = scores + rb_ref[...]
    ninf = jnp.float32(-jnp.inf)

    sub = lax.broadcasted_iota(jnp.int32, (per_group, tm), 0)
    grp_rows = []
    for g in range(N_EXPERT_GROUPS):
        blk = biased[g * per_group:(g + 1) * per_group, :]
        m1 = jnp.max(blk, axis=0, keepdims=True)
        f1 = jnp.min(jnp.where(blk == m1, sub, per_group), axis=0, keepdims=True)
        m2 = jnp.max(jnp.where(sub == f1, ninf, blk), axis=0, keepdims=True)
        grp_rows.append(m1 + m2)
    grp = jnp.concatenate(grp_rows, axis=0)

    gi = lax.broadcasted_iota(jnp.int32, grp.shape, 0)
    chosen = jnp.zeros(grp.shape, jnp.bool_)
    cur = grp
    for _ in range(TOPK_GROUPS):
        m = jnp.max(cur, axis=0, keepdims=True)
        f = jnp.min(jnp.where(cur == m, gi, N_EXPERT_GROUPS), axis=0, keepdims=True)
        hit = gi == f
        chosen = chosen | hit
        cur = jnp.where(hit, ninf, cur)
    emask = jnp.concatenate(
        [jnp.broadcast_to(chosen[g:g + 1, :], (per_group, tm)) for g in range(N_EXPERT_GROUPS)], axis=0)

    ei = lax.broadcasted_iota(jnp.int32, (n_exp, tm), 0)
    cand = jnp.where(emask, biased, ninf)
    hits, ids, wts = [], [], []
    for _ in range(TOP_K):
        m = jnp.max(cand, axis=0, keepdims=True)
        f = jnp.min(jnp.where(cand == m, ei, n_exp), axis=0, keepdims=True)
        hit = ei == f
        wts.append(jnp.sum(jnp.where(hit, scores, 0.0), axis=0, keepdims=True))
        cand = jnp.where(hit, ninf, cand)
        hits.append(hit)
        ids.append(f)
    onehot = hits[0]
    for hit in hits[1:]:
        onehot = onehot | hit
    onehot = jnp.where(onehot, 1.0, 0.0)

    @pl.when(pl.program_id(0) == 0)
    def _():
        cnt_ref[...] = jnp.zeros(cnt_ref.shape, F32)

    before = jnp.dot(onehot.astype(BF16), tri_ref[...], preferred_element_type=F32) + cnt_ref[...]
    ranks = [jnp.sum(jnp.where(hit, before, 0.0), axis=0, keepdims=True) for hit in hits]
    cnt_ref[...] += jnp.sum(onehot, axis=1, keepdims=True)

    denom = wts[0]
    for wk in wts[1:]:
        denom = denom + wk
    e_ref[...] = jnp.concatenate(ids, axis=0)
    w_ref[...] = jnp.concatenate([ROUTED_SCALE * wk / denom for wk in wts], axis=0)
    r_ref[...] = jnp.concatenate(ranks, axis=0).astype(jnp.int32)


def _post(alpha, oa, ga, gbyb, xf, mod, wpa, wo, l1g, l1b, wrt, rb, wgs, wus, wds, seq):
    t, d = xf.shape
    aw = oa.shape[1]
    n_exp = wrt.shape[0]
    tm = TM_POST
    per_seq = seq // tm
    row = lambda i: (i, 0)
    col = lambda i: (0, i)
    tri = (jnp.arange(tm)[:, None] < jnp.arange(tm)[None, :]).astype(BF16)
    return pl.pallas_call(
        functools.partial(_post_kernel, alpha),
        grid=(t // tm,),
        in_specs=[pl.BlockSpec((tm, aw), row), pl.BlockSpec((tm, d), row), pl.BlockSpec((tm, d), row),
                  pl.BlockSpec((tm, d), row),
                  pl.BlockSpec((1,) + mod.shape[1:], lambda i: (i // per_seq, 0, 0)),
                  _const_spec(wpa.shape), _const_spec(wo.shape), _const_spec(l1g.shape),
                  _const_spec(l1b.shape), _const_spec(wrt.shape), _const_spec(rb.shape),
                  _const_spec(wgs.shape), _const_spec(wus.shape), _const_spec(wds.shape),
                  _const_spec(tri.shape)],
        out_specs=[pl.BlockSpec((tm * ROW_TILE, LANES), row), pl.BlockSpec((tm, d), row),
                   pl.BlockSpec((TOP_K, tm), col), pl.BlockSpec((TOP_K, tm), col),
                   pl.BlockSpec((TOP_K, tm), col), _const_spec((n_exp, 1))],
        out_shape=[jax.ShapeDtypeStruct((t * ROW_TILE, LANES), F32), jax.ShapeDtypeStruct((t, d), F32),
                   jax.ShapeDtypeStruct((TOP_K, t), jnp.int32), jax.ShapeDtypeStruct((TOP_K, t), F32),
                   jax.ShapeDtypeStruct((TOP_K, t), jnp.int32), jax.ShapeDtypeStruct((n_exp, 1), F32)],
        compiler_params=_cparams(("arbitrary",)),
        name="post",
    )(oa, ga, gbyb, xf, mod, wpa, wo, l1g, l1b, wrt, rb, wgs, wus, wds, tri)


def _row_copy(src_ref, src_row, dst_ref, dst_row, sem):
    return pltpu.make_async_copy(
        src_ref.at[pl.ds(pl.multiple_of(src_row * ROW_TILE, ROW_TILE), ROW_TILE), :],
        dst_ref.at[pl.ds(pl.multiple_of(dst_row * ROW_TILE, ROW_TILE), ROW_TILE), :], sem)


def _dest_kernel(rs_ref, e_ref, r_ref, o_ref):
    e = e_ref[...]
    start = lax.fori_loop(0, rs_ref.shape[0], lambda x, acc: jnp.where(e == x, rs_ref[x], acc),
                          jnp.zeros(e.shape, jnp.int32))
    o_ref[...] = start + r_ref[...]


def _dest(row_start, e_idx, rank):
    t = e_idx.shape[1]
    tm = min(t, 2048)
    blk = pl.BlockSpec((TOP_K, tm), lambda i, *_: (0, i))
    return pl.pallas_call(
        _dest_kernel,
        grid_spec=pltpu.PrefetchScalarGridSpec(num_scalar_prefetch=1, grid=(t // tm,), in_specs=[blk, blk],
                                               out_specs=blk),
        out_shape=jax.ShapeDtypeStruct((TOP_K, t), jnp.int32),
        compiler_params=_cparams(("arbitrary",)),
        name="dest",
    )(row_start, e_idx, rank)


def _dispatch_kernel(n_blocks, ps_ref, pl_ref, nu_ref, dest_ref, h2_ref, xs_ref, zbuf, sem, zsem):
    i = pl.program_id(0)
    n_steps = pl.num_programs(0)
    tm = dest_ref.shape[1]
    n_exp = ps_ref.shape[0]
    per_step = -(-n_exp // n_steps)
    zbuf[...] = jnp.zeros(zbuf.shape, F32)

    def zero_fill(act):
        for q in range(per_step):
            e = jnp.minimum(i * per_step + q, n_exp - 1)
            live = i * per_step + q < n_exp
            gap = pl_ref[e]
            for bit in [1 << s for s in range(BM.bit_length() - 1)]:
                @pl.when(live & ((gap & bit) != 0))
                def _(bit=bit, e=e, gap=gap):
                    off = ps_ref[e] + (gap & ~(2 * bit - 1))
                    act(pltpu.make_async_copy(
                        zbuf.at[pl.ds(0, bit * ROW_TILE), :],
                        xs_ref.at[pl.ds(pl.multiple_of(off * ROW_TILE, ROW_TILE), bit * ROW_TILE), :], zsem))
            blk = nu_ref[0] + i * per_step + q
            @pl.when(blk < n_blocks)
            def _(blk=blk):
                rows = BM * ROW_TILE
                act(pltpu.make_async_copy(
                    zbuf, xs_ref.at[pl.ds(pl.multiple_of(blk * rows, rows), rows), :], zsem))

    zero_fill(lambda cp: cp.start(priority=1))

    def issue(j, c):
        for k in range(TOP_K):
            _row_copy(h2_ref, j, xs_ref, dest_ref[k, j], sem).start(priority=k % 2)
        return c

    lax.fori_loop(0, tm, issue, 0)
    n = tm * TOP_K * ROW_TILE
    pltpu.make_async_copy(xs_ref.at[pl.ds(0, n), :], xs_ref.at[pl.ds(0, n), :], sem).wait()
    zero_fill(lambda cp: cp.wait())


def _dispatch(pad_start, pad_len, n_used, dest, h2t, n_blocks):
    t = dest.shape[1]
    tm = TM_DISPATCH
    grid_spec = pltpu.PrefetchScalarGridSpec(
        num_scalar_prefetch=3,
        grid=(t // tm,),
        in_specs=[pl.BlockSpec((TOP_K, tm), lambda i, *_: (0, i), memory_space=pltpu.SMEM),
                  pl.BlockSpec((tm * ROW_TILE, LANES), lambda i, *_: (i, 0))],
        out_specs=pl.BlockSpec(memory_space=pl.ANY),
        scratch_shapes=[pltpu.VMEM((BM * ROW_TILE, LANES), F32), pltpu.SemaphoreType.DMA,
                        pltpu.SemaphoreType.DMA],
    )
    return pl.pallas_call(
        functools.partial(_dispatch_kernel, n_blocks),
        grid_spec=grid_spec,
        out_shape=jax.ShapeDtypeStruct((n_blocks * BM * ROW_TILE, LANES), F32),
        compiler_params=_cparams(("arbitrary",)),
        name="dispatch",
    )(pad_start, pad_len, n_used, dest, h2t)


def _expert_kernel(nblk_ref, bstart_ref, widx_ref, nu_ref, xs_ref, wg_ref, wu_ref, wd_ref, ys_ref,
                   xbuf, ybuf, wg_b, wu_b, wd_b, xsem, ysem):
    e = pl.program_id(0)
    n_used = nu_ref[0]
    rows = BM * ROW_TILE
    ring = EXPERT_RING

    def slot_base(g):
        return pl.multiple_of((g % ring) * rows, rows)

    def hbm_rows(ref, g):
        return ref.at[pl.ds(pl.multiple_of(g * rows, rows), rows), :]

    def x_copy(g):
        return pltpu.make_async_copy(hbm_rows(xs_ref, g), xbuf.at[pl.ds(slot_base(g), rows), :], xsem.at[g % ring])

    def y_copy(g):
        return pltpu.make_async_copy(ybuf.at[pl.ds(slot_base(g), rows), :], hbm_rows(ys_ref, g), ysem.at[g % ring])

    @pl.when(e == 0)
    def _():
        for g0 in range(ring - 1):
            @pl.when(g0 < n_used)
            def _(g0=g0):
                x_copy(g0).start()

    wg_b[...] = wg_ref[0].astype(BF16)
    wu_b[...] = wu_ref[0].astype(BF16)
    wd_b[...] = wd_ref[0].astype(BF16)

    def block(b, c):
        g = bstart_ref[e] + b
        x_copy(g).wait()

        @pl.when(g + ring - 1 < n_used)
        def _():
            x_copy(g + ring - 1).start()

        @pl.when(g >= ring)
        def _():
            y_copy(g - ring).wait()

        x = _load_token_rows(xbuf, slot_base(g), BM).astype(BF16)
        a = jnp.dot(x, wg_b[...], preferred_element_type=F32)
        u = jnp.dot(x, wu_b[...], preferred_element_type=F32)
        act = (a * _sigmoid(a) * u).astype(BF16)
        _store_token_rows(ybuf, slot_base(g), jnp.dot(act, wd_b[...], preferred_element_type=F32))
        y_copy(g).start(priority=1)
        return c

    lax.fori_loop(0, nblk_ref[e], block, 0)

    @pl.when(e == pl.num_programs(0) - 1)
    def _():
        for back in range(ring, 0, -1):
            @pl.when(n_used >= back)
            def _(back=back):
                y_copy(n_used - back).wait()


def _experts(nblk, blk_start, w_idx, n_used, xs, wg, wu, wd):
    n_exp, d, de = wg.shape
    rows = BM * ROW_TILE
    wsel = lambda e, nb, bs, wi, nu: (wi[e], 0, 0)
    grid_spec = pltpu.PrefetchScalarGridSpec(
        num_scalar_prefetch=4,
        grid=(n_exp,),
        in_specs=[pl.BlockSpec(memory_space=pl.ANY),
                  pl.BlockSpec((1, d, de), wsel), pl.BlockSpec((1, d, de), wsel),
                  pl.BlockSpec((1, de, d), wsel)],
        out_specs=pl.BlockSpec(memory_space=pl.ANY),
        scratch_shapes=[pltpu.VMEM((EXPERT_RING * rows, LANES), F32), pltpu.VMEM((EXPERT_RING * rows, LANES), F32),
                        pltpu.VMEM((d, de), BF16), pltpu.VMEM((d, de), BF16), pltpu.VMEM((de, d), BF16),
                        pltpu.SemaphoreType.DMA((EXPERT_RING,)), pltpu.SemaphoreType.DMA((EXPERT_RING,))],
    )
    return pl.pallas_call(
        _expert_kernel,
        grid_spec=grid_spec,
        out_shape=jax.ShapeDtypeStruct(xs.shape, xs.dtype),
        input_output_aliases={4: 0},
        compiler_params=_cparams(("arbitrary",)),
        name="experts",
    )(nblk, blk_start, w_idx, n_used, xs, wg, wu, wd)


def _combine_kernel(dest_ref, next_dest_ref, w_ref, base_ref, mod_ref, g_ref, b_ref, ys_ref, o_ref, buf, sems):
    i = pl.program_id(0)
    tm = dest_ref.shape[1]
    slab = tm * ROW_TILE
    slot_rows = TOP_K * slab

    def gather(idx_ref, slot):
        def issue(j, c):
            for k in range(TOP_K):
                _row_copy(ys_ref, idx_ref[k, j], buf, slot * (TOP_K * tm) + k * tm + j,
                          sems.at[slot]).start(priority=k % 2)
            return c
        lax.fori_loop(0, tm, issue, 0)

    @pl.when(i == 0)
    def _():
        gather(dest_ref, 0)

    @pl.when(i + 1 < pl.num_programs(0))
    def _():
        gather(next_dest_ref, (i + 1) % 2)

    slot = i % 2
    base_row = pl.multiple_of(slot * slot_rows, slot_rows)
    pltpu.make_async_copy(ys_ref.at[pl.ds(0, slot_rows), :], buf.at[pl.ds(base_row, slot_rows), :],
                          sems.at[slot]).wait()

    wpad = jnp.concatenate([w_ref[...], jnp.zeros((LANES - TOP_K, tm), F32)], axis=0)
    wcol = wpad.T
    routed = jnp.zeros(base_ref.shape, F32)
    for k in range(TOP_K):
        routed = routed + wcol[:, k:k + 1] * _load_token_rows(buf, base_row + k * slab, tm)
    gate_f = mod_ref[0, 5:6, :]
    o_ref[...] = _layer_norm(base_ref[...] + gate_f * routed, g_ref[...], b_ref[...])


def _combine(dest, wts, base, mod, g, b, ys, seq):
    t, d = base.shape
    tm = TM_COMBINE
    per_seq = seq // tm
    n_steps = t // tm
    col = lambda i: (0, i)
    row = lambda i: (i, 0)
    return pl.pallas_call(
        _combine_kernel,
        grid=(n_steps,),
        in_specs=[pl.BlockSpec((TOP_K, tm), col, memory_space=pltpu.SMEM),
                  pl.BlockSpec((TOP_K, tm), lambda i: (0, jnp.minimum(i + 1, n_steps - 1)),
                               memory_space=pltpu.SMEM),
                  pl.BlockSpec((TOP_K, tm), col), pl.BlockSpec((tm, d), row),
                  pl.BlockSpec((1,) + mod.shape[1:], lambda i: (i // per_seq, 0, 0)),
                  _const_spec(g.shape), _const_spec(b.shape),
                  pl.BlockSpec(memory_space=pl.ANY)],
        out_specs=pl.BlockSpec((tm, d), row),
        out_shape=jax.ShapeDtypeStruct((t, d), F32),
        scratch_shapes=[pltpu.VMEM((2 * TOP_K * tm * ROW_TILE, LANES), F32), pltpu.SemaphoreType.DMA((2,))],
        compiler_params=_cparams(("arbitrary",)),
        name="combine",
    )(dest, dest, wts, base, mod, g, b, ys)


def _rope_tables(seq):
    half = HEAD_DIM // 2
    inv_freq = ROPE_THETA ** (-jnp.arange(half, dtype=F32) / half)
    ang = jnp.arange(seq, dtype=F32)[:, None] * inv_freq[None, :]
    cos = jnp.concatenate([jnp.cos(ang), jnp.cos(ang)], axis=1)
    sin = jnp.concatenate([-jnp.sin(ang), jnp.sin(ang)], axis=1)
    return jnp.tile(cos, (1, N_HEADS)), jnp.tile(sin, (1, N_HEADS))


def _layer(x, c, w_ada, b_ada, w_in, ln_v_g, ln_v_b, w_spatial, b_spatial, w_proj_a, w_proj_b, w_out,
           ln1_g, ln1_b, w_router, router_bias, w_gate_e, w_up_e, w_down_e, w_gate_sh, w_up_sh,
           w_down_sh, ln2_g, ln2_b, alpha):
    bsz, seq, d = x.shape
    assert seq % (max(DILATIONS) * BAND) == 0 and seq % TM_IN == 0 and d == ROW_TILE * LANES
    t = bsz * seq
    aw = N_HEADS * HEAD_DIM
    gwid = ln_v_g.shape[0]
    n_exp = w_router.shape[1]
    xf = x.reshape(t, d)

    mod = _ada(c, w_ada, b_ada).reshape(bsz, 6, d)

    cuts = [0, aw, 2 * aw, 3 * aw, 3 * aw + gwid, 3 * aw + 2 * gwid, 3 * aw + 2 * gwid + d,
            3 * aw + 2 * gwid + 2 * d]
    wq, wk, wv, wu, wvg, wga, wgb = [w_in[:, a:b].astype(BF16) for a, b in zip(cuts[:-1], cuts[1:])]
    cos, sin = _rope_tables(seq)
    bsp = jnp.repeat(b_spatial.T, gwid // GMLP_GROUPS, axis=1)
    qkv, ga, gbyb = _inproj(xf, mod, cos, sin, wq, wk, wv, wu, wvg, wga, wgb,
                            ln_v_g.reshape(1, gwid), ln_v_b.reshape(1, gwid), w_spatial, bsp,
                            w_proj_b.astype(BF16), seq)

    oa = _attention(qkv, bsz, seq)

    h2t, base, e_idx, wts, rank, cnt = _post(
        alpha, oa.reshape(t, aw), ga, gbyb, xf, mod, w_proj_a.astype(BF16), w_out.astype(BF16),
        ln1_g.reshape(1, d), ln1_b.reshape(1, d), w_router.T.astype(BF16), router_bias.reshape(n_exp, 1),
        w_gate_sh.astype(BF16), w_up_sh.astype(BF16), w_down_sh.astype(BF16), seq)

    counts = cnt[:, 0].astype(jnp.int32)
    nblk = ((counts + BM - 1) // BM).astype(jnp.int32)
    blk_end = jnp.cumsum(nblk).astype(jnp.int32)
    blk_start = blk_end - nblk
    n_blocks = (t * TOP_K) // BM + n_exp
    n_used = blk_end[-1:]
    row_start = blk_start * BM
    ids = jnp.arange(n_exp, dtype=jnp.int32)
    seen = jnp.where((ids[None, :] <= ids[:, None]) & (nblk[None, :] > 0), ids[None, :], -1).max(axis=1)
    w_idx = jnp.where(seen >= 0, seen, jnp.where(nblk > 0, ids, n_exp - 1).min()).astype(jnp.int32)

    dest = _dest(row_start, e_idx, rank)
    xs = _dispatch(row_start + counts, nblk * BM - counts, n_used, dest, h2t, n_blocks)
    ys = _experts(nblk, blk_start, w_idx, n_used, xs, w_gate_e, w_up_e, w_down_e)
    out = _combine(dest, wts, base, mod, ln2_g.reshape(1, d), ln2_b.reshape(1, d), ys, seq)
    return out.reshape(bsz, seq, d)


def kernel(x, c, w_ada, b_ada, w_in, ln_v_g, ln_v_b, w_spatial, b_spatial, w_proj_a, w_proj_b, w_out,
           ln1_g, ln1_b, w_router, router_bias, w_gate_e, w_up_e, w_down_e, w_gate_sh, w_up_sh,
           w_down_sh, ln2_g, ln2_b):
    depth = w_ada.shape[0]
    alpha = (2.0 * depth) ** 0.25
    for l in range(depth):
        x = _layer(x, c, w_ada[l], b_ada[l], w_in[l], ln_v_g[l], ln_v_b[l], w_spatial[l], b_spatial[l],
                   w_proj_a[l], w_proj_b[l], w_out[l], ln1_g[l], ln1_b[l], w_router[l], router_bias[l],
                   w_gate_e[l], w_up_e[l], w_down_e[l], w_gate_sh[l], w_up_sh[l], w_down_sh[l],
                   ln2_g[l], ln2_b[l], alpha)
    return x
```

```python
import functools
import math

import jax
import jax.numpy as jnp
from jax import lax
from jax.experimental import pallas as pl
from jax.experimental.pallas import tpu as pltpu

F32 = jnp.float32
BF16 = jnp.bfloat16

N_HEADS = 8
HEAD_DIM = 64
DILATIONS = (1, 4, 16)
BAND = 128
ROPE_THETA = 10000.0
GMLP_GROUPS = 8
GMLP_CHUNK = 128
N_EXPERT_GROUPS = 8
TOPK_GROUPS = 4
TOP_K = 8
ROUTED_SCALE = 2.5
LN_EPS = 1e-5
LANES = 128
SUBLANES = 8
ROW_TILE = 8
VMEM_LIMIT = 56 * 1024 * 1024

TM_IN = 512
TM_POST = 256
TM_DISPATCH = 512
TM_COMBINE = 128
BM = 256
EXPERT_RING = 4
ATTN_BLOCKS_PER_TRIP = 16


def _cparams(sem):
    return pltpu.CompilerParams(dimension_semantics=sem, vmem_limit_bytes=VMEM_LIMIT)


def _const_spec(shape):
    nd = len(shape)
    return pl.BlockSpec(shape, lambda *_: (0,) * nd)


def _layer_norm(r, g, b):
    mu = jnp.mean(r, axis=-1, keepdims=True)
    d = r - mu
    var = jnp.mean(d * d, axis=-1, keepdims=True)
    return d * lax.rsqrt(var + LN_EPS) * g + b


def _gelu(t):
    return 0.5 * t * (1.0 + lax.erf(t * (1.0 / math.sqrt(2.0))))


def _sigmoid(t):
    return 1.0 / (1.0 + jnp.exp(-t))


def _store_token_rows(ref, first_row, v):
    m = v.shape[0]
    for s in range(ROW_TILE):
        ref[pl.ds(first_row + s, m, stride=ROW_TILE), :] = v[:, s * LANES:(s + 1) * LANES]


def _load_token_rows(ref, first_row, m):
    return jnp.concatenate([ref[pl.ds(first_row + s, m, stride=ROW_TILE), :] for s in range(ROW_TILE)], axis=1)


def _ada_kernel(c_ref, w_ref, b_ref, o_ref):
    c = c_ref[...]
    o_ref[...] = jnp.dot(c * _sigmoid(c), w_ref[...], preferred_element_type=F32) + b_ref[...]


def _ada(c, w, b):
    bsz, d = c.shape
    n = w.shape[1]
    tn = 1024
    return pl.pallas_call(
        _ada_kernel,
        grid=(n // tn,),
        in_specs=[_const_spec((bsz, d)), pl.BlockSpec((d, tn), lambda j: (0, j)),
                  pl.BlockSpec((1, tn), lambda j: (0, j))],
        out_specs=pl.BlockSpec((bsz, tn), lambda j: (0, j)),
        out_shape=jax.ShapeDtypeStruct((bsz, n), F32),
        compiler_params=_cparams(("arbitrary",)),
        name="ada",
    )(c, w, b.reshape(1, n))


def _inproj_kernel(x_ref, mod_ref, cos_ref, sin_ref, wq_ref, wk_ref, wv_ref, wu_ref, wvg_ref,
                   wga_ref, wgb_ref, lng_ref, lnb_ref, wsp_ref, bsp_ref, wpb_ref, *rest):
    n_cm = len(DILATIONS) - 1
    q_refs, k_refs, v_refs = [rest[(1 + n_cm) * a:(1 + n_cm) * (a + 1)] for a in range(3)]
    ga_ref, gbyb_ref, u_scr, vv_scr, gated_scr, q_stage, k_stage, v_stage = rest[3 * (1 + n_cm):]
    tm = x_ref.shape[0]

    def emit(val, refs, stage):
        refs[0][...] = val.astype(BF16)
        for hp in range(val.shape[1] // LANES):
            stage[pl.ds(hp * tm, tm), :] = val[:, hp * LANES:(hp + 1) * LANES]
        for ref, dil in zip(refs[1:], DILATIONS[1:]):
            for hp in range(val.shape[1] // LANES):
                for r in range(dil):
                    ref[0, hp, r, :, :] = stage[pl.ds(hp * tm + r, tm // dil, stride=dil), :].astype(BF16)
    shift = mod_ref[0, 0:1, :]
    scale = mod_ref[0, 1:2, :]
    h = (x_ref[...] * (1.0 + scale) + shift).astype(BF16)

    cos = cos_ref[...]
    sin = sin_ref[...]
    aw = cos.shape[1]
    lane = lax.broadcasted_iota(jnp.int32, (tm, aw), 1)
    first_half = (lane % HEAD_DIM) < (HEAD_DIM // 2)

    def rope(t):
        partner = jnp.where(first_half, pltpu.roll(t, aw - HEAD_DIM // 2, 1),
                            pltpu.roll(t, HEAD_DIM // 2, 1))
        return t * cos + partner * sin

    emit(rope(jnp.dot(h, wq_ref[...], preferred_element_type=F32)) * (HEAD_DIM ** -0.5), q_refs, q_stage)
    emit(rope(jnp.dot(h, wk_ref[...], preferred_element_type=F32)), k_refs, k_stage)
    emit(jnp.dot(h, wv_ref[...], preferred_element_type=F32), v_refs, v_stage)

    u_scr[...] = _gelu(jnp.dot(h, wu_ref[...], preferred_element_type=F32))
    vg = _gelu(jnp.dot(h, wvg_ref[...], preferred_element_type=F32))
    vv_scr[...] = _layer_norm(vg, lng_ref[...], lnb_ref[...]).astype(BF16)

    ti = lax.broadcasted_iota(jnp.int32, (GMLP_CHUNK, GMLP_CHUNK), 0)
    si = lax.broadcasted_iota(jnp.int32, (GMLP_CHUNK, GMLP_CHUNK), 1)
    causal = si <= ti
    gw = u_scr.shape[1] // GMLP_GROUPS
    for g in range(GMLP_GROUPS):
        ws = jnp.where(causal, wsp_ref[g], 0.0).astype(BF16)
        cols = slice(g * gw, (g + 1) * gw)
        for c in range(tm // GMLP_CHUNK):
            rows = slice(c * GMLP_CHUNK, (c + 1) * GMLP_CHUNK)
            sv = jnp.dot(ws, vv_scr[rows, cols], preferred_element_type=F32) + bsp_ref[:, cols]
            gated_scr[rows, cols] = (u_scr[rows, cols] * sv).astype(BF16)

    y_b = jnp.dot(gated_scr[...], wpb_ref[...], preferred_element_type=F32)
    g_b = _sigmoid(jnp.dot(h, wgb_ref[...], preferred_element_type=F32))
    gbyb_ref[...] = (g_b * y_b).astype(BF16)
    ga_ref[...] = _sigmoid(jnp.dot(h, wga_ref[...], preferred_element_type=F32)).astype(BF16)


def _inproj(xf, mod, cos, sin, wq, wk, wv, wu, wvg, wga, wgb, lng, lnb, wsp, bsp, wpb, seq):
    t, d = xf.shape
    aw = wq.shape[1]
    gwid = wu.shape[1]
    tm = TM_IN
    per_seq = seq // tm
    row = lambda i: (i, 0)
    tab = lambda i: (i % per_seq, 0)
    bsz = t // seq
    qkv_specs = [pl.BlockSpec((tm, aw), row)] + [
        pl.BlockSpec((1, aw // LANES, dil, tm // dil, LANES), lambda i: (i // per_seq, 0, 0, i % per_seq, 0))
        for dil in DILATIONS[1:]]
    qkv_shapes = [jax.ShapeDtypeStruct((t, aw), BF16)] + [
        jax.ShapeDtypeStruct((bsz, aw // LANES, dil, seq // dil, LANES), BF16) for dil in DILATIONS[1:]]
    outs = pl.pallas_call(
        _inproj_kernel,
        grid=(t // tm,),
        in_specs=[pl.BlockSpec((tm, d), row),
                  pl.BlockSpec((1,) + mod.shape[1:], lambda i: (i // per_seq, 0, 0)),
                  pl.BlockSpec((tm, aw), tab), pl.BlockSpec((tm, aw), tab),
                  _const_spec(wq.shape), _const_spec(wk.shape), _const_spec(wv.shape),
                  _const_spec(wu.shape), _const_spec(wvg.shape), _const_spec(wga.shape),
                  _const_spec(wgb.shape), _const_spec(lng.shape), _const_spec(lnb.shape),
                  _const_spec(wsp.shape), _const_spec(bsp.shape), _const_spec(wpb.shape)],
        out_specs=qkv_specs * 3 + [pl.BlockSpec((tm, d), row)] * 2,
        out_shape=qkv_shapes * 3 + [jax.ShapeDtypeStruct((t, d), BF16)] * 2,
        scratch_shapes=[pltpu.VMEM((tm, gwid), F32), pltpu.VMEM((tm, gwid), BF16),
                        pltpu.VMEM((tm, gwid), BF16)] + [pltpu.VMEM((tm * aw // LANES, LANES), F32)] * 3,
        compiler_params=_cparams(("arbitrary",)),
        name="inproj",
    )(xf, mod, cos, sin, wq, wk, wv, wu, wvg, wga, wgb, lng, lnb, wsp, bsp, wpb)
    n_lay = len(DILATIONS)
    return [outs[a * n_lay:(a + 1) * n_lay] for a in range(3)], outs[3 * n_lay], outs[3 * n_lay + 1]


def _rows(start, size, stride):
    if stride == 1:
        return pl.ds(start, size)
    return pl.ds(start, size, stride=stride)


def _band_aligned(start):
    return start if isinstance(start, int) else pl.multiple_of(start, BAND)


def _interleaved_loop(count, load, compute, store):
    per_trip = max(u for u in range(1, ATTN_BLOCKS_PER_TRIP + 1) if count % u == 0) if count else 1

    def trip(it, c):
        items = [it * per_trip + u for u in range(per_trip)]
        results = [compute(ops) for ops in [load(i) for i in items]]
        for i, res in zip(items, results):
            store(i, res)
        return c

    if count == per_trip:
        trip(0, 0)
    elif count:
        lax.fori_loop(0, count // per_trip, trip, 0)


def _attn_kernel(*refs):
    npat = len(DILATIONS)
    qkv = [refs[3 * p:3 * p + 3] for p in range(npat)]
    o_ref, out_scr, lse_scr = refs[3 * npat:]
    seq = o_ref.shape[1]

    lane = lax.broadcasted_iota(jnp.int32, (BAND, LANES), 1)
    head0 = lane < HEAD_DIM
    neg = jnp.float32(-1e30)

    qi = lax.broadcasted_iota(jnp.int32, (BAND, 2 * BAND), 0)
    kj = lax.broadcasted_iota(jnp.int32, (BAND, 2 * BAND), 1)

    def load(p, blk, nb):
        q_ref, k_ref, v_ref = qkv[p]
        q_start = blk * BAND
        k_start = jnp.maximum(q_start - BAND, 0)
        off = q_start - k_start
        lowest = jnp.where(blk % nb == 0, off, qi)
        mask = (kj >= lowest) & (kj <= qi + off)
        krows = pl.ds(_band_aligned(k_start), 2 * BAND)
        return (q_ref[0, 0, pl.ds(_band_aligned(q_start), BAND), :], k_ref[0, 0, krows, :],
                v_ref[0, 0, krows, :], mask)

    def attend(ops):
        qb, kb, vb, mask = ops
        zero = jnp.zeros(qb.shape, qb.dtype)
        parts = []
        for hd in range(2):
            qh = jnp.where(head0 if hd == 0 else ~head0, qb, zero)
            s = lax.dot_general(qh, kb, (((1,), (1,)), ((), ())), preferred_element_type=F32)
            s = jnp.where(mask, s, neg)
            m = jnp.max(s, axis=-1, keepdims=True)
            pr = jnp.exp(s - m)
            l = jnp.sum(pr, axis=-1, keepdims=True)
            acc = jnp.dot(pr.astype(BF16), vb, preferred_element_type=F32)
            parts.append((acc / l, m + jnp.log(l)))
        return [jnp.where(head0, a0, a1) for a0, a1 in zip(*parts)]

    def store(p, seq_start, stride, res):
        rows = _rows(seq_start if stride > 1 else _band_aligned(seq_start), BAND, stride)
        out_scr[p, rows, :] = res[0]
        lse_scr[p, rows, :] = res[1]

    for p, dil in enumerate(DILATIONS):
        n = seq // dil
        nb = n // BAND

        _interleaved_loop(
            seq // BAND, lambda blk, p=p, nb=nb: load(p, blk, nb), attend,
            lambda blk, res, p=p, dil=dil, nb=nb: store(p, blk // nb + dil * BAND * (blk % nb), dil, res))

    top = jnp.maximum(jnp.maximum(lse_scr[0], lse_scr[1]), lse_scr[2])
    num = jnp.zeros(top.shape, F32)
    den = jnp.zeros(top.shape, F32)
    for p in range(npat):
        w = jnp.exp(lse_scr[p] - top)
        num = num + w * out_scr[p]
        den = den + w
    o_ref[0] = (num / den).astype(BF16)


def _attention(qkv, bsz, seq):
    aw = qkv[0][0].shape[1]
    npat = len(DILATIONS)
    blk = (1, 1, seq, LANES)
    operands, specs = [], []
    for p, dil in enumerate(DILATIONS):
        for layouts in qkv:
            if dil == 1:
                operands.append(layouts[p].reshape(bsz, 1, seq, aw))
                specs.append(pl.BlockSpec(blk, lambda b, hp: (b, 0, 0, hp)))
            else:
                operands.append(layouts[p].reshape(bsz, aw // LANES, seq, LANES))
                specs.append(pl.BlockSpec(blk, lambda b, hp: (b, hp, 0, 0)))
    return pl.pallas_call(
        _attn_kernel,
        grid=(bsz, aw // LANES),
        in_specs=specs,
        out_specs=pl.BlockSpec((1, seq, LANES), lambda b, hp: (b, 0, hp)),
        out_shape=jax.ShapeDtypeStruct((bsz, seq, aw), BF16),
        scratch_shapes=[pltpu.VMEM((npat, seq, LANES), F32)] * 2,
        compiler_params=_cparams(("arbitrary", "arbitrary")),
        name="attn",
    )(*operands)


def _post_kernel(alpha, oa_ref, ga_ref, gbyb_ref, x_ref, mod_ref, wpa_ref, wo_ref, l1g_ref, l1b_ref,
                 wrt_ref, rb_ref, wgs_ref, wus_ref, wds_ref, tri_ref,
                 h2_ref, base_ref, e_ref, w_ref, r_ref, cnt_ref):
    tm = x_ref.shape[0]
    n_exp = wrt_ref.shape[0]
    per_group = n_exp // N_EXPERT_GROUPS
    gate_a = mod_ref[0, 2:3, :]
    shift_f = mod_ref[0, 3:4, :]
    scale_f = mod_ref[0, 4:5, :]
    gate_f = mod_ref[0, 5:6, :]

    y_a = jnp.dot(oa_ref[...], wpa_ref[...], preferred_element_type=F32)
    merged = ga_ref[...].astype(F32) * y_a + gbyb_ref[...].astype(F32)
    mix = jnp.dot(merged.astype(BF16), wo_ref[...], preferred_element_type=F32)
    x1 = _layer_norm(alpha * x_ref[...] + gate_a * mix, l1g_ref[...], l1b_ref[...])
    h2 = x1 * (1.0 + scale_f) + shift_f
    h2b = h2.astype(BF16)

    _store_token_rows(h2_ref, 0, h2)

    a = jnp.dot(h2b, wgs_ref[...], preferred_element_type=F32)
    b = jnp.dot(h2b, wus_ref[...], preferred_element_type=F32)
    shared = jnp.dot((a * _sigmoid(a) * b).astype(BF16), wds_ref[...], preferred_element_type=F32)
    base_ref[...] = alpha * x1 + gate_f * shared

    logits = lax.dot_general(wrt_ref[...], h2b, (((1,), (1,)), ((), ())), preferred_element_type=F32)
    scores = _sigmoid(logits)
    biased = scores + rb_ref[...]
    ninf = jnp.float32(-jnp.inf)

    sub = lax.broadcasted_iota(jnp.int32, (per_group, tm), 0)
    grp_rows = []
    for g in range(N_EXPERT_GROUPS):
        blk = biased[g * per_group:(g + 1) * per_group, :]
        m1 = jnp.max(blk, axis=0, keepdims=True)
        f1 = jnp.min(jnp.where(blk == m1, sub, per_group), axis=0, keepdims=True)
        m2 = jnp.max(jnp.where(sub == f1, ninf, blk), axis=0, keepdims=True)
        grp_rows.append(m1 + m2)
    grp = jnp.concatenate(grp_rows, axis=0)

    gi = lax.broadcasted_iota(jnp.int32, grp.shape, 0)
    chosen = jnp.zeros(grp.shape, jnp.bool_)
    cur = grp
    for _ in range(TOPK_GROUPS):
        m = jnp.max(cur, axis=0, keepdims=True)
        f = jnp.min(jnp.where(cur == m, gi, N_EXPERT_GROUPS), axis=0, keepdims=True)
        hit = gi == f
        chosen = chosen | hit
        cur = jnp.where(hit, ninf, cur)
    emask = jnp.concatenate(
        [jnp.broadcast_to(chosen[g:g + 1, :], (per_group, tm)) for g in range(N_EXPERT_GROUPS)], axis=0)

    ei = lax.broadcasted_iota(jnp.int32, (n_exp, tm), 0)
    cand = jnp.where(emask, biased, ninf)
    hits, ids, wts = [], [], []
    for _ in range(TOP_K):
        m = jnp.max(cand, axis=0, keepdims=True)
        f = jnp.min(jnp.where(cand == m, ei, n_exp), axis=0, keepdims=True)
        hit = ei == f
        wts.append(jnp.sum(jnp.where(hit, scores, 0.0), axis=0, keepdims=True))
        cand = jnp.where(hit, ninf, cand)
        hits.append(hit)
        ids.append(f)
    onehot = hits[0]
    for hit in hits[1:]:
        onehot = onehot | hit
    onehot = jnp.where(onehot, 1.0, 0.0)

    @pl.when(pl.program_id(0) == 0)
    def _():
        cnt_ref[...] = jnp.zeros(cnt_ref.shape, F32)

    before = jnp.dot(onehot.astype(BF16), tri_ref[...], preferred_element_type=F32) + cnt_ref[...]
    ranks = [jnp.sum(jnp.where(hit, before, 0.0), axis=0, keepdims=True) for hit in hits]
    cnt_ref[...] += jnp.sum(onehot, axis=1, keepdims=True)

    denom = wts[0]
    for wk in wts[1:]:
        denom = denom + wk
    e_ref[...] = jnp.concatenate(ids, axis=0)
    w_ref[...] = jnp.concatenate([ROUTED_SCALE * wk / denom for wk in wts], axis=0)
    r_ref[...] = jnp.concatenate(ranks, axis=0).astype(jnp.int32)


def _post(alpha, oa, ga, gbyb, xf, mod, wpa, wo, l1g, l1b, wrt, rb, wgs, wus, wds, seq):
    t, d = xf.shape
    aw = oa.shape[1]
    n_exp = wrt.shape[0]
    tm = TM_POST
    per_seq = seq // tm
    row = lambda i: (i, 0)
    col = lambda i: (0, i)
    tri = (jnp.arange(tm)[:, None] < jnp.arange(tm)[None, :]).astype(BF16)
    return pl.pallas_call(
        functools.partial(_post_kernel, alpha),
        grid=(t // tm,),
        in_specs=[pl.BlockSpec((tm, aw), row), pl.BlockSpec((tm, d), row), pl.BlockSpec((tm, d), row),
                  pl.BlockSpec((tm, d), row),
                  pl.BlockSpec((1,) + mod.shape[1:], lambda i: (i // per_seq, 0, 0)),
                  _const_spec(wpa.shape), _const_spec(wo.shape), _const_spec(l1g.shape),
                  _const_spec(l1b.shape), _const_spec(wrt.shape), _const_spec(rb.shape),
                  _const_spec(wgs.shape), _const_spec(wus.shape), _const_spec(wds.shape),
                  _const_spec(tri.shape)],
        out_specs=[pl.BlockSpec((tm * ROW_TILE, LANES), row), pl.BlockSpec((tm, d), row),
                   pl.BlockSpec((TOP_K, tm), col), pl.BlockSpec((TOP_K, tm), col),
                   pl.BlockSpec((TOP_K, tm), col), _const_spec((n_exp, 1))],
        out_shape=[jax.ShapeDtypeStruct((t * ROW_TILE, LANES), F32), jax.ShapeDtypeStruct((t, d), F32),
                   jax.ShapeDtypeStruct((TOP_K, t), jnp.int32), jax.ShapeDtypeStruct((TOP_K, t), F32),
                   jax.ShapeDtypeStruct((TOP_K, t), jnp.int32), jax.ShapeDtypeStruct((n_exp, 1), F32)],
        compiler_params=_cparams(("arbitrary",)),
        name="post",
    )(oa, ga, gbyb, xf, mod, wpa, wo, l1g, l1b, wrt, rb, wgs, wus, wds, tri)


def _row_copy(src_ref, src_row, dst_ref, dst_row, sem):
    return pltpu.make_async_copy(
        src_ref.at[pl.ds(pl.multiple_of(src_row * ROW_TILE, ROW_TILE), ROW_TILE), :],
        dst_ref.at[pl.ds(pl.multiple_of(dst_row * ROW_TILE, ROW_TILE), ROW_TILE), :], sem)


def _dest_kernel(rs_ref, e_ref, r_ref, o_ref):
    e = e_ref[...]
    start = lax.fori_loop(0, rs_ref.shape[0], lambda x, acc: jnp.where(e == x, rs_ref[x], acc),
                          jnp.zeros(e.shape, jnp.int32))
    o_ref[...] = start + r_ref[...]


def _dest(row_start, e_idx, rank):
    t = e_idx.shape[1]
    tm = min(t, 2048)
    blk = pl.BlockSpec((TOP_K, tm), lambda i, *_: (0, i))
    return pl.pallas_call(
        _dest_kernel,
        grid_spec=pltpu.PrefetchScalarGridSpec(num_scalar_prefetch=1, grid=(t // tm,), in_specs=[blk, blk],
                                               out_specs=blk),
        out_shape=jax.ShapeDtypeStruct((TOP_K, t), jnp.int32),
        compiler_params=_cparams(("arbitrary",)),
        name="dest",
    )(row_start, e_idx, rank)


def _dispatch_kernel(n_blocks, ps_ref, pl_ref, nu_ref, dest_ref, h2_ref, xs_ref, zbuf, sem, zsem):
    i = pl.program_id(0)
    n_steps = pl.num_programs(0)
    tm = dest_ref.shape[1]
    n_exp = ps_ref.shape[0]
    per_step = -(-n_exp // n_steps)
    zbuf[...] = jnp.zeros(zbuf.shape, F32)

    def zero_fill(act):
        for q in range(per_step):
            e = jnp.minimum(i * per_step + q, n_exp - 1)
            live = i * per_step + q < n_exp
            gap = pl_ref[e]
            for bit in [1 << s for s in range(BM.bit_length() - 1)]:
                @pl.when(live & ((gap & bit) != 0))
                def _(bit=bit, e=e, gap=gap):
                    off = ps_ref[e] + (gap & ~(2 * bit - 1))
                    act(pltpu.make_async_copy(
                        zbuf.at[pl.ds(0, bit * ROW_TILE), :],
                        xs_ref.at[pl.ds(pl.multiple_of(off * ROW_TILE, ROW_TILE), bit * ROW_TILE), :], zsem))
            blk = nu_ref[0] + i * per_step + q
            @pl.when(blk < n_blocks)
            def _(blk=blk):
                rows = BM * ROW_TILE
                act(pltpu.make_async_copy(
                    zbuf, xs_ref.at[pl.ds(pl.multiple_of(blk * rows, rows), rows), :], zsem))

    zero_fill(lambda cp: cp.start(priority=1))

    def issue(j, c):
        for k in range(TOP_K):
            _row_copy(h2_ref, j, xs_ref, dest_ref[k, j], sem).start(priority=k % 2)
        return c

    lax.fori_loop(0, tm, issue, 0)
    n = tm * TOP_K * ROW_TILE
    pltpu.make_async_copy(xs_ref.at[pl.ds(0, n), :], xs_ref.at[pl.ds(0, n), :], sem).wait()
    zero_fill(lambda cp: cp.wait())


def _dispatch(pad_start, pad_len, n_used, dest, h2t, n_blocks):
    t = dest.shape[1]
    tm = TM_DISPATCH
    grid_spec = pltpu.PrefetchScalarGridSpec(
        num_scalar_prefetch=3,
        grid=(t // tm,),
        in_specs=[pl.BlockSpec((TOP_K, tm), lambda i, *_: (0, i), memory_space=pltpu.SMEM),
                  pl.BlockSpec((tm * ROW_TILE, LANES), lambda i, *_: (i, 0))],
        out_specs=pl.BlockSpec(memory_space=pl.ANY),
        scratch_shapes=[pltpu.VMEM((BM * ROW_TILE, LANES), F32), pltpu.SemaphoreType.DMA,
                        pltpu.SemaphoreType.DMA],
    )
    return pl.pallas_call(
        functools.partial(_dispatch_kernel, n_blocks),
        grid_spec=grid_spec,
        out_shape=jax.ShapeDtypeStruct((n_blocks * BM * ROW_TILE, LANES), F32),
        compiler_params=_cparams(("arbitrary",)),
        name="dispatch",
    )(pad_start, pad_len, n_used, dest, h2t)


def _expert_kernel(nblk_ref, bstart_ref, widx_ref, nu_ref, xs_ref, wg_ref, wu_ref, wd_ref, ys_ref,
                   xbuf, ybuf, wg_b, wu_b, wd_b, xsem, ysem):
    e = pl.program_id(0)
    n_used = nu_ref[0]
    rows = BM * ROW_TILE
    ring = EXPERT_RING

    def slot_base(g):
        return pl.multiple_of((g % ring) * rows, rows)

    def hbm_rows(ref, g):
        return ref.at[pl.ds(pl.multiple_of(g * rows, rows), rows), :]

    def x_copy(g):
        return pltpu.make_async_copy(hbm_rows(xs_ref, g), xbuf.at[pl.ds(slot_base(g), rows), :], xsem.at[g % ring])

    def y_copy(g):
        return pltpu.make_async_copy(ybuf.at[pl.ds(slot_base(g), rows), :], hbm_rows(ys_ref, g), ysem.at[g % ring])

    @pl.when(e == 0)
    def _():
        for g0 in range(ring - 1):
            @pl.when(g0 < n_used)
            def _(g0=g0):
                x_copy(g0).start()

    wg_b[...] = wg_ref[0].astype(BF16)
    wu_b[...] = wu_ref[0].astype(BF16)
    wd_b[...] = wd_ref[0].astype(BF16)

    def block(b, c):
        g = bstart_ref[e] + b
        x_copy(g).wait()

        @pl.when(g + ring - 1 < n_used)
        def _():
            x_copy(g + ring - 1).start()

        @pl.when(g >= ring)
        def _():
            y_copy(g - ring).wait()

        x = _load_token_rows(xbuf, slot_base(g), BM).astype(BF16)
        a = jnp.dot(x, wg_b[...], preferred_element_type=F32)
        u = jnp.dot(x, wu_b[...], preferred_element_type=F32)
        act = (a * _sigmoid(a) * u).astype(BF16)
        _store_token_rows(ybuf, slot_base(g), jnp.dot(act, wd_b[...], preferred_element_type=F32))
        y_copy(g).start(priority=1)
        return c

    lax.fori_loop(0, nblk_ref[e], block, 0)

    @pl.when(e == pl.num_programs(0) - 1)
    def _():
        for back in range(ring, 0, -1):
            @pl.when(n_used >= back)
            def _(back=back):
                y_copy(n_used - back).wait()


def _experts(nblk, blk_start, w_idx, n_used, xs, wg, wu, wd):
    n_exp, d, de = wg.shape
    rows = BM * ROW_TILE
    wsel = lambda e, nb, bs, wi, nu: (wi[e], 0, 0)
    grid_spec = pltpu.PrefetchScalarGridSpec(
        num_scalar_prefetch=4,
        grid=(n_exp,),
        in_specs=[pl.BlockSpec(memory_space=pl.ANY),
                  pl.BlockSpec((1, d, de), wsel), pl.BlockSpec((1, d, de), wsel),
                  pl.BlockSpec((1, de, d), wsel)],
        out_specs=pl.BlockSpec(memory_space=pl.ANY),
        scratch_shapes=[pltpu.VMEM((EXPERT_RING * rows, LANES), F32), pltpu.VMEM((EXPERT_RING * rows, LANES), F32),
                        pltpu.VMEM((d, de), BF16), pltpu.VMEM((d, de), BF16), pltpu.VMEM((de, d), BF16),
                        pltpu.SemaphoreType.DMA((EXPERT_RING,)), pltpu.SemaphoreType.DMA((EXPERT_RING,))],
    )
    return pl.pallas_call(
        _expert_kernel,
        grid_spec=grid_spec,
        out_shape=jax.ShapeDtypeStruct(xs.shape, xs.dtype),
        input_output_aliases={4: 0},
        compiler_params=_cparams(("arbitrary",)),
        name="experts",
    )(nblk, blk_start, w_idx, n_used, xs, wg, wu, wd)


def _combine_kernel(dest_ref, next_dest_ref, w_ref, base_ref, mod_ref, g_ref, b_ref, ys_ref, o_ref, buf, sems):
    i = pl.program_id(0)
    tm = dest_ref.shape[1]
    slab = tm * ROW_TILE
    slot_rows = TOP_K * slab

    def gather(idx_ref, slot):
        def issue(j, c):
            for k in range(TOP_K):
                _row_copy(ys_ref, idx_ref[k, j], buf, slot * (TOP_K * tm) + k * tm + j,
                          sems.at[slot]).start(priority=k % 2)
            return c
        lax.fori_loop(0, tm, issue, 0)

    @pl.when(i == 0)
    def _():
        gather(dest_ref, 0)

    @pl.when(i + 1 < pl.num_programs(0))
    def _():
        gather(next_dest_ref, (i + 1) % 2)

    slot = i % 2
    base_row = pl.multiple_of(slot * slot_rows, slot_rows)
    pltpu.make_async_copy(ys_ref.at[pl.ds(0, slot_rows), :], buf.at[pl.ds(base_row, slot_rows), :],
                          sems.at[slot]).wait()

    wpad = jnp.concatenate([w_ref[...], jnp.zeros((LANES - TOP_K, tm), F32)], axis=0)
    wcol = wpad.T
    routed = jnp.zeros(base_ref.shape, F32)
    for k in range(TOP_K):
        routed = routed + wcol[:, k:k + 1] * _load_token_rows(buf, base_row + k * slab, tm)
    gate_f = mod_ref[0, 5:6, :]
    o_ref[...] = _layer_norm(base_ref[...] + gate_f * routed, g_ref[...], b_ref[...])


def _combine(dest, wts, base, mod, g, b, ys, seq):
    t, d = base.shape
    tm = TM_COMBINE
    per_seq = seq // tm
    n_steps = t // tm
    col = lambda i: (0, i)
    row = lambda i: (i, 0)
    return pl.pallas_call(
        _combine_kernel,
        grid=(n_steps,),
        in_specs=[pl.BlockSpec((TOP_K, tm), col, memory_space=pltpu.SMEM),
                  pl.BlockSpec((TOP_K, tm), lambda i: (0, jnp.minimum(i + 1, n_steps - 1)),
                               memory_space=pltpu.SMEM),
                  pl.BlockSpec((TOP_K, tm), col), pl.BlockSpec((tm, d), row),
                  pl.BlockSpec((1,) + mod.shape[1:], lambda i: (i // per_seq, 0, 0)),
                  _const_spec(g.shape), _const_spec(b.shape),
                  pl.BlockSpec(memory_space=pl.ANY)],
        out_specs=pl.BlockSpec((tm, d), row),
        out_shape=jax.ShapeDtypeStruct((t, d), F32),
        scratch_shapes=[pltpu.VMEM((2 * TOP_K * tm * ROW_TILE, LANES), F32), pltpu.SemaphoreType.DMA((2,))],
        compiler_params=_cparams(("arbitrary",)),
        name="combine",
    )(dest, dest, wts, base, mod, g, b, ys)


def _rope_tables(seq):
    half = HEAD_DIM // 2
    inv_freq = ROPE_THETA ** (-jnp.arange(half, dtype=F32) / half)
    ang = jnp.arange(seq, dtype=F32)[:, None] * inv_freq[None, :]
    cos = jnp.concatenate([jnp.cos(ang), jnp.cos(ang)], axis=1)
    sin = jnp.concatenate([-jnp.sin(ang), jnp.sin(ang)], axis=1)
    return jnp.tile(cos, (1, N_HEADS)), jnp.tile(sin, (1, N_HEADS))


def _layer(x, c, w_ada, b_ada, w_in, ln_v_g, ln_v_b, w_spatial, b_spatial, w_proj_a, w_proj_b, w_out,
           ln1_g, ln1_b, w_router, router_bias, w_gate_e, w_up_e, w_down_e, w_gate_sh, w_up_sh,
           w_down_sh, ln2_g, ln2_b, alpha):
    bsz, seq, d = x.shape
    assert seq % (max(DILATIONS) * BAND) == 0 and seq % TM_IN == 0 and d == ROW_TILE * LANES
    t = bsz * seq
    aw = N_HEADS * HEAD_DIM
    gwid = ln_v_g.shape[0]
    n_exp = w_router.shape[1]
    xf = x.reshape(t, d)

    mod = _ada(c, w_ada, b_ada).reshape(bsz, 6, d)

    cuts = [0, aw, 2 * aw, 3 * aw, 3 * aw + gwid, 3 * aw + 2 * gwid, 3 * aw + 2 * gwid + d,
            3 * aw + 2 * gwid + 2 * d]
    wq, wk, wv, wu, wvg, wga, wgb = [w_in[:, a:b].astype(BF16) for a, b in zip(cuts[:-1], cuts[1:])]
    cos, sin = _rope_tables(seq)
    bsp = jnp.repeat(b_spatial.T, gwid // GMLP_GROUPS, axis=1)
    qkv, ga, gbyb = _inproj(xf, mod, cos, sin, wq, wk, wv, wu, wvg, wga, wgb,
                            ln_v_g.reshape(1, gwid), ln_v_b.reshape(1, gwid), w_spatial, bsp,
                            w_proj_b.astype(BF16), seq)

    oa = _attention(qkv, bsz, seq)

    h2t, base, e_idx, wts, rank, cnt = _post(
        alpha, oa.reshape(t, aw), ga, gbyb, xf, mod, w_proj_a.astype(BF16), w_out.astype(BF16),
        ln1_g.reshape(1, d), ln1_b.reshape(1, d), w_router.T.astype(BF16), router_bias.reshape(n_exp, 1),
        w_gate_sh.astype(BF16), w_up_sh.astype(BF16), w_down_sh.astype(BF16), seq)

    counts = cnt[:, 0].astype(jnp.int32)
    nblk = ((counts + BM - 1) // BM).astype(jnp.int32)
    blk_end = jnp.cumsum(nblk).astype(jnp.int32)
    blk_start = blk_end - nblk
    n_blocks = (t * TOP_K) // BM + n_exp
    n_used = blk_end[-1:]
    row_start = blk_start * BM
    ids = jnp.arange(n_exp, dtype=jnp.int32)
    seen = jnp.where((ids[None, :] <= ids[:, None]) & (nblk[None, :] > 0), ids[None, :], -1).max(axis=1)
    w_idx = jnp.where(seen >= 0, seen, jnp.where(nblk > 0, ids, n_exp - 1).min()).astype(jnp.int32)

    dest = _dest(row_start, e_idx, rank)
    xs = _dispatch(row_start + counts, nblk * BM - counts, n_used, dest, h2t, n_blocks)
    ys = _experts(nblk, blk_start, w_idx, n_used, xs, w_gate_e, w_up_e, w_down_e)
    out = _combine(dest, wts, base, mod, ln2_g.reshape(1, d), ln2_b.reshape(1, d), ys, seq)
    return out.reshape(bsz, seq, d)


def kernel(x, c, w_ada, b_ada, w_in, ln_v_g, ln_v_b, w_spatial, b_spatial, w_proj_a, w_proj_b, w_out,
           ln1_g, ln1_b, w_router, router_bias, w_gate_e, w_up_e, w_down_e, w_gate_sh, w_up_sh,
           w_down_sh, ln2_g, ln2_b):
    depth = w_ada.shape[0]
    alpha = (2.0 * depth) ** 0.25
    for l in range(depth):
        x = _layer(x, c, w_ada[l], b_ada[l], w_in[l], ln_v_g[l], ln_v_b[l], w_spatial[l], b_spatial[l],
                   w_proj_a[l], w_proj_b[l], w_out[l], ln1_g[l], ln1_b[l], w_router[l], router_bias[l],
                   w_gate_e[l], w_up_e[l], w_down_e[l], w_gate_sh[l], w_up_sh[l], w_down_sh[l],
                   ln2_g[l], ln2_b[l], alpha)
    return x
```

```python
import functools
import math

import jax
import jax.numpy as jnp
from jax import lax
from jax.experimental import pallas as pl
from jax.experimental.pallas import tpu as pltpu

F32 = jnp.float32
BF16 = jnp.bfloat16

N_HEADS = 8
HEAD_DIM = 64
DILATIONS = (1, 4, 16)
BAND = 128
ROPE_THETA = 10000.0
GMLP_GROUPS = 8
GMLP_CHUNK = 128
N_EXPERT_GROUPS = 8
TOPK_GROUPS = 4
TOP_K = 8
ROUTED_SCALE = 2.5
LN_EPS = 1e-5
LANES = 128
SUBLANES = 8
ROW_TILE = 8
VMEM_LIMIT = 56 * 1024 * 1024

TM_IN = 512
TM_POST = 512
TM_DISPATCH = 512
TM_COMBINE = 256
BM = 256
EXPERT_RING = 4
ATTN_BLOCKS_PER_TRIP = 16


def _cparams(sem):
    return pltpu.CompilerParams(dimension_semantics=sem, vmem_limit_bytes=VMEM_LIMIT)


def _const_spec(shape):
    nd = len(shape)
    return pl.BlockSpec(shape, lambda *_: (0,) * nd)


def _layer_norm(r, g, b):
    mu = jnp.mean(r, axis=-1, keepdims=True)
    d = r - mu
    var = jnp.mean(d * d, axis=-1, keepdims=True)
    return d * lax.rsqrt(var + LN_EPS) * g + b


def _gelu(t):
    return 0.5 * t * (1.0 + lax.erf(t * (1.0 / math.sqrt(2.0))))


def _sigmoid(t):
    return 1.0 / (1.0 + jnp.exp(-t))


def _store_token_rows(ref, first_row, v):
    m = v.shape[0]
    for s in range(ROW_TILE):
        ref[pl.ds(first_row + s, m, stride=ROW_TILE), :] = v[:, s * LANES:(s + 1) * LANES]


def _load_token_rows(ref, first_row, m):
    return jnp.concatenate([ref[pl.ds(first_row + s, m, stride=ROW_TILE), :] for s in range(ROW_TILE)], axis=1)


def _ada_kernel(c_ref, w_ref, b_ref, o_ref):
    c = c_ref[...]
    o_ref[...] = jnp.dot(c * _sigmoid(c), w_ref[...], preferred_element_type=F32) + b_ref[...]


def _ada(c, w, b):
    bsz, d = c.shape
    n = w.shape[1]
    tn = 1024
    return pl.pallas_call(
        _ada_kernel,
        grid=(n // tn,),
        in_specs=[_const_spec((bsz, d)), pl.BlockSpec((d, tn), lambda j: (0, j)),
                  pl.BlockSpec((1, tn), lambda j: (0, j))],
        out_specs=pl.BlockSpec((bsz, tn), lambda j: (0, j)),
        out_shape=jax.ShapeDtypeStruct((bsz, n), F32),
        compiler_params=_cparams(("arbitrary",)),
        name="ada",
    )(c, w, b.reshape(1, n))


def _inproj_kernel(x_ref, mod_ref, cos_ref, sin_ref, wq_ref, wk_ref, wv_ref, wu_ref, wvg_ref,
                   wga_ref, wgb_ref, lng_ref, lnb_ref, wsp_ref, bsp_ref, wpb_ref, *rest):
    n_cm = len(DILATIONS) - 1
    q_refs, k_refs, v_refs = [rest[(1 + n_cm) * a:(1 + n_cm) * (a + 1)] for a in range(3)]
    ga_ref, gbyb_ref, u_scr, vv_scr, gated_scr, q_stage, k_stage, v_stage = rest[3 * (1 + n_cm):]
    tm = x_ref.shape[0]

    def emit(val, refs, stage):
        refs[0][...] = val.astype(BF16)
        for hp in range(val.shape[1] // LANES):
            stage[pl.ds(hp * tm, tm), :] = val[:, hp * LANES:(hp + 1) * LANES]
        for ref, dil in zip(refs[1:], DILATIONS[1:]):
            for hp in range(val.shape[1] // LANES):
                for r in range(dil):
                    ref[0, hp, r, :, :] = stage[pl.ds(hp * tm + r, tm // dil, stride=dil), :].astype(BF16)
    shift = mod_ref[0, 0:1, :]
    scale = mod_ref[0, 1:2, :]
    h = (x_ref[...] * (1.0 + scale) + shift).astype(BF16)

    cos = cos_ref[...]
    sin = sin_ref[...]
    aw = cos.shape[1]
    lane = lax.broadcasted_iota(jnp.int32, (tm, aw), 1)
    first_half = (lane % HEAD_DIM) < (HEAD_DIM // 2)

    def rope(t):
        partner = jnp.where(first_half, pltpu.roll(t, aw - HEAD_DIM // 2, 1),
                            pltpu.roll(t, HEAD_DIM // 2, 1))
        return t * cos + partner * sin

    emit(rope(jnp.dot(h, wq_ref[...], preferred_element_type=F32)) * (HEAD_DIM ** -0.5), q_refs, q_stage)
    emit(rope(jnp.dot(h, wk_ref[...], preferred_element_type=F32)), k_refs, k_stage)
    emit(jnp.dot(h, wv_ref[...], preferred_element_type=F32), v_refs, v_stage)

    u_scr[...] = _gelu(jnp.dot(h, wu_ref[...], preferred_element_type=F32))
    vg = _gelu(jnp.dot(h, wvg_ref[...], preferred_element_type=F32))
    vv_scr[...] = _layer_norm(vg, lng_ref[...], lnb_ref[...]).astype(BF16)

    ti = lax.broadcasted_iota(jnp.int32, (GMLP_CHUNK, GMLP_CHUNK), 0)
    si = lax.broadcasted_iota(jnp.int32, (GMLP_CHUNK, GMLP_CHUNK), 1)
    causal = si <= ti
    gw = u_scr.shape[1] // GMLP_GROUPS
    for g in range(GMLP_GROUPS):
        ws = jnp.where(causal, wsp_ref[g], 0.0).astype(BF16)
        cols = slice(g * gw, (g + 1) * gw)
        for c in range(tm // GMLP_CHUNK):
            rows = slice(c * GMLP_CHUNK, (c + 1) * GMLP_CHUNK)
            sv = jnp.dot(ws, vv_scr[rows, cols], preferred_element_type=F32) + bsp_ref[:, cols]
            gated_scr[rows, cols] = (u_scr[rows, cols] * sv).astype(BF16)

    y_b = jnp.dot(gated_scr[...], wpb_ref[...], preferred_element_type=F32)
    g_b = _sigmoid(jnp.dot(h, wgb_ref[...], preferred_element_type=F32))
    gbyb_ref[...] = (g_b * y_b).astype(BF16)
    ga_ref[...] = _sigmoid(jnp.dot(h, wga_ref[...], preferred_element_type=F32)).astype(BF16)


def _inproj(xf, mod, cos, sin, wq, wk, wv, wu, wvg, wga, wgb, lng, lnb, wsp, bsp, wpb, seq):
    t, d = xf.shape
    aw = wq.shape[1]
    gwid = wu.shape[1]
    tm = TM_IN
    per_seq = seq // tm
    row = lambda i: (i, 0)
    tab = lambda i: (i % per_seq, 0)
    bsz = t // seq
    qkv_specs = [pl.BlockSpec((tm, aw), row)] + [
        pl.BlockSpec((1, aw // LANES, dil, tm // dil, LANES), lambda i: (i // per_seq, 0, 0, i % per_seq, 0))
        for dil in DILATIONS[1:]]
    qkv_shapes = [jax.ShapeDtypeStruct((t, aw), BF16)] + [
        jax.ShapeDtypeStruct((bsz, aw // LANES, dil, seq // dil, LANES), BF16) for dil in DILATIONS[1:]]
    outs = pl.pallas_call(
        _inproj_kernel,
        grid=(t // tm,),
        in_specs=[pl.BlockSpec((tm, d), row),
                  pl.BlockSpec((1,) + mod.shape[1:], lambda i: (i // per_seq, 0, 0)),
                  pl.BlockSpec((tm, aw), tab), pl.BlockSpec((tm, aw), tab),
                  _const_spec(wq.shape), _const_spec(wk.shape), _const_spec(wv.shape),
                  _const_spec(wu.shape), _const_spec(wvg.shape), _const_spec(wga.shape),
                  _const_spec(wgb.shape), _const_spec(lng.shape), _const_spec(lnb.shape),
                  _const_spec(wsp.shape), _const_spec(bsp.shape), _const_spec(wpb.shape)],
        out_specs=qkv_specs * 3 + [pl.BlockSpec((tm, d), row)] * 2,
        out_shape=qkv_shapes * 3 + [jax.ShapeDtypeStruct((t, d), BF16)] * 2,
        scratch_shapes=[pltpu.VMEM((tm, gwid), F32), pltpu.VMEM((tm, gwid), BF16),
                        pltpu.VMEM((tm, gwid), BF16)] + [pltpu.VMEM((tm * aw // LANES, LANES), F32)] * 3,
        compiler_params=_cparams(("arbitrary",)),
        name="inproj",
    )(xf, mod, cos, sin, wq, wk, wv, wu, wvg, wga, wgb, lng, lnb, wsp, bsp, wpb)
    n_lay = len(DILATIONS)
    return [outs[a * n_lay:(a + 1) * n_lay] for a in range(3)], outs[3 * n_lay], outs[3 * n_lay + 1]


def _rows(start, size, stride):
    if stride == 1:
        return pl.ds(start, size)
    return pl.ds(start, size, stride=stride)


def _band_aligned(start):
    return start if isinstance(start, int) else pl.multiple_of(start, BAND)


def _interleaved_loop(count, load, compute, store):
    per_trip = max(u for u in range(1, ATTN_BLOCKS_PER_TRIP + 1) if count % u == 0) if count else 1

    def trip(it, c):
        items = [it * per_trip + u for u in range(per_trip)]
        results = [compute(ops) for ops in [load(i) for i in items]]
        for i, res in zip(items, results):
            store(i, res)
        return c

    if count == per_trip:
        trip(0, 0)
    elif count:
        lax.fori_loop(0, count // per_trip, trip, 0)


def _attn_kernel(*refs):
    npat = len(DILATIONS)
    qkv = [refs[3 * p:3 * p + 3] for p in range(npat)]
    o_ref, out_scr, lse_scr = refs[3 * npat:]
    seq = o_ref.shape[1]

    lane = lax.broadcasted_iota(jnp.int32, (BAND, LANES), 1)
    head0 = lane < HEAD_DIM
    neg = jnp.float32(-1e30)

    qi = lax.broadcasted_iota(jnp.int32, (BAND, 2 * BAND), 0)
    kj = lax.broadcasted_iota(jnp.int32, (BAND, 2 * BAND), 1)

    def load(p, blk, nb):
        q_ref, k_ref, v_ref = qkv[p]
        q_start = blk * BAND
        k_start = jnp.maximum(q_start - BAND, 0)
        off = q_start - k_start
        lowest = jnp.where(blk % nb == 0, off, qi)
        mask = (kj >= lowest) & (kj <= qi + off)
        krows = pl.ds(_band_aligned(k_start), 2 * BAND)
        return (q_ref[0, 0, pl.ds(_band_aligned(q_start), BAND), :], k_ref[0, 0, krows, :],
                v_ref[0, 0, krows, :], mask)

    def attend(ops):
        qb, kb, vb, mask = ops
        zero = jnp.zeros(qb.shape, qb.dtype)
        parts = []
        for hd in range(2):
            qh = jnp.where(head0 if hd == 0 else ~head0, qb, zero)
            s = lax.dot_general(qh, kb, (((1,), (1,)), ((), ())), preferred_element_type=F32)
            s = jnp.where(mask, s, neg)
            m = jnp.max(s, axis=-1, keepdims=True)
            pr = jnp.exp(s - m)
            l = jnp.sum(pr, axis=-1, keepdims=True)
            acc = jnp.dot(pr.astype(BF16), vb, preferred_element_type=F32)
            parts.append((acc / l, m + jnp.log(l)))
        return [jnp.where(head0, a0, a1) for a0, a1 in zip(*parts)]

    def store(p, seq_start, stride, res):
        rows = _rows(seq_start if stride > 1 else _band_aligned(seq_start), BAND, stride)
        out_scr[p, rows, :] = res[0]
        lse_scr[p, rows, :] = res[1]

    for p, dil in enumerate(DILATIONS):
        n = seq // dil
        nb = n // BAND

        _interleaved_loop(
            seq // BAND, lambda blk, p=p, nb=nb: load(p, blk, nb), attend,
            lambda blk, res, p=p, dil=dil, nb=nb: store(p, blk // nb + dil * BAND * (blk % nb), dil, res))

    top = jnp.maximum(jnp.maximum(lse_scr[0], lse_scr[1]), lse_scr[2])
    num = jnp.zeros(top.shape, F32)
    den = jnp.zeros(top.shape, F32)
    for p in range(npat):
        w = jnp.exp(lse_scr[p] - top)
        num = num + w * out_scr[p]
        den = den + w
    o_ref[0] = (num / den).astype(BF16)


def _attention(qkv, bsz, seq):
    aw = qkv[0][0].shape[1]
    npat = len(DILATIONS)
    blk = (1, 1, seq, LANES)
    operands, specs = [], []
    for p, dil in enumerate(DILATIONS):
        for layouts in qkv:
            if dil == 1:
                operands.append(layouts[p].reshape(bsz, 1, seq, aw))
                specs.append(pl.BlockSpec(blk, lambda b, hp: (b, 0, 0, hp)))
            else:
                operands.append(layouts[p].reshape(bsz, aw // LANES, seq, LANES))
                specs.append(pl.BlockSpec(blk, lambda b, hp: (b, hp, 0, 0)))
    return pl.pallas_call(
        _attn_kernel,
        grid=(bsz, aw // LANES),
        in_specs=specs,
        out_specs=pl.BlockSpec((1, seq, LANES), lambda b, hp: (b, 0, hp)),
        out_shape=jax.ShapeDtypeStruct((bsz, seq, aw), BF16),
        scratch_shapes=[pltpu.VMEM((npat, seq, LANES), F32)] * 2,
        compiler_params=_cparams(("arbitrary", "arbitrary")),
        name="attn",
    )(*operands)


def _post_kernel(alpha, oa_ref, ga_ref, gbyb_ref, x_ref, mod_ref, wpa_ref, wo_ref, l1g_ref, l1b_ref,
                 wrt_ref, rb_ref, wgs_ref, wus_ref, wds_ref, tri_ref,
                 h2_ref, base_ref, e_ref, w_ref, r_ref, cnt_ref):
    tm = x_ref.shape[0]
    n_exp = wrt_ref.shape[0]
    per_group = n_exp // N_EXPERT_GROUPS
    gate_a = mod_ref[0, 2:3, :]
    shift_f = mod_ref[0, 3:4, :]
    scale_f = mod_ref[0, 4:5, :]
    gate_f = mod_ref[0, 5:6, :]

    y_a = jnp.dot(oa_ref[...], wpa_ref[...], preferred_element_type=F32)
    merged = ga_ref[...].astype(F32) * y_a + gbyb_ref[...].astype(F32)
    mix = jnp.dot(merged.astype(BF16), wo_ref[...], preferred_element_type=F32)
    x1 = _layer_norm(alpha * x_ref[...] + gate_a * mix, l1g_ref[...], l1b_ref[...])
    h2 = x1 * (1.0 + scale_f) + shift_f
    h2b = h2.astype(BF16)

    _store_token_rows(h2_ref, 0, h2)

    a = jnp.dot(h2b, wgs_ref[...], preferred_element_type=F32)
    b = jnp.dot(h2b, wus_ref[...], preferred_element_type=F32)
    shared = jnp.dot((a * _sigmoid(a) * b).astype(BF16), wds_ref[...], preferred_element_type=F32)
    base_ref[...] = alpha * x1 + gate_f * shared

    logits = lax.dot_general(wrt_ref[...], h2b, (((1,), (1,)), ((), ())), preferred_element_type=F32)
    scores = _sigmoid(logits)
    biased = scores + rb_ref[...]
    ninf = jnp.float32(-jnp.inf)

    sub = lax.broadcasted_iota(jnp.int32, (per_group, tm), 0)
    grp_rows = []
    for g in range(N_EXPERT_GROUPS):
        blk = biased[g * per_group:(g + 1) * per_group, :]
        m1 = jnp.max(blk, axis=0, keepdims=True)
        f1 = jnp.min(jnp.where(blk == m1, sub, per_group), axis=0, keepdims=True)
        m2 = jnp.max(jnp.where(sub == f1, ninf, blk), axis=0, keepdims=True)
        grp_rows.append(m1 + m2)
    grp = jnp.concatenate(grp_rows, axis=0)

    gi = lax.broadcasted_iota(jnp.int32, grp.shape, 0)
    chosen = jnp.zeros(grp.shape, jnp.bool_)
    cur = grp
    for _ in range(TOPK_GROUPS):
        m = jnp.max(cur, axis=0, keepdims=True)
        f = jnp.min(jnp.where(cur == m, gi, N_EXPERT_GROUPS), axis=0, keepdims=True)
        hit = gi == f
        chosen = chosen | hit
        cur = jnp.where(hit, ninf, cur)
    emask = jnp.concatenate(
        [jnp.broadcast_to(chosen[g:g + 1, :], (per_group, tm)) for g in range(N_EXPERT_GROUPS)], axis=0)

    ei = lax.broadcasted_iota(jnp.int32, (n_exp, tm), 0)
    cand = jnp.where(emask, biased, ninf)
    hits, ids, wts = [], [], []
    for _ in range(TOP_K):
        m = jnp.max(cand, axis=0, keepdims=True)
        f = jnp.min(jnp.where(cand == m, ei, n_exp), axis=0, keepdims=True)
        hit = ei == f
        wts.append(jnp.sum(jnp.where(hit, scores, 0.0), axis=0, keepdims=True))
        cand = jnp.where(hit, ninf, cand)
        hits.append(hit)
        ids.append(f)
    onehot = hits[0]
    for hit in hits[1:]:
        onehot = onehot | hit
    onehot = jnp.where(onehot, 1.0, 0.0)

    @pl.when(pl.program_id(0) == 0)
    def _():
        cnt_ref[...] = jnp.zeros(cnt_ref.shape, F32)

    before = jnp.dot(onehot.astype(BF16), tri_ref[...], preferred_element_type=F32) + cnt_ref[...]
    ranks = [jnp.sum(jnp.where(hit, before, 0.0), axis=0, keepdims=True) for hit in hits]
    cnt_ref[...] += jnp.sum(onehot, axis=1, keepdims=True)

    denom = wts[0]
    for wk in wts[1:]:
        denom = denom + wk
    e_ref[...] = jnp.concatenate(ids, axis=0)
    w_ref[...] = jnp.concatenate([ROUTED_SCALE * wk / denom for wk in wts], axis=0)
    r_ref[...] = jnp.concatenate(ranks, axis=0).astype(jnp.int32)


def _post(alpha, oa, ga, gbyb, xf, mod, wpa, wo, l1g, l1b, wrt, rb, wgs, wus, wds, seq):
    t, d = xf.shape
    aw = oa.shape[1]
    n_exp = wrt.shape[0]
    tm = TM_POST
    per_seq = seq // tm
    row = lambda i: (i, 0)
    col = lambda i: (0, i)
    tri = (jnp.arange(tm)[:, None] < jnp.arange(tm)[None, :]).astype(BF16)
    return pl.pallas_call(
        functools.partial(_post_kernel, alpha),
        grid=(t // tm,),
        in_specs=[pl.BlockSpec((tm, aw), row), pl.BlockSpec((tm, d), row), pl.BlockSpec((tm, d), row),
                  pl.BlockSpec((tm, d), row),
                  pl.BlockSpec((1,) + mod.shape[1:], lambda i: (i // per_seq, 0, 0)),
                  _const_spec(wpa.shape), _const_spec(wo.shape), _const_spec(l1g.shape),
                  _const_spec(l1b.shape), _const_spec(wrt.shape), _const_spec(rb.shape),
                  _const_spec(wgs.shape), _const_spec(wus.shape), _const_spec(wds.shape),
                  _const_spec(tri.shape)],
        out_specs=[pl.BlockSpec((tm * ROW_TILE, LANES), row), pl.BlockSpec((tm, d), row),
                   pl.BlockSpec((TOP_K, tm), col), pl.BlockSpec((TOP_K, tm), col),
                   pl.BlockSpec((TOP_K, tm), col), _const_spec((n_exp, 1))],
        out_shape=[jax.ShapeDtypeStruct((t * ROW_TILE, LANES), F32), jax.ShapeDtypeStruct((t, d), F32),
                   jax.ShapeDtypeStruct((TOP_K, t), jnp.int32), jax.ShapeDtypeStruct((TOP_K, t), F32),
                   jax.ShapeDtypeStruct((TOP_K, t), jnp.int32), jax.ShapeDtypeStruct((n_exp, 1), F32)],
        compiler_params=_cparams(("arbitrary",)),
        name="post",
    )(oa, ga, gbyb, xf, mod, wpa, wo, l1g, l1b, wrt, rb, wgs, wus, wds, tri)


def _row_copy(src_ref, src_row, dst_ref, dst_row, sem):
    return pltpu.make_async_copy(
        src_ref.at[pl.ds(pl.multiple_of(src_row * ROW_TILE, ROW_TILE), ROW_TILE), :],
        dst_ref.at[pl.ds(pl.multiple_of(dst_row * ROW_TILE, ROW_TILE), ROW_TILE), :], sem)


def _dest_kernel(rs_ref, e_ref, r_ref, o_ref):
    e = e_ref[...]
    start = lax.fori_loop(0, rs_ref.shape[0], lambda x, acc: jnp.where(e == x, rs_ref[x], acc),
                          jnp.zeros(e.shape, jnp.int32))
    o_ref[...] = start + r_ref[...]


def _dest(row_start, e_idx, rank):
    t = e_idx.shape[1]
    tm = min(t, 2048)
    blk = pl.BlockSpec((TOP_K, tm), lambda i, *_: (0, i))
    return pl.pallas_call(
        _dest_kernel,
        grid_spec=pltpu.PrefetchScalarGridSpec(num_scalar_prefetch=1, grid=(t // tm,), in_specs=[blk, blk],
                                               out_specs=blk),
        out_shape=jax.ShapeDtypeStruct((TOP_K, t), jnp.int32),
        compiler_params=_cparams(("arbitrary",)),
        name="dest",
    )(row_start, e_idx, rank)


def _dispatch_kernel(n_blocks, ps_ref, pl_ref, nu_ref, dest_ref, h2_ref, xs_ref, zbuf, sem, zsem):
    i = pl.program_id(0)
    n_steps = pl.num_programs(0)
    tm = dest_ref.shape[1]
    n_exp = ps_ref.shape[0]
    per_step = -(-n_exp // n_steps)
    zbuf[...] = jnp.zeros(zbuf.shape, F32)

    def zero_fill(act):
        for q in range(per_step):
            e = jnp.minimum(i * per_step + q, n_exp - 1)
            live = i * per_step + q < n_exp
            gap = pl_ref[e]
            for bit in [1 << s for s in range(BM.bit_length() - 1)]:
                @pl.when(live & ((gap & bit) != 0))
                def _(bit=bit, e=e, gap=gap):
                    off = ps_ref[e] + (gap & ~(2 * bit - 1))
                    act(pltpu.make_async_copy(
                        zbuf.at[pl.ds(0, bit * ROW_TILE), :],
                        xs_ref.at[pl.ds(pl.multiple_of(off * ROW_TILE, ROW_TILE), bit * ROW_TILE), :], zsem))
            blk = nu_ref[0] + i * per_step + q
            @pl.when(blk < n_blocks)
            def _(blk=blk):
                rows = BM * ROW_TILE
                act(pltpu.make_async_copy(
                    zbuf, xs_ref.at[pl.ds(pl.multiple_of(blk * rows, rows), rows), :], zsem))

    zero_fill(lambda cp: cp.start(priority=1))

    def issue(j, c):
        for k in range(TOP_K):
            _row_copy(h2_ref, j, xs_ref, dest_ref[k, j], sem).start(priority=k % 2)
        return c

    lax.fori_loop(0, tm, issue, 0)
    n = tm * TOP_K * ROW_TILE
    pltpu.make_async_copy(xs_ref.at[pl.ds(0, n), :], xs_ref.at[pl.ds(0, n), :], sem).wait()
    zero_fill(lambda cp: cp.wait())


def _dispatch(pad_start, pad_len, n_used, dest, h2t, n_blocks):
    t = dest.shape[1]
    tm = TM_DISPATCH
    grid_spec = pltpu.PrefetchScalarGridSpec(
        num_scalar_prefetch=3,
        grid=(t // tm,),
        in_specs=[pl.BlockSpec((TOP_K, tm), lambda i, *_: (0, i), memory_space=pltpu.SMEM),
                  pl.BlockSpec((tm * ROW_TILE, LANES), lambda i, *_: (i, 0))],
        out_specs=pl.BlockSpec(memory_space=pl.ANY),
        scratch_shapes=[pltpu.VMEM((BM * ROW_TILE, LANES), F32), pltpu.SemaphoreType.DMA,
                        pltpu.SemaphoreType.DMA],
    )
    return pl.pallas_call(
        functools.partial(_dispatch_kernel, n_blocks),
        grid_spec=grid_spec,
        out_shape=jax.ShapeDtypeStruct((n_blocks * BM * ROW_TILE, LANES), F32),
        compiler_params=_cparams(("arbitrary",)),
        name="dispatch",
    )(pad_start, pad_len, n_used, dest, h2t)


def _expert_kernel(nblk_ref, bstart_ref, widx_ref, nu_ref, xs_ref, wg_ref, wu_ref, wd_ref, ys_ref,
                   xbuf, ybuf, wg_b, wu_b, wd_b, xsem, ysem):
    e = pl.program_id(0)
    n_used = nu_ref[0]
    rows = BM * ROW_TILE
    ring = EXPERT_RING

    def slot_base(g):
        return pl.multiple_of((g % ring) * rows, rows)

    def hbm_rows(ref, g):
        return ref.at[pl.ds(pl.multiple_of(g * rows, rows), rows), :]

    def x_copy(g):
        return pltpu.make_async_copy(hbm_rows(xs_ref, g), xbuf.at[pl.ds(slot_base(g), rows), :], xsem.at[g % ring])

    def y_copy(g):
        return pltpu.make_async_copy(ybuf.at[pl.ds(slot_base(g), rows), :], hbm_rows(ys_ref, g), ysem.at[g % ring])

    @pl.when(e == 0)
    def _():
        for g0 in range(ring - 1):
            @pl.when(g0 < n_used)
            def _(g0=g0):
                x_copy(g0).start()

    wg_b[...] = wg_ref[0].astype(BF16)
    wu_b[...] = wu_ref[0].astype(BF16)
    wd_b[...] = wd_ref[0].astype(BF16)

    def block(b, c):
        g = bstart_ref[e] + b
        x_copy(g).wait()

        @pl.when(g + ring - 1 < n_used)
        def _():
            x_copy(g + ring - 1).start()

        @pl.when(g >= ring)
        def _():
            y_copy(g - ring).wait()

        x = _load_token_rows(xbuf, slot_base(g), BM).astype(BF16)
        a = jnp.dot(x, wg_b[...], preferred_element_type=F32)
        u = jnp.dot(x, wu_b[...], preferred_element_type=F32)
        act = (a * _sigmoid(a) * u).astype(BF16)
        _store_token_rows(ybuf, slot_base(g), jnp.dot(act, wd_b[...], preferred_element_type=F32))
        y_copy(g).start(priority=1)
        return c

    lax.fori_loop(0, nblk_ref[e], block, 0)

    @pl.when(e == pl.num_programs(0) - 1)
    def _():
        for back in range(ring, 0, -1):
            @pl.when(n_used >= back)
            def _(back=back):
                y_copy(n_used - back).wait()


def _experts(nblk, blk_start, w_idx, n_used, xs, wg, wu, wd):
    n_exp, d, de = wg.shape
    rows = BM * ROW_TILE
    wsel = lambda e, nb, bs, wi, nu: (wi[e], 0, 0)
    grid_spec = pltpu.PrefetchScalarGridSpec(
        num_scalar_prefetch=4,
        grid=(n_exp,),
        in_specs=[pl.BlockSpec(memory_space=pl.ANY),
                  pl.BlockSpec((1, d, de), wsel), pl.BlockSpec((1, d, de), wsel),
                  pl.BlockSpec((1, de, d), wsel)],
        out_specs=pl.BlockSpec(memory_space=pl.ANY),
        scratch_shapes=[pltpu.VMEM((EXPERT_RING * rows, LANES), F32), pltpu.VMEM((EXPERT_RING * rows, LANES), F32),
                        pltpu.VMEM((d, de), BF16), pltpu.VMEM((d, de), BF16), pltpu.VMEM((de, d), BF16),
                        pltpu.SemaphoreType.DMA((EXPERT_RING,)), pltpu.SemaphoreType.DMA((EXPERT_RING,))],
    )
    return pl.pallas_call(
        _expert_kernel,
        grid_spec=grid_spec,
        out_shape=jax.ShapeDtypeStruct(xs.shape, xs.dtype),
        input_output_aliases={4: 0},
        compiler_params=_cparams(("arbitrary",)),
        name="experts",
    )(nblk, blk_start, w_idx, n_used, xs, wg, wu, wd)


def _combine_kernel(dest_ref, next_dest_ref, w_ref, base_ref, mod_ref, g_ref, b_ref, ys_ref, o_ref, buf, sems):
    i = pl.program_id(0)
    tm = dest_ref.shape[1]
    slab = tm * ROW_TILE
    slot_rows = TOP_K * slab

    def gather(idx_ref, slot):
        def issue(j, c):
            for k in range(TOP_K):
                _row_copy(ys_ref, idx_ref[k, j], buf, slot * (TOP_K * tm) + k * tm + j,
                          sems.at[slot]).start(priority=k % 2)
            return c
        lax.fori_loop(0, tm, issue, 0)

    @pl.when(i == 0)
    def _():
        gather(dest_ref, 0)

    @pl.when(i + 1 < pl.num_programs(0))
    def _():
        gather(next_dest_ref, (i + 1) % 2)

    slot = i % 2
    base_row = pl.multiple_of(slot * slot_rows, slot_rows)
    pltpu.make_async_copy(ys_ref.at[pl.ds(0, slot_rows), :], buf.at[pl.ds(base_row, slot_rows), :],
                          sems.at[slot]).wait()

    wpad = jnp.concatenate([w_ref[...], jnp.zeros((LANES - TOP_K, tm), F32)], axis=0)
    wcol = wpad.T
    routed = jnp.zeros(base_ref.shape, F32)
    for k in range(TOP_K):
        routed = routed + wcol[:, k:k + 1] * _load_token_rows(buf, base_row + k * slab, tm)
    gate_f = mod_ref[0, 5:6, :]
    o_ref[...] = _layer_norm(base_ref[...] + gate_f * routed, g_ref[...], b_ref[...])


def _combine(dest, wts, base, mod, g, b, ys, seq):
    t, d = base.shape
    tm = TM_COMBINE
    per_seq = seq // tm
    n_steps = t // tm
    col = lambda i: (0, i)
    row = lambda i: (i, 0)
    return pl.pallas_call(
        _combine_kernel,
        grid=(n_steps,),
        in_specs=[pl.BlockSpec((TOP_K, tm), col, memory_space=pltpu.SMEM),
                  pl.BlockSpec((TOP_K, tm), lambda i: (0, jnp.minimum(i + 1, n_steps - 1)),
                               memory_space=pltpu.SMEM),
                  pl.BlockSpec((TOP_K, tm), col), pl.BlockSpec((tm, d), row),
                  pl.BlockSpec((1,) + mod.shape[1:], lambda i: (i // per_seq, 0, 0)),
                  _const_spec(g.shape), _const_spec(b.shape),
                  pl.BlockSpec(memory_space=pl.ANY)],
        out_specs=pl.BlockSpec((tm, d), row),
        out_shape=jax.ShapeDtypeStruct((t, d), F32),
        scratch_shapes=[pltpu.VMEM((2 * TOP_K * tm * ROW_TILE, LANES), F32), pltpu.SemaphoreType.DMA((2,))],
        compiler_params=_cparams(("arbitrary",)),
        name="combine",
    )(dest, dest, wts, base, mod, g, b, ys)


def _rope_tables(seq):
    half = HEAD_DIM // 2
    inv_freq = ROPE_THETA ** (-jnp.arange(half, dtype=F32) / half)
    ang = jnp.arange(seq, dtype=F32)[:, None] * inv_freq[None, :]
    cos = jnp.concatenate([jnp.cos(ang), jnp.cos(ang)], axis=1)
    sin = jnp.concatenate([-jnp.sin(ang), jnp.sin(ang)], axis=1)
    return jnp.tile(cos, (1, N_HEADS)), jnp.tile(sin, (1, N_HEADS))


def _layer(x, c, w_ada, b_ada, w_in, ln_v_g, ln_v_b, w_spatial, b_spatial, w_proj_a, w_proj_b, w_out,
           ln1_g, ln1_b, w_router, router_bias, w_gate_e, w_up_e, w_down_e, w_gate_sh, w_up_sh,
           w_down_sh, ln2_g, ln2_b, alpha):
    bsz, seq, d = x.shape
    assert seq % (max(DILATIONS) * BAND) == 0 and seq % TM_IN == 0 and d == ROW_TILE * LANES
    t = bsz * seq
    aw = N_HEADS * HEAD_DIM
    gwid = ln_v_g.shape[0]
    n_exp = w_router.shape[1]
    xf = x.reshape(t, d)

    mod = _ada(c, w_ada, b_ada).reshape(bsz, 6, d)

    cuts = [0, aw, 2 * aw, 3 * aw, 3 * aw + gwid, 3 * aw + 2 * gwid, 3 * aw + 2 * gwid + d,
            3 * aw + 2 * gwid + 2 * d]
    wq, wk, wv, wu, wvg, wga, wgb = [w_in[:, a:b].astype(BF16) for a, b in zip(cuts[:-1], cuts[1:])]
    cos, sin = _rope_tables(seq)
    bsp = jnp.repeat(b_spatial.T, gwid // GMLP_GROUPS, axis=1)
    qkv, ga, gbyb = _inproj(xf, mod, cos, sin, wq, wk, wv, wu, wvg, wga, wgb,
                            ln_v_g.reshape(1, gwid), ln_v_b.reshape(1, gwid), w_spatial, bsp,
                            w_proj_b.astype(BF16), seq)

    oa = _attention(qkv, bsz, seq)

    h2t, base, e_idx, wts, rank, cnt = _post(
        alpha, oa.reshape(t, aw), ga, gbyb, xf, mod, w_proj_a.astype(BF16), w_out.astype(BF16),
        ln1_g.reshape(1, d), ln1_b.reshape(1, d), w_router.T.astype(BF16), router_bias.reshape(n_exp, 1),
        w_gate_sh.astype(BF16), w_up_sh.astype(BF16), w_down_sh.astype(BF16), seq)

    counts = cnt[:, 0].astype(jnp.int32)
    nblk = ((counts + BM - 1) // BM).astype(jnp.int32)
    blk_end = jnp.cumsum(nblk).astype(jnp.int32)
    blk_start = blk_end - nblk
    n_blocks = (t * TOP_K) // BM + n_exp
    n_used = blk_end[-1:]
    row_start = blk_start * BM
    ids = jnp.arange(n_exp, dtype=jnp.int32)
    seen = jnp.where((ids[None, :] <= ids[:, None]) & (nblk[None, :] > 0), ids[None, :], -1).max(axis=1)
    w_idx = jnp.where(seen >= 0, seen, jnp.where(nblk > 0, ids, n_exp - 1).min()).astype(jnp.int32)

    dest = _dest(row_start, e_idx, rank)
    xs = _dispatch(row_start + counts, nblk * BM - counts, n_used, dest, h2t, n_blocks)
    ys = _experts(nblk, blk_start, w_idx, n_used, xs, w_gate_e, w_up_e, w_down_e)
    out = _combine(dest, wts, base, mod, ln2_g.reshape(1, d), ln2_b.reshape(1, d), ys, seq)
    return out.reshape(bsz, seq, d)


def kernel(x, c, w_ada, b_ada, w_in, ln_v_g, ln_v_b, w_spatial, b_spatial, w_proj_a, w_proj_b, w_out,
           ln1_g, ln1_b, w_router, router_bias, w_gate_e, w_up_e, w_down_e, w_gate_sh, w_up_sh,
           w_down_sh, ln2_g, ln2_b):
    depth = w_ada.shape[0]
    alpha = (2.0 * depth) ** 0.25
    for l in range(depth):
        x = _layer(x, c, w_ada[l], b_ada[l], w_in[l], ln_v_g[l], ln_v_b[l], w_spatial[l], b_spatial[l],
                   w_proj_a[l], w_proj_b[l], w_out[l], ln1_g[l], ln1_b[l], w_router[l], router_bias[l],
                   w_gate_e[l], w_up_e[l], w_down_e[l], w_gate_sh[l], w_up_sh[l], w_down_sh[l],
                   ln2_g[l], ln2_b[l], alpha)
    return x
```

```python
import functools
import math

import jax
import jax.numpy as jnp
from jax import lax
from jax.experimental import pallas as pl
from jax.experimental.pallas import tpu as pltpu

F32 = jnp.float32
BF16 = jnp.bfloat16

N_HEADS = 8
HEAD_DIM = 64
DILATIONS = (1, 4, 16)
BAND = 128
ROPE_THETA = 10000.0
GMLP_GROUPS = 8
GMLP_CHUNK = 128
N_EXPERT_GROUPS = 8
TOPK_GROUPS = 4
TOP_K = 8
ROUTED_SCALE = 2.5
LN_EPS = 1e-5
LANES = 128
SUBLANES = 8
ROW_TILE = 8
VMEM_LIMIT = 56 * 1024 * 1024

TM_IN = 512
TM_POST = 512
TM_DISPATCH = 1024
TM_COMBINE = 256
BM = 256
EXPERT_RING = 4
EXPERT_CHUNKS = 4
ATTN_BLOCKS_PER_TRIP = 16


def _cparams(sem):
    return pltpu.CompilerParams(dimension_semantics=sem, vmem_limit_bytes=VMEM_LIMIT)


def _const_spec(shape):
    nd = len(shape)
    return pl.BlockSpec(shape, lambda *_: (0,) * nd)


def _layer_norm(r, g, b):
    mu = jnp.mean(r, axis=-1, keepdims=True)
    d = r - mu
    var = jnp.mean(d * d, axis=-1, keepdims=True)
    return d * lax.rsqrt(var + LN_EPS) * g + b


def _gelu(t):
    return 0.5 * t * (1.0 + lax.erf(t * (1.0 / math.sqrt(2.0))))


def _sigmoid(t):
    return 1.0 / (1.0 + jnp.exp(-t))


def _store_token_rows(ref, first_row, v):
    m = v.shape[0]
    for s in range(ROW_TILE):
        ref[pl.ds(first_row + s, m, stride=ROW_TILE), :] = v[:, s * LANES:(s + 1) * LANES]


def _load_token_rows(ref, first_row, m):
    return jnp.concatenate([ref[pl.ds(first_row + s, m, stride=ROW_TILE), :] for s in range(ROW_TILE)], axis=1)


def _ada_kernel(c_ref, w_ref, b_ref, o_ref):
    c = c_ref[...]
    o_ref[...] = jnp.dot(c * _sigmoid(c), w_ref[...], preferred_element_type=F32) + b_ref[...]


def _ada(c, w, b):
    bsz, d = c.shape
    n = w.shape[1]
    tn = 1024
    return pl.pallas_call(
        _ada_kernel,
        grid=(n // tn,),
        in_specs=[_const_spec((bsz, d)), pl.BlockSpec((d, tn), lambda j: (0, j)),
                  pl.BlockSpec((1, tn), lambda j: (0, j))],
        out_specs=pl.BlockSpec((bsz, tn), lambda j: (0, j)),
        out_shape=jax.ShapeDtypeStruct((bsz, n), F32),
        compiler_params=_cparams(("arbitrary",)),
        name="ada",
    )(c, w, b.reshape(1, n))


def _inproj_kernel(x_ref, mod_ref, cos_ref, sin_ref, wq_ref, wk_ref, wv_ref, wu_ref, wvg_ref,
                   wga_ref, wgb_ref, lng_ref, lnb_ref, wsp_ref, bsp_ref, wpb_ref, *rest):
    n_cm = len(DILATIONS) - 1
    q_refs, k_refs, v_refs = [rest[(1 + n_cm) * a:(1 + n_cm) * (a + 1)] for a in range(3)]
    ga_ref, gbyb_ref, u_scr, vv_scr, gated_scr, q_stage, k_stage, v_stage = rest[3 * (1 + n_cm):]
    tm = x_ref.shape[0]

    def emit(val, refs, stage):
        refs[0][...] = val.astype(BF16)
        for hp in range(val.shape[1] // LANES):
            stage[pl.ds(hp * tm, tm), :] = val[:, hp * LANES:(hp + 1) * LANES]
        for ref, dil in zip(refs[1:], DILATIONS[1:]):
            for hp in range(val.shape[1] // LANES):
                for r in range(dil):
                    ref[0, hp, r, :, :] = stage[pl.ds(hp * tm + r, tm // dil, stride=dil), :].astype(BF16)
    shift = mod_ref[0, 0:1, :]
    scale = mod_ref[0, 1:2, :]
    h = (x_ref[...] * (1.0 + scale) + shift).astype(BF16)

    cos = cos_ref[...]
    sin = sin_ref[...]
    aw = cos.shape[1]
    lane = lax.broadcasted_iota(jnp.int32, (tm, aw), 1)
    first_half = (lane % HEAD_DIM) < (HEAD_DIM // 2)

    def rope(t):
        partner = jnp.where(first_half, pltpu.roll(t, aw - HEAD_DIM // 2, 1),
                            pltpu.roll(t, HEAD_DIM // 2, 1))
        return t * cos + partner * sin

    emit(rope(jnp.dot(h, wq_ref[...], preferred_element_type=F32)) * (HEAD_DIM ** -0.5), q_refs, q_stage)
    emit(rope(jnp.dot(h, wk_ref[...], preferred_element_type=F32)), k_refs, k_stage)
    emit(jnp.dot(h, wv_ref[...], preferred_element_type=F32), v_refs, v_stage)

    u_scr[...] = _gelu(jnp.dot(h, wu_ref[...], preferred_element_type=F32))
    vg = _gelu(jnp.dot(h, wvg_ref[...], preferred_element_type=F32))
    vv_scr[...] = _layer_norm(vg, lng_ref[...], lnb_ref[...]).astype(BF16)

    ti = lax.broadcasted_iota(jnp.int32, (GMLP_CHUNK, GMLP_CHUNK), 0)
    si = lax.broadcasted_iota(jnp.int32, (GMLP_CHUNK, GMLP_CHUNK), 1)
    causal = si <= ti
    gw = u_scr.shape[1] // GMLP_GROUPS
    for g in range(GMLP_GROUPS):
        ws = jnp.where(causal, wsp_ref[g], 0.0).astype(BF16)
        cols = slice(g * gw, (g + 1) * gw)
        for c in range(tm // GMLP_CHUNK):
            rows = slice(c * GMLP_CHUNK, (c + 1) * GMLP_CHUNK)
            sv = jnp.dot(ws, vv_scr[rows, cols], preferred_element_type=F32) + bsp_ref[:, cols]
            gated_scr[rows, cols] = (u_scr[rows, cols] * sv).astype(BF16)

    y_b = jnp.dot(gated_scr[...], wpb_ref[...], preferred_element_type=F32)
    g_b = _sigmoid(jnp.dot(h, wgb_ref[...], preferred_element_type=F32))
    gbyb_ref[...] = (g_b * y_b).astype(BF16)
    ga_ref[...] = _sigmoid(jnp.dot(h, wga_ref[...], preferred_element_type=F32)).astype(BF16)


def _inproj(xf, mod, cos, sin, wq, wk, wv, wu, wvg, wga, wgb, lng, lnb, wsp, bsp, wpb, seq):
    t, d = xf.shape
    aw = wq.shape[1]
    gwid = wu.shape[1]
    tm = TM_IN
    per_seq = seq // tm
    row = lambda i: (i, 0)
    tab = lambda i: (i % per_seq, 0)
    bsz = t // seq
    qkv_specs = [pl.BlockSpec((tm, aw), row)] + [
        pl.BlockSpec((1, aw // LANES, dil, tm // dil, LANES), lambda i: (i // per_seq, 0, 0, i % per_seq, 0))
        for dil in DILATIONS[1:]]
    qkv_shapes = [jax.ShapeDtypeStruct((t, aw), BF16)] + [
        jax.ShapeDtypeStruct((bsz, aw // LANES, dil, seq // dil, LANES), BF16) for dil in DILATIONS[1:]]
    outs = pl.pallas_call(
        _inproj_kernel,
        grid=(t // tm,),
        in_specs=[pl.BlockSpec((tm, d), row),
                  pl.BlockSpec((1,) + mod.shape[1:], lambda i: (i // per_seq, 0, 0)),
                  pl.BlockSpec((tm, aw), tab), pl.BlockSpec((tm, aw), tab),
                  _const_spec(wq.shape), _const_spec(wk.shape), _const_spec(wv.shape),
                  _const_spec(wu.shape), _const_spec(wvg.shape), _const_spec(wga.shape),
                  _const_spec(wgb.shape), _const_spec(lng.shape), _const_spec(lnb.shape),
                  _const_spec(wsp.shape), _const_spec(bsp.shape), _const_spec(wpb.shape)],
        out_specs=qkv_specs * 3 + [pl.BlockSpec((tm, d), row)] * 2,
        out_shape=qkv_shapes * 3 + [jax.ShapeDtypeStruct((t, d), BF16)] * 2,
        scratch_shapes=[pltpu.VMEM((tm, gwid), F32), pltpu.VMEM((tm, gwid), BF16),
                        pltpu.VMEM((tm, gwid), BF16)] + [pltpu.VMEM((tm * aw // LANES, LANES), F32)] * 3,
        compiler_params=_cparams(("arbitrary",)),
        name="inproj",
    )(xf, mod, cos, sin, wq, wk, wv, wu, wvg, wga, wgb, lng, lnb, wsp, bsp, wpb)
    n_lay = len(DILATIONS)
    return [outs[a * n_lay:(a + 1) * n_lay] for a in range(3)], outs[3 * n_lay], outs[3 * n_lay + 1]


def _rows(start, size, stride):
    if stride == 1:
        return pl.ds(start, size)
    return pl.ds(start, size, stride=stride)


def _band_aligned(start):
    return start if isinstance(start, int) else pl.multiple_of(start, BAND)


def _interleaved_loop(count, load, compute, store):
    per_trip = max(u for u in range(1, ATTN_BLOCKS_PER_TRIP + 1) if count % u == 0) if count else 1

    def trip(it, c):
        items = [it * per_trip + u for u in range(per_trip)]
        results = [compute(ops) for ops in [load(i) for i in items]]
        for i, res in zip(items, results):
            store(i, res)
        return c

    if count == per_trip:
        trip(0, 0)
    elif count:
        lax.fori_loop(0, count // per_trip, trip, 0)


def _attn_kernel(*refs):
    npat = len(DILATIONS)
    qkv = [refs[3 * p:3 * p + 3] for p in range(npat)]
    o_ref, out_scr, lse_scr = refs[3 * npat:]
    seq = o_ref.shape[1]

    lane = lax.broadcasted_iota(jnp.int32, (BAND, LANES), 1)
    head0 = lane < HEAD_DIM
    neg = jnp.float32(-1e30)

    qi = lax.broadcasted_iota(jnp.int32, (BAND, 2 * BAND), 0)
    kj = lax.broadcasted_iota(jnp.int32, (BAND, 2 * BAND), 1)

    def load(p, blk, nb):
        q_ref, k_ref, v_ref = qkv[p]
        q_start = blk * BAND
        k_start = jnp.maximum(q_start - BAND, 0)
        off = q_start - k_start
        lowest = jnp.where(blk % nb == 0, off, qi)
        mask = (kj >= lowest) & (kj <= qi + off)
        krows = pl.ds(_band_aligned(k_start), 2 * BAND)
        return (q_ref[0, 0, pl.ds(_band_aligned(q_start), BAND), :], k_ref[0, 0, krows, :],
                v_ref[0, 0, krows, :], mask)

    def attend(ops):
        qb, kb, vb, mask = ops
        zero = jnp.zeros(qb.shape, qb.dtype)
        parts = []
        for hd in range(2):
            qh = jnp.where(head0 if hd == 0 else ~head0, qb, zero)
            s = lax.dot_general(qh, kb, (((1,), (1,)), ((), ())), preferred_element_type=F32)
            s = jnp.where(mask, s, neg)
            m = jnp.max(s, axis=-1, keepdims=True)
            pr = jnp.exp(s - m)
            l = jnp.sum(pr, axis=-1, keepdims=True)
            acc = jnp.dot(pr.astype(BF16), vb, preferred_element_type=F32)
            parts.append((acc / l, m + jnp.log(l)))
        return [jnp.where(head0, a0, a1) for a0, a1 in zip(*parts)]

    def store(p, seq_start, stride, res):
        rows = _rows(seq_start if stride > 1 else _band_aligned(seq_start), BAND, stride)
        out_scr[p, rows, :] = res[0]
        lse_scr[p, rows, :] = res[1]

    for p, dil in enumerate(DILATIONS):
        n = seq // dil
        nb = n // BAND

        _interleaved_loop(
            seq // BAND, lambda blk, p=p, nb=nb: load(p, blk, nb), attend,
            lambda blk, res, p=p, dil=dil, nb=nb: store(p, blk // nb + dil * BAND * (blk % nb), dil, res))

    top = jnp.maximum(jnp.maximum(lse_scr[0], lse_scr[1]), lse_scr[2])
    num = jnp.zeros(top.shape, F32)
    den = jnp.zeros(top.shape, F32)
    for p in range(npat):
        w = jnp.exp(lse_scr[p] - top)
        num = num + w * out_scr[p]
        den = den + w
    o_ref[0] = (num / den).astype(BF16)


def _attention(qkv, bsz, seq):
    aw = qkv[0][0].shape[1]
    npat = len(DILATIONS)
    blk = (1, 1, seq, LANES)
    operands, specs = [], []
    for p, dil in enumerate(DILATIONS):
        for layouts in qkv:
            if dil == 1:
                operands.append(layouts[p].reshape(bsz, 1, seq, aw))
                specs.append(pl.BlockSpec(blk, lambda b, hp: (b, 0, 0, hp)))
            else:
                operands.append(layouts[p].reshape(bsz, aw // LANES, seq, LANES))
                specs.append(pl.BlockSpec(blk, lambda b, hp: (b, hp, 0, 0)))
    return pl.pallas_call(
        _attn_kernel,
        grid=(bsz, aw // LANES),
        in_specs=specs,
        out_specs=pl.BlockSpec((1, seq, LANES), lambda b, hp: (b, 0, hp)),
        out_shape=jax.ShapeDtypeStruct((bsz, seq, aw), BF16),
        scratch_shapes=[pltpu.VMEM((npat, seq, LANES), F32)] * 2,
        compiler_params=_cparams(("arbitrary", "arbitrary")),
        name="attn",
    )(*operands)


def _post_kernel(alpha, oa_ref, ga_ref, gbyb_ref, x_ref, mod_ref, wpa_ref, wo_ref, l1g_ref, l1b_ref,
                 wrt_ref, rb_ref, wgs_ref, wus_ref, wds_ref, tri_ref,
                 h2_ref, base_ref, e_ref, w_ref, r_ref, cnt_ref):
    tm = x_ref.shape[0]
    n_exp = wrt_ref.shape[0]
    per_group = n_exp // N_EXPERT_GROUPS
    gate_a = mod_ref[0, 2:3, :]
    shift_f = mod_ref[0, 3:4, :]
    scale_f = mod_ref[0, 4:5, :]
    gate_f = mod_ref[0, 5:6, :]

    y_a = jnp.dot(oa_ref[...], wpa_ref[...], preferred_element_type=F32)
    merged = ga_ref[...].astype(F32) * y_a + gbyb_ref[...].astype(F32)
    mix = jnp.dot(merged.astype(BF16), wo_ref[...], preferred_element_type=F32)
    x1 = _layer_norm(alpha * x_ref[...] + gate_a * mix, l1g_ref[...], l1b_ref[...])
    h2 = x1 * (1.0 + scale_f) + shift_f
    h2b = h2.astype(BF16)

    _store_token_rows(h2_ref, 0, h2)

    a = jnp.dot(h2b, wgs_ref[...], preferred_element_type=F32)
    b = jnp.dot(h2b, wus_ref[...], preferred_element_type=F32)
    shared = jnp.dot((a * _sigmoid(a) * b).astype(BF16), wds_ref[...], preferred_element_type=F32)
    base_ref[...] = alpha * x1 + gate_f * shared

    logits = lax.dot_general(wrt_ref[...], h2b, (((1,), (1,)), ((), ())), preferred_element_type=F32)
    scores = _sigmoid(logits)
    biased = scores + rb_ref[...]
    ninf = jnp.float32(-jnp.inf)

    sub = lax.broadcasted_iota(jnp.int32, (per_group, tm), 0)
    grp_rows = []
    for g in range(N_EXPERT_GROUPS):
        blk = biased[g * per_group:(g + 1) * per_group, :]
        m1 = jnp.max(blk, axis=0, keepdims=True)
        f1 = jnp.min(jnp.where(blk == m1, sub, per_group), axis=0, keepdims=True)
        m2 = jnp.max(jnp.where(sub == f1, ninf, blk), axis=0, keepdims=True)
        grp_rows.append(m1 + m2)
    grp = jnp.concatenate(grp_rows, axis=0)

    gi = lax.broadcasted_iota(jnp.int32, grp.shape, 0)
    chosen = jnp.zeros(grp.shape, jnp.bool_)
    cur = grp
    for _ in range(TOPK_GROUPS):
        m = jnp.max(cur, axis=0, keepdims=True)
        f = jnp.min(jnp.where(cur == m, gi, N_EXPERT_GROUPS), axis=0, keepdims=True)
        hit = gi == f
        chosen = chosen | hit
        cur = jnp.where(hit, ninf, cur)
    emask = jnp.concatenate(
        [jnp.broadcast_to(chosen[g:g + 1, :], (per_group, tm)) for g in range(N_EXPERT_GROUPS)], axis=0)

    ei = lax.broadcasted_iota(jnp.int32, (n_exp, tm), 0)
    cand = jnp.where(emask, biased, ninf)
    hits, ids, wts = [], [], []
    for _ in range(TOP_K):
        m = jnp.max(cand, axis=0, keepdims=True)
        f = jnp.min(jnp.where(cand == m, ei, n_exp), axis=0, keepdims=True)
        hit = ei == f
        wts.append(jnp.sum(jnp.where(hit, scores, 0.0), axis=0, keepdims=True))
        cand = jnp.where(hit, ninf, cand)
        hits.append(hit)
        ids.append(f)
    onehot = hits[0]
    for hit in hits[1:]:
        onehot = onehot | hit
    onehot = jnp.where(onehot, 1.0, 0.0)

    @pl.when(pl.program_id(0) == 0)
    def _():
        cnt_ref[...] = jnp.zeros(cnt_ref.shape, F32)

    before = jnp.dot(onehot.astype(BF16), tri_ref[...], preferred_element_type=F32) + cnt_ref[...]
    ranks = [jnp.sum(jnp.where(hit, before, 0.0), axis=0, keepdims=True) for hit in hits]
    cnt_ref[...] += jnp.sum(onehot, axis=1, keepdims=True)

    denom = wts[0]
    for wk in wts[1:]:
        denom = denom + wk
    e_ref[...] = jnp.concatenate(ids, axis=0)
    w_ref[...] = jnp.concatenate([ROUTED_SCALE * wk / denom for wk in wts], axis=0)
    r_ref[...] = jnp.concatenate(ranks, axis=0).astype(jnp.int32)


def _post(alpha, oa, ga, gbyb, xf, mod, wpa, wo, l1g, l1b, wrt, rb, wgs, wus, wds, seq):
    t, d = xf.shape
    aw = oa.shape[1]
    n_exp = wrt.shape[0]
    tm = TM_POST
    per_seq = seq // tm
    row = lambda i: (i, 0)
    col = lambda i: (0, i)
    tri = (jnp.arange(tm)[:, None] < jnp.arange(tm)[None, :]).astype(BF16)
    return pl.pallas_call(
        functools.partial(_post_kernel, alpha),
        grid=(t // tm,),
        in_specs=[pl.BlockSpec((tm, aw), row), pl.BlockSpec((tm, d), row), pl.BlockSpec((tm, d), row),
                  pl.BlockSpec((tm, d), row),
                  pl.BlockSpec((1,) + mod.shape[1:], lambda i: (i // per_seq, 0, 0)),
                  _const_spec(wpa.shape), _const_spec(wo.shape), _const_spec(l1g.shape),
                  _const_spec(l1b.shape), _const_spec(wrt.shape), _const_spec(rb.shape),
                  _const_spec(wgs.shape), _const_spec(wus.shape), _const_spec(wds.shape),
                  _const_spec(tri.shape)],
        out_specs=[pl.BlockSpec((tm * ROW_TILE, LANES), row), pl.BlockSpec((tm, d), row),
                   pl.BlockSpec((TOP_K, tm), col), pl.BlockSpec((TOP_K, tm), col),
                   pl.BlockSpec((TOP_K, tm), col), _const_spec((n_exp, 1))],
        out_shape=[jax.ShapeDtypeStruct((t * ROW_TILE, LANES), F32), jax.ShapeDtypeStruct((t, d), F32),
                   jax.ShapeDtypeStruct((TOP_K, t), jnp.int32), jax.ShapeDtypeStruct((TOP_K, t), F32),
                   jax.ShapeDtypeStruct((TOP_K, t), jnp.int32), jax.ShapeDtypeStruct((n_exp, 1), F32)],
        compiler_params=_cparams(("arbitrary",)),
        name="post",
    )(oa, ga, gbyb, xf, mod, wpa, wo, l1g, l1b, wrt, rb, wgs, wus, wds, tri)


def _row_copy(src_ref, src_row, dst_ref, dst_row, sem):
    return pltpu.make_async_copy(
        src_ref.at[pl.ds(pl.multiple_of(src_row * ROW_TILE, ROW_TILE), ROW_TILE), :],
        dst_ref.at[pl.ds(pl.multiple_of(dst_row * ROW_TILE, ROW_TILE), ROW_TILE), :], sem)


def _dest_kernel(rs_ref, e_ref, r_ref, o_ref):
    e = e_ref[...]
    start = lax.fori_loop(0, rs_ref.shape[0], lambda x, acc: jnp.where(e == x, rs_ref[x], acc),
                          jnp.zeros(e.shape, jnp.int32))
    o_ref[...] = start + r_ref[...]


def _dest(row_start, e_idx, rank):
    t = e_idx.shape[1]
    tm = min(t, 2048)
    blk = pl.BlockSpec((TOP_K, tm), lambda i, *_: (0, i))
    return pl.pallas_call(
        _dest_kernel,
        grid_spec=pltpu.PrefetchScalarGridSpec(num_scalar_prefetch=1, grid=(t // tm,), in_specs=[blk, blk],
                                               out_specs=blk),
        out_shape=jax.ShapeDtypeStruct((TOP_K, t), jnp.int32),
        compiler_params=_cparams(("arbitrary",)),
        name="dest",
    )(row_start, e_idx, rank)


def _dispatch_kernel(n_blocks, ps_ref, pl_ref, nu_ref, dest_ref, h2_ref, xs_ref, zbuf, sem, zsem):
    i = pl.program_id(0)
    n_steps = pl.num_programs(0)
    tm = dest_ref.shape[1]
    n_exp = ps_ref.shape[0]
    per_step = -(-n_exp // n_steps)
    zbuf[...] = jnp.zeros(zbuf.shape, F32)

    def zero_fill(act):
        for q in range(per_step):
            e = jnp.minimum(i * per_step + q, n_exp - 1)
            live = i * per_step + q < n_exp
            gap = pl_ref[e]
            for bit in [1 << s for s in range(BM.bit_length() - 1)]:
                @pl.when(live & ((gap & bit) != 0))
                def _(bit=bit, e=e, gap=gap):
                    off = ps_ref[e] + (gap & ~(2 * bit - 1))
                    act(pltpu.make_async_copy(
                        zbuf.at[pl.ds(0, bit * ROW_TILE), :],
                        xs_ref.at[pl.ds(pl.multiple_of(off * ROW_TILE, ROW_TILE), bit * ROW_TILE), :], zsem))
            blk = nu_ref[0] + i * per_step + q
            @pl.when(blk < n_blocks)
            def _(blk=blk):
                rows = BM * ROW_TILE
                act(pltpu.make_async_copy(
                    zbuf, xs_ref.at[pl.ds(pl.multiple_of(blk * rows, rows), rows), :], zsem))

    zero_fill(lambda cp: cp.start(priority=1))

    def issue(j, c):
        for k in range(TOP_K):
            _row_copy(h2_ref, j, xs_ref, dest_ref[k, j], sem).start(priority=k % 2)
        return c

    lax.fori_loop(0, tm, issue, 0)
    n = tm * TOP_K * ROW_TILE
    pltpu.make_async_copy(xs_ref.at[pl.ds(0, n), :], xs_ref.at[pl.ds(0, n), :], sem).wait()
    zero_fill(lambda cp: cp.wait())


def _dispatch(pad_start, pad_len, n_used, dest, h2t, n_blocks):
    t = dest.shape[1]
    tm = TM_DISPATCH
    grid_spec = pltpu.PrefetchScalarGridSpec(
        num_scalar_prefetch=3,
        grid=(t // tm,),
        in_specs=[pl.BlockSpec((TOP_K, tm), lambda i, *_: (0, i), memory_space=pltpu.SMEM),
                  pl.BlockSpec((tm * ROW_TILE, LANES), lambda i, *_: (i, 0))],
        out_specs=pl.BlockSpec(memory_space=pl.ANY),
        scratch_shapes=[pltpu.VMEM((BM * ROW_TILE, LANES), F32), pltpu.SemaphoreType.DMA,
                        pltpu.SemaphoreType.DMA],
    )
    return pl.pallas_call(
        functools.partial(_dispatch_kernel, n_blocks),
        grid_spec=grid_spec,
        out_shape=jax.ShapeDtypeStruct((n_blocks * BM * ROW_TILE, LANES), F32),
        compiler_params=_cparams(("arbitrary",)),
        name="dispatch",
    )(pad_start, pad_len, n_used, dest, h2t)


def _expert_kernel(nblk_ref, bstart_ref, widx_ref, nu_ref, nv_ref, xs_ref, wg_ref, wu_ref, wd_ref, ys_ref,
                   xbuf, ybuf, wg_b, wu_b, wd_b, xsem, ysem):
    e = pl.program_id(0)
    n_used = nu_ref[0]
    rows = BM * ROW_TILE
    piece = rows // EXPERT_CHUNKS
    ring = EXPERT_RING

    def slot_base(g):
        return pl.multiple_of((g % ring) * rows, rows)

    def pieces(g, hbm_ref, buf, sem, to_hbm, act):
        for c in range(EXPERT_CHUNKS):
            @pl.when(c * (BM // EXPERT_CHUNKS) < nv_ref[g])
            def _(c=c):
                in_hbm = hbm_ref.at[pl.ds(pl.multiple_of(g * rows + c * piece, piece), piece), :]
                in_vmem = buf.at[pl.ds(pl.multiple_of(slot_base(g) + c * piece, piece), piece), :]
                src, dst = (in_vmem, in_hbm) if to_hbm else (in_hbm, in_vmem)
                act(pltpu.make_async_copy(src, dst, sem.at[g % ring]))

    def x_load(g, act):
        pieces(g, xs_ref, xbuf, xsem, False, act)

    def y_store(g, act):
        pieces(g, ys_ref, ybuf, ysem, True, act)

    start = lambda cp: cp.start()
    start_store = lambda cp: cp.start(priority=1)
    wait = lambda cp: cp.wait()

    @pl.when(e == 0)
    def _():
        xbuf[...] = jnp.zeros(xbuf.shape, F32)
        for g0 in range(ring - 1):
            @pl.when(g0 < n_used)
            def _(g0=g0):
                x_load(g0, start)

    wg_b[...] = wg_ref[0].astype(BF16)
    wu_b[...] = wu_ref[0].astype(BF16)
    wd_b[...] = wd_ref[0].astype(BF16)

    def block(b, c):
        g = bstart_ref[e] + b
        x_load(g, wait)

        @pl.when(g + ring - 1 < n_used)
        def _():
            x_load(g + ring - 1, start)

        @pl.when(g >= ring)
        def _():
            y_store(g - ring, wait)

        x = _load_token_rows(xbuf, slot_base(g), BM).astype(BF16)
        a = jnp.dot(x, wg_b[...], preferred_element_type=F32)
        u = jnp.dot(x, wu_b[...], preferred_element_type=F32)
        act = (a * _sigmoid(a) * u).astype(BF16)
        _store_token_rows(ybuf, slot_base(g), jnp.dot(act, wd_b[...], preferred_element_type=F32))
        y_store(g, start_store)
        return c

    lax.fori_loop(0, nblk_ref[e], block, 0)

    @pl.when(e == pl.num_programs(0) - 1)
    def _():
        for back in range(ring, 0, -1):
            @pl.when(n_used >= back)
            def _(back=back):
                y_store(n_used - back, wait)


def _experts(nblk, blk_start, w_idx, n_used, block_nv, xs, wg, wu, wd):
    n_exp, d, de = wg.shape
    rows = BM * ROW_TILE
    wsel = lambda e, nb, bs, wi, nu, nv: (wi[e], 0, 0)
    grid_spec = pltpu.PrefetchScalarGridSpec(
        num_scalar_prefetch=5,
        grid=(n_exp,),
        in_specs=[pl.BlockSpec(memory_space=pl.ANY),
                  pl.BlockSpec((1, d, de), wsel), pl.BlockSpec((1, d, de), wsel),
                  pl.BlockSpec((1, de, d), wsel)],
        out_specs=pl.BlockSpec(memory_space=pl.ANY),
        scratch_shapes=[pltpu.VMEM((EXPERT_RING * rows, LANES), F32), pltpu.VMEM((EXPERT_RING * rows, LANES), F32),
                        pltpu.VMEM((d, de), BF16), pltpu.VMEM((d, de), BF16), pltpu.VMEM((de, d), BF16),
                        pltpu.SemaphoreType.DMA((EXPERT_RING,)), pltpu.SemaphoreType.DMA((EXPERT_RING,))],
    )
    return pl.pallas_call(
        _expert_kernel,
        grid_spec=grid_spec,
        out_shape=jax.ShapeDtypeStruct(xs.shape, xs.dtype),
        input_output_aliases={5: 0},
        compiler_params=_cparams(("arbitrary",)),
        name="experts",
    )(nblk, blk_start, w_idx, n_used, block_nv, xs, wg, wu, wd)


def _combine_kernel(dest_ref, next_dest_ref, w_ref, base_ref, mod_ref, g_ref, b_ref, ys_ref, o_ref, buf, sems):
    i = pl.program_id(0)
    tm = dest_ref.shape[1]
    slab = tm * ROW_TILE
    slot_rows = TOP_K * slab

    def gather(idx_ref, slot):
        def issue(j, c):
            for k in range(TOP_K):
                _row_copy(ys_ref, idx_ref[k, j], buf, slot * (TOP_K * tm) + k * tm + j,
                          sems.at[slot]).start(priority=k % 2)
            return c
        lax.fori_loop(0, tm, issue, 0)

    @pl.when(i == 0)
    def _():
        gather(dest_ref, 0)

    @pl.when(i + 1 < pl.num_programs(0))
    def _():
        gather(next_dest_ref, (i + 1) % 2)

    slot = i % 2
    base_row = pl.multiple_of(slot * slot_rows, slot_rows)
    pltpu.make_async_copy(ys_ref.at[pl.ds(0, slot_rows), :], buf.at[pl.ds(base_row, slot_rows), :],
                          sems.at[slot]).wait()

    wpad = jnp.concatenate([w_ref[...], jnp.zeros((LANES - TOP_K, tm), F32)], axis=0)
    wcol = wpad.T
    routed = jnp.zeros(base_ref.shape, F32)
    for k in range(TOP_K):
        routed = routed + wcol[:, k:k + 1] * _load_token_rows(buf, base_row + k * slab, tm)
    gate_f = mod_ref[0, 5:6, :]
    o_ref[...] = _layer_norm(base_ref[...] + gate_f * routed, g_ref[...], b_ref[...])


def _combine(dest, wts, base, mod, g, b, ys, seq):
    t, d = base.shape
    tm = TM_COMBINE
    per_seq = seq // tm
    n_steps = t // tm
    col = lambda i: (0, i)
    row = lambda i: (i, 0)
    return pl.pallas_call(
        _combine_kernel,
        grid=(n_steps,),
        in_specs=[pl.BlockSpec((TOP_K, tm), col, memory_space=pltpu.SMEM),
                  pl.BlockSpec((TOP_K, tm), lambda i: (0, jnp.minimum(i + 1, n_steps - 1)),
                               memory_space=pltpu.SMEM),
                  pl.BlockSpec((TOP_K, tm), col), pl.BlockSpec((tm, d), row),
                  pl.BlockSpec((1,) + mod.shape[1:], lambda i: (i // per_seq, 0, 0)),
                  _const_spec(g.shape), _const_spec(b.shape),
                  pl.BlockSpec(memory_space=pl.ANY)],
        out_specs=pl.BlockSpec((tm, d), row),
        out_shape=jax.ShapeDtypeStruct((t, d), F32),
        scratch_shapes=[pltpu.VMEM((2 * TOP_K * tm * ROW_TILE, LANES), F32), pltpu.SemaphoreType.DMA((2,))],
        compiler_params=_cparams(("arbitrary",)),
        name="combine",
    )(dest, dest, wts, base, mod, g, b, ys)


def _rope_tables(seq):
    half = HEAD_DIM // 2
    inv_freq = ROPE_THETA ** (-jnp.arange(half, dtype=F32) / half)
    ang = jnp.arange(seq, dtype=F32)[:, None] * inv_freq[None, :]
    cos = jnp.concatenate([jnp.cos(ang), jnp.cos(ang)], axis=1)
    sin = jnp.concatenate([-jnp.sin(ang), jnp.sin(ang)], axis=1)
    return jnp.tile(cos, (1, N_HEADS)), jnp.tile(sin, (1, N_HEADS))


def _layer(x, c, w_ada, b_ada, w_in, ln_v_g, ln_v_b, w_spatial, b_spatial, w_proj_a, w_proj_b, w_out,
           ln1_g, ln1_b, w_router, router_bias, w_gate_e, w_up_e, w_down_e, w_gate_sh, w_up_sh,
           w_down_sh, ln2_g, ln2_b, alpha):
    bsz, seq, d = x.shape
    assert seq % (max(DILATIONS) * BAND) == 0 and seq % TM_IN == 0 and d == ROW_TILE * LANES
    t = bsz * seq
    aw = N_HEADS * HEAD_DIM
    gwid = ln_v_g.shape[0]
    n_exp = w_router.shape[1]
    xf = x.reshape(t, d)

    mod = _ada(c, w_ada, b_ada).reshape(bsz, 6, d)

    cuts = [0, aw, 2 * aw, 3 * aw, 3 * aw + gwid, 3 * aw + 2 * gwid, 3 * aw + 2 * gwid + d,
            3 * aw + 2 * gwid + 2 * d]
    wq, wk, wv, wu, wvg, wga, wgb = [w_in[:, a:b].astype(BF16) for a, b in zip(cuts[:-1], cuts[1:])]
    cos, sin = _rope_tables(seq)
    bsp = jnp.repeat(b_spatial.T, gwid // GMLP_GROUPS, axis=1)
    qkv, ga, gbyb = _inproj(xf, mod, cos, sin, wq, wk, wv, wu, wvg, wga, wgb,
                            ln_v_g.reshape(1, gwid), ln_v_b.reshape(1, gwid), w_spatial, bsp,
                            w_proj_b.astype(BF16), seq)

    oa = _attention(qkv, bsz, seq)

    h2t, base, e_idx, wts, rank, cnt = _post(
        alpha, oa.reshape(t, aw), ga, gbyb, xf, mod, w_proj_a.astype(BF16), w_out.astype(BF16),
        ln1_g.reshape(1, d), ln1_b.reshape(1, d), w_router.T.astype(BF16), router_bias.reshape(n_exp, 1),
        w_gate_sh.astype(BF16), w_up_sh.astype(BF16), w_down_sh.astype(BF16), seq)

    counts = cnt[:, 0].astype(jnp.int32)
    nblk = ((counts + BM - 1) // BM).astype(jnp.int32)
    blk_end = jnp.cumsum(nblk).astype(jnp.int32)
    blk_start = blk_end - nblk
    n_blocks = (t * TOP_K) // BM + n_exp
    n_used = blk_end[-1:]
    row_start = blk_start * BM
    ids = jnp.arange(n_exp, dtype=jnp.int32)
    seen = jnp.where((ids[None, :] <= ids[:, None]) & (nblk[None, :] > 0), ids[None, :], -1).max(axis=1)
    w_idx = jnp.where(seen >= 0, seen, jnp.where(nblk > 0, ids, n_exp - 1).min()).astype(jnp.int32)

    bidx = jnp.arange(n_blocks, dtype=jnp.int32)[:, None]
    owner = (bidx >= blk_start[None, :]) & (bidx < blk_end[None, :])
    block_nv = jnp.where(owner, jnp.minimum(counts[None, :] - (bidx - blk_start[None, :]) * BM, BM), 0).sum(axis=1)

    dest = _dest(row_start, e_idx, rank)
    xs = _dispatch(row_start + counts, nblk * BM - counts, n_used, dest, h2t, n_blocks)
    ys = _experts(nblk, blk_start, w_idx, n_used, block_nv.astype(jnp.int32), xs, w_gate_e, w_up_e, w_down_e)
    out = _combine(dest, wts, base, mod, ln2_g.reshape(1, d), ln2_b.reshape(1, d), ys, seq)
    return out.reshape(bsz, seq, d)


def kernel(x, c, w_ada, b_ada, w_in, ln_v_g, ln_v_b, w_spatial, b_spatial, w_proj_a, w_proj_b, w_out,
           ln1_g, ln1_b, w_router, router_bias, w_gate_e, w_up_e, w_down_e, w_gate_sh, w_up_sh,
           w_down_sh, ln2_g, ln2_b):
    depth = w_ada.shape[0]
    alpha = (2.0 * depth) ** 0.25
    for l in range(depth):
        x = _layer(x, c, w_ada[l], b_ada[l], w_in[l], ln_v_g[l], ln_v_b[l], w_spatial[l], b_spatial[l],
                   w_proj_a[l], w_proj_b[l], w_out[l], ln1_g[l], ln1_b[l], w_router[l], router_bias[l],
                   w_gate_e[l], w_up_e[l], w_down_e[l], w_gate_sh[l], w_up_sh[l], w_down_sh[l],
                   ln2_g[l], ln2_b[l], alpha)
    return x
```

```python
import functools
import math

import jax
import jax.numpy as jnp
from jax import lax
from jax.experimental import pallas as pl
from jax.experimental.pallas import tpu as pltpu

F32 = jnp.float32
BF16 = jnp.bfloat16

N_HEADS = 8
HEAD_DIM = 64
DILATIONS = (1, 4, 16)
BAND = 128
ROPE_THETA = 10000.0
GMLP_GROUPS = 8
GMLP_CHUNK = 128
N_EXPERT_GROUPS = 8
TOPK_GROUPS = 4
TOP_K = 8
ROUTED_SCALE = 2.5
LN_EPS = 1e-5
LANES = 128
SUBLANES = 8
ROW_TILE = SUBLANES
VMEM_LIMIT = 56 * 1024 * 1024

TM_IN = 512
TM_POST = 512
TM_DISPATCH = 1024
TM_COMBINE = 256
BM = 256
EXPERT_RING = 4
EXPERT_CHUNKS = 4
ATTN_BLOCKS_PER_TRIP = 16


def _cparams(sem):
    return pltpu.CompilerParams(dimension_semantics=sem, vmem_limit_bytes=VMEM_LIMIT)


def _const_spec(shape):
    nd = len(shape)
    return pl.BlockSpec(shape, lambda *_: (0,) * nd)


def _layer_norm(r, g, b):
    mu = jnp.mean(r, axis=-1, keepdims=True)
    d = r - mu
    var = jnp.mean(d * d, axis=-1, keepdims=True)
    return d * lax.rsqrt(var + LN_EPS) * g + b


def _gelu(t):
    return 0.5 * t * (1.0 + lax.erf(t * (1.0 / math.sqrt(2.0))))


def _sigmoid(t):
    return 1.0 / (1.0 + jnp.exp(-t))


def _store_token_rows(ref, first_row, v):
    m = v.shape[0]
    for s in range(ROW_TILE):
        ref[pl.ds(first_row + s, m, stride=ROW_TILE), :] = v[:, s * LANES:(s + 1) * LANES]


def _load_token_rows(ref, first_row, m):
    return jnp.concatenate([ref[pl.ds(first_row + s, m, stride=ROW_TILE), :] for s in range(ROW_TILE)], axis=1)


def _ada_kernel(c_ref, w_ref, b_ref, o_ref):
    c = c_ref[...]
    o_ref[...] = jnp.dot(c * _sigmoid(c), w_ref[...], preferred_element_type=F32) + b_ref[...]


def _ada(c, w, b):
    bsz, d = c.shape
    n = w.shape[1]
    tn = 1024
    return pl.pallas_call(
        _ada_kernel,
        grid=(n // tn,),
        in_specs=[_const_spec((bsz, d)), pl.BlockSpec((d, tn), lambda j: (0, j)),
                  pl.BlockSpec((1, tn), lambda j: (0, j))],
        out_specs=pl.BlockSpec((bsz, tn), lambda j: (0, j)),
        out_shape=jax.ShapeDtypeStruct((bsz, n), F32),
        compiler_params=_cparams(("arbitrary",)),
        name="ada",
    )(c, w, b.reshape(1, n))


def _inproj_kernel(x_ref, mod_ref, cos_ref, sin_ref, wq_ref, wk_ref, wv_ref, wu_ref, wvg_ref,
                   wga_ref, wgb_ref, lng_ref, lnb_ref, wsp_ref, bsp_ref, wpb_ref, *rest):
    n_cm = len(DILATIONS) - 1
    q_refs, k_refs, v_refs = [rest[(1 + n_cm) * a:(1 + n_cm) * (a + 1)] for a in range(3)]
    ga_ref, gbyb_ref, u_scr, vv_scr, gated_scr, q_stage, k_stage, v_stage = rest[3 * (1 + n_cm):]
    tm = x_ref.shape[0]

    def emit(val, refs, stage):
        refs[0][...] = val.astype(BF16)
        for hp in range(val.shape[1] // LANES):
            stage[pl.ds(hp * tm, tm), :] = val[:, hp * LANES:(hp + 1) * LANES]
        for ref, dil in zip(refs[1:], DILATIONS[1:]):
            for hp in range(val.shape[1] // LANES):
                for r in range(dil):
                    ref[0, hp, r, :, :] = stage[pl.ds(hp * tm + r, tm // dil, stride=dil), :].astype(BF16)
    shift = mod_ref[0, 0:1, :]
    scale = mod_ref[0, 1:2, :]
    h = (x_ref[...] * (1.0 + scale) + shift).astype(BF16)

    cos = cos_ref[...]
    sin = sin_ref[...]
    aw = cos.shape[1]
    lane = lax.broadcasted_iota(jnp.int32, (tm, aw), 1)
    first_half = (lane % HEAD_DIM) < (HEAD_DIM // 2)

    def rope(t):
        partner = jnp.where(first_half, pltpu.roll(t, aw - HEAD_DIM // 2, 1),
                            pltpu.roll(t, HEAD_DIM // 2, 1))
        return t * cos + partner * sin

    emit(rope(jnp.dot(h, wq_ref[...], preferred_element_type=F32)) * (HEAD_DIM ** -0.5), q_refs, q_stage)
    emit(rope(jnp.dot(h, wk_ref[...], preferred_element_type=F32)), k_refs, k_stage)
    emit(jnp.dot(h, wv_ref[...], preferred_element_type=F32), v_refs, v_stage)

    u_scr[...] = _gelu(jnp.dot(h, wu_ref[...], preferred_element_type=F32))
    vg = _gelu(jnp.dot(h, wvg_ref[...], preferred_element_type=F32))
    vv_scr[...] = _layer_norm(vg, lng_ref[...], lnb_ref[...]).astype(BF16)

    ti = lax.broadcasted_iota(jnp.int32, (GMLP_CHUNK, GMLP_CHUNK), 0)
    si = lax.broadcasted_iota(jnp.int32, (GMLP_CHUNK, GMLP_CHUNK), 1)
    causal = si <= ti
    gw = u_scr.shape[1] // GMLP_GROUPS
    for g in range(GMLP_GROUPS):
        ws = jnp.where(causal, wsp_ref[g], 0.0).astype(BF16)
        cols = slice(g * gw, (g + 1) * gw)
        for c in range(tm // GMLP_CHUNK):
            rows = slice(c * GMLP_CHUNK, (c + 1) * GMLP_CHUNK)
            sv = jnp.dot(ws, vv_scr[rows, cols], preferred_element_type=F32) + bsp_ref[:, cols]
            gated_scr[rows, cols] = (u_scr[rows, cols] * sv).astype(BF16)

    y_b = jnp.dot(gated_scr[...], wpb_ref[...], preferred_element_type=F32)
    g_b = _sigmoid(jnp.dot(h, wgb_ref[...], preferred_element_type=F32))
    gbyb_ref[...] = (g_b * y_b).astype(BF16)
    ga_ref[...] = _sigmoid(jnp.dot(h, wga_ref[...], preferred_element_type=F32)).astype(BF16)


def _inproj(xf, mod, cos, sin, wq, wk, wv, wu, wvg, wga, wgb, lng, lnb, wsp, bsp, wpb, seq):
    t, d = xf.shape
    aw = wq.shape[1]
    gwid = wu.shape[1]
    tm = TM_IN
    per_seq = seq // tm
    row = lambda i: (i, 0)
    tab = lambda i: (i % per_seq, 0)
    bsz = t // seq
    qkv_specs = [pl.BlockSpec((tm, aw), row)] + [
        pl.BlockSpec((1, aw // LANES, dil, tm // dil, LANES), lambda i: (i // per_seq, 0, 0, i % per_seq, 0))
        for dil in DILATIONS[1:]]
    qkv_shapes = [jax.ShapeDtypeStruct((t, aw), BF16)] + [
        jax.ShapeDtypeStruct((bsz, aw // LANES, dil, seq // dil, LANES), BF16) for dil in DILATIONS[1:]]
    outs = pl.pallas_call(
        _inproj_kernel,
        grid=(t // tm,),
        in_specs=[pl.BlockSpec((tm, d), row),
                  pl.BlockSpec((1,) + mod.shape[1:], lambda i: (i // per_seq, 0, 0)),
                  pl.BlockSpec((tm, aw), tab), pl.BlockSpec((tm, aw), tab),
                  _const_spec(wq.shape), _const_spec(wk.shape), _const_spec(wv.shape),
                  _const_spec(wu.shape), _const_spec(wvg.shape), _const_spec(wga.shape),
                  _const_spec(wgb.shape), _const_spec(lng.shape), _const_spec(lnb.shape),
                  _const_spec(wsp.shape), _const_spec(bsp.shape), _const_spec(wpb.shape)],
        out_specs=qkv_specs * 3 + [pl.BlockSpec((tm, d), row)] * 2,
        out_shape=qkv_shapes * 3 + [jax.ShapeDtypeStruct((t, d), BF16)] * 2,
        scratch_shapes=[pltpu.VMEM((tm, gwid), F32), pltpu.VMEM((tm, gwid), BF16),
                        pltpu.VMEM((tm, gwid), BF16)] + [pltpu.VMEM((tm * aw // LANES, LANES), F32)] * 3,
        compiler_params=_cparams(("arbitrary",)),
        name="inproj",
    )(xf, mod, cos, sin, wq, wk, wv, wu, wvg, wga, wgb, lng, lnb, wsp, bsp, wpb)
    n_lay = len(DILATIONS)
    return [outs[a * n_lay:(a + 1) * n_lay] for a in range(3)], outs[3 * n_lay], outs[3 * n_lay + 1]


def _rows(start, size, stride):
    if stride == 1:
        return pl.ds(start, size)
    return pl.ds(start, size, stride=stride)


def _band_aligned(start):
    return start if isinstance(start, int) else pl.multiple_of(start, BAND)


def _interleaved_loop(count, load, compute, store):
    per_trip = max(u for u in range(1, ATTN_BLOCKS_PER_TRIP + 1) if count % u == 0) if count else 1

    def trip(it, c):
        items = [it * per_trip + u for u in range(per_trip)]
        results = [compute(ops) for ops in [load(i) for i in items]]
        for i, res in zip(items, results):
            store(i, res)
        return c

    if count == per_trip:
        trip(0, 0)
    elif count:
        lax.fori_loop(0, count // per_trip, trip, 0)


def _attn_kernel(*refs):
    npat = len(DILATIONS)
    qkv = [refs[3 * p:3 * p + 3] for p in range(npat)]
    o_ref, acc_scr, max_scr, sum_scr = refs[3 * npat:]
    seq = o_ref.shape[1]

    lane = lax.broadcasted_iota(jnp.int32, (BAND, LANES), 1)
    head0 = lane < HEAD_DIM
    neg = jnp.float32(-1e30)

    qi = lax.broadcasted_iota(jnp.int32, (BAND, 2 * BAND), 0)
    kj = lax.broadcasted_iota(jnp.int32, (BAND, 2 * BAND), 1)

    def load(p, blk, nb):
        q_ref, k_ref, v_ref = qkv[p]
        q_start = blk * BAND
        k_start = jnp.maximum(q_start - BAND, 0)
        off = q_start - k_start
        lowest = jnp.where(blk % nb == 0, off, qi)
        mask = (kj >= lowest) & (kj <= qi + off)
        krows = pl.ds(_band_aligned(k_start), 2 * BAND)
        return (q_ref[0, 0, pl.ds(_band_aligned(q_start), BAND), :], k_ref[0, 0, krows, :],
                v_ref[0, 0, krows, :], mask)

    def attend(ops):
        qb, kb, vb, mask = ops
        zero = jnp.zeros(qb.shape, qb.dtype)
        parts = []
        for hd in range(2):
            qh = jnp.where(head0 if hd == 0 else ~head0, qb, zero)
            s = lax.dot_general(qh, kb, (((1,), (1,)), ((), ())), preferred_element_type=F32)
            s = jnp.where(mask, s, neg)
            m = jnp.max(s, axis=-1, keepdims=True)
            pr = jnp.exp(s - m)
            l = jnp.sum(pr, axis=-1, keepdims=True)
            acc = jnp.dot(pr.astype(BF16), vb, preferred_element_type=F32)
            parts.append((acc, m, l))
        return [jnp.where(head0, a0, a1) for a0, a1 in zip(*parts)]

    def store(p, seq_start, stride, res):
        rows = _rows(seq_start if stride > 1 else _band_aligned(seq_start), BAND, stride)
        acc_scr[p, rows, :] = res[0]
        max_scr[p, rows, :] = res[1]
        sum_scr[p, rows, :] = res[2]

    for p, dil in enumerate(DILATIONS):
        n = seq // dil
        nb = n // BAND

        _interleaved_loop(
            seq // BAND, lambda blk, p=p, nb=nb: load(p, blk, nb), attend,
            lambda blk, res, p=p, dil=dil, nb=nb: store(p, blk // nb + dil * BAND * (blk % nb), dil, res))

    top = jnp.maximum(jnp.maximum(max_scr[0], max_scr[1]), max_scr[2])
    num = jnp.zeros(top.shape, F32)
    den = jnp.zeros(top.shape, F32)
    for p in range(npat):
        w = jnp.exp(max_scr[p] - top)
        num = num + w * acc_scr[p]
        den = den + w * sum_scr[p]
    o_ref[0] = (num / den).astype(BF16)


def _attention(qkv, bsz, seq):
    aw = qkv[0][0].shape[1]
    npat = len(DILATIONS)
    blk = (1, 1, seq, LANES)
    operands, specs = [], []
    for p, dil in enumerate(DILATIONS):
        for layouts in qkv:
            if dil == 1:
                operands.append(layouts[p].reshape(bsz, 1, seq, aw))
                specs.append(pl.BlockSpec(blk, lambda b, hp: (b, 0, 0, hp)))
            else:
                operands.append(layouts[p].reshape(bsz, aw // LANES, seq, LANES))
                specs.append(pl.BlockSpec(blk, lambda b, hp: (b, hp, 0, 0)))
    return pl.pallas_call(
        _attn_kernel,
        grid=(bsz, aw // LANES),
        in_specs=specs,
        out_specs=pl.BlockSpec((1, seq, LANES), lambda b, hp: (b, 0, hp)),
        out_shape=jax.ShapeDtypeStruct((bsz, seq, aw), BF16),
        scratch_shapes=[pltpu.VMEM((npat, seq, LANES), F32)] * 3,
        compiler_params=_cparams(("arbitrary", "arbitrary")),
        name="attn",
    )(*operands)


def _post_kernel(alpha, oa_ref, ga_ref, gbyb_ref, x_ref, mod_ref, wpa_ref, wo_ref, l1g_ref, l1b_ref,
                 wrt_ref, rb_ref, wgs_ref, wus_ref, wds_ref, tri_ref,
                 h2_ref, base_ref, e_ref, w_ref, r_ref, cnt_ref):
    tm = x_ref.shape[0]
    n_exp = wrt_ref.shape[0]
    per_group = n_exp // N_EXPERT_GROUPS
    gate_a = mod_ref[0, 2:3, :]
    shift_f = mod_ref[0, 3:4, :]
    scale_f = mod_ref[0, 4:5, :]
    gate_f = mod_ref[0, 5:6, :]

    y_a = jnp.dot(oa_ref[...], wpa_ref[...], preferred_element_type=F32)
    merged = ga_ref[...].astype(F32) * y_a + gbyb_ref[...].astype(F32)
    mix = jnp.dot(merged.astype(BF16), wo_ref[...], preferred_element_type=F32)
    x1 = _layer_norm(alpha * x_ref[...] + gate_a * mix, l1g_ref[...], l1b_ref[...])
    h2 = x1 * (1.0 + scale_f) + shift_f
    h2b = h2.astype(BF16)

    _store_token_rows(h2_ref, 0, h2)

    a = jnp.dot(h2b, wgs_ref[...], preferred_element_type=F32)
    b = jnp.dot(h2b, wus_ref[...], preferred_element_type=F32)
    shared = jnp.dot((a * _sigmoid(a) * b).astype(BF16), wds_ref[...], preferred_element_type=F32)
    base_ref[...] = alpha * x1 + gate_f * shared

    logits = lax.dot_general(wrt_ref[...], h2b, (((1,), (1,)), ((), ())), preferred_element_type=F32)
    scores = _sigmoid(logits)
    biased = scores + rb_ref[...]
    ninf = jnp.float32(-jnp.inf)

    sub = lax.broadcasted_iota(jnp.int32, (per_group, tm), 0)
    grp_rows = []
    for g in range(N_EXPERT_GROUPS):
        blk = biased[g * per_group:(g + 1) * per_group, :]
        m1 = jnp.max(blk, axis=0, keepdims=True)
        f1 = jnp.min(jnp.where(blk == m1, sub, per_group), axis=0, keepdims=True)
        m2 = jnp.max(jnp.where(sub == f1, ninf, blk), axis=0, keepdims=True)
        grp_rows.append(m1 + m2)
    grp = jnp.concatenate(grp_rows, axis=0)

    gi = lax.broadcasted_iota(jnp.int32, grp.shape, 0)
    chosen = jnp.zeros(grp.shape, jnp.bool_)
    cur = grp
    for _ in range(TOPK_GROUPS):
        m = jnp.max(cur, axis=0, keepdims=True)
        f = jnp.min(jnp.where(cur == m, gi, N_EXPERT_GROUPS), axis=0, keepdims=True)
        hit = gi == f
        chosen = chosen | hit
        cur = jnp.where(hit, ninf, cur)
    emask = jnp.concatenate(
        [jnp.broadcast_to(chosen[g:g + 1, :], (per_group, tm)) for g in range(N_EXPERT_GROUPS)], axis=0)

    ei = lax.broadcasted_iota(jnp.int32, (n_exp, tm), 0)
    cand = jnp.where(emask, biased, ninf)
    hits, ids, wts = [], [], []
    for _ in range(TOP_K):
        m = jnp.max(cand, axis=0, keepdims=True)
        f = jnp.min(jnp.where(cand == m, ei, n_exp), axis=0, keepdims=True)
        hit = ei == f
        wts.append(jnp.sum(jnp.where(hit, scores, 0.0), axis=0, keepdims=True))
        cand = jnp.where(hit, ninf, cand)
        hits.append(hit)
        ids.append(f)
    onehot = hits[0]
    for hit in hits[1:]:
        onehot = onehot | hit
    onehot = jnp.where(onehot, 1.0, 0.0)

    @pl.when(pl.program_id(0) == 0)
    def _():
        cnt_ref[...] = jnp.zeros(cnt_ref.shape, F32)

    before = jnp.dot(onehot.astype(BF16), tri_ref[...], preferred_element_type=F32) + cnt_ref[...]
    ranks = [jnp.sum(jnp.where(hit, before, 0.0), axis=0, keepdims=True) for hit in hits]
    cnt_ref[...] += jnp.sum(onehot, axis=1, keepdims=True)

    denom = wts[0]
    for wk in wts[1:]:
        denom = denom + wk
    e_ref[...] = jnp.concatenate(ids, axis=0)
    w_ref[...] = jnp.concatenate([ROUTED_SCALE * wk / denom for wk in wts], axis=0)
    r_ref[...] = jnp.concatenate(ranks, axis=0).astype(jnp.int32)


def _post(alpha, oa, ga, gbyb, xf, mod, wpa, wo, l1g, l1b, wrt, rb, wgs, wus, wds, seq):
    t, d = xf.shape
    aw = oa.shape[1]
    n_exp = wrt.shape[0]
    tm = TM_POST
    per_seq = seq // tm
    row = lambda i: (i, 0)
    col = lambda i: (0, i)
    tri = (jnp.arange(tm)[:, None] < jnp.arange(tm)[None, :]).astype(BF16)
    return pl.pallas_call(
        functools.partial(_post_kernel, alpha),
        grid=(t // tm,),
        in_specs=[pl.BlockSpec((tm, aw), row), pl.BlockSpec((tm, d), row), pl.BlockSpec((tm, d), row),
                  pl.BlockSpec((tm, d), row),
                  pl.BlockSpec((1,) + mod.shape[1:], lambda i: (i // per_seq, 0, 0)),
                  _const_spec(wpa.shape), _const_spec(wo.shape), _const_spec(l1g.shape),
                  _const_spec(l1b.shape), _const_spec(wrt.shape), _const_spec(rb.shape),
                  _const_spec(wgs.shape), _const_spec(wus.shape), _const_spec(wds.shape),
                  _const_spec(tri.shape)],
        out_specs=[pl.BlockSpec((tm * ROW_TILE, LANES), row), pl.BlockSpec((tm, d), row),
                   pl.BlockSpec((TOP_K, tm), col), pl.BlockSpec((TOP_K, tm), col),
                   pl.BlockSpec((TOP_K, tm), col), _const_spec((n_exp, 1))],
        out_shape=[jax.ShapeDtypeStruct((t * ROW_TILE, LANES), F32), jax.ShapeDtypeStruct((t, d), F32),
                   jax.ShapeDtypeStruct((TOP_K, t), jnp.int32), jax.ShapeDtypeStruct((TOP_K, t), F32),
                   jax.ShapeDtypeStruct((TOP_K, t), jnp.int32), jax.ShapeDtypeStruct((n_exp, 1), F32)],
        compiler_params=_cparams(("arbitrary",)),
        name="post",
    )(oa, ga, gbyb, xf, mod, wpa, wo, l1g, l1b, wrt, rb, wgs, wus, wds, tri)


def _row_copy(src_ref, src_row, dst_ref, dst_row, sem):
    return pltpu.make_async_copy(
        src_ref.at[pl.ds(pl.multiple_of(src_row * ROW_TILE, ROW_TILE), ROW_TILE), :],
        dst_ref.at[pl.ds(pl.multiple_of(dst_row * ROW_TILE, ROW_TILE), ROW_TILE), :], sem)


def _dest_kernel(rs_ref, e_ref, r_ref, o_ref):
    e = e_ref[...]
    start = lax.fori_loop(0, rs_ref.shape[0], lambda x, acc: jnp.where(e == x, rs_ref[x], acc),
                          jnp.zeros(e.shape, jnp.int32))
    o_ref[...] = start + r_ref[...]


def _dest(row_start, e_idx, rank):
    t = e_idx.shape[1]
    tm = min(t, 2048)
    blk = pl.BlockSpec((TOP_K, tm), lambda i, *_: (0, i))
    return pl.pallas_call(
        _dest_kernel,
        grid_spec=pltpu.PrefetchScalarGridSpec(num_scalar_prefetch=1, grid=(t // tm,), in_specs=[blk, blk],
                                               out_specs=blk),
        out_shape=jax.ShapeDtypeStruct((TOP_K, t), jnp.int32),
        compiler_params=_cparams(("arbitrary",)),
        name="dest",
    )(row_start, e_idx, rank)


def _dispatch_kernel(n_blocks, ps_ref, pl_ref, nu_ref, dest_ref, h2_ref, xs_ref, zbuf, sem, zsem):
    i = pl.program_id(0)
    n_steps = pl.num_programs(0)
    tm = dest_ref.shape[1]
    n_exp = ps_ref.shape[0]
    per_step = -(-n_exp // n_steps)
    zbuf[...] = jnp.zeros(zbuf.shape, F32)

    def zero_fill(act):
        for q in range(per_step):
            e = jnp.minimum(i * per_step + q, n_exp - 1)
            live = i * per_step + q < n_exp
            gap = pl_ref[e]
            for bit in [1 << s for s in range(BM.bit_length() - 1)]:
                @pl.when(live & ((gap & bit) != 0))
                def _(bit=bit, e=e, gap=gap):
                    off = ps_ref[e] + (gap & ~(2 * bit - 1))
                    act(pltpu.make_async_copy(
                        zbuf.at[pl.ds(0, bit * ROW_TILE), :],
                        xs_ref.at[pl.ds(pl.multiple_of(off * ROW_TILE, ROW_TILE), bit * ROW_TILE), :], zsem))
            blk = nu_ref[0] + i * per_step + q
            @pl.when(blk < n_blocks)
            def _(blk=blk):
                rows = BM * ROW_TILE
                act(pltpu.make_async_copy(
                    zbuf, xs_ref.at[pl.ds(pl.multiple_of(blk * rows, rows), rows), :], zsem))

    zero_fill(lambda cp: cp.start(priority=1))

    def issue(j, c):
        for k in range(TOP_K):
            _row_copy(h2_ref, j, xs_ref, dest_ref[k, j], sem).start(priority=k % 2)
        return c

    lax.fori_loop(0, tm, issue, 0)
    n = tm * TOP_K * ROW_TILE
    pltpu.make_async_copy(xs_ref.at[pl.ds(0, n), :], xs_ref.at[pl.ds(0, n), :], sem).wait()
    zero_fill(lambda cp: cp.wait())


def _dispatch(pad_start, pad_len, n_used, dest, h2t, n_blocks):
    t = dest.shape[1]
    tm = TM_DISPATCH
    grid_spec = pltpu.PrefetchScalarGridSpec(
        num_scalar_prefetch=3,
        grid=(t // tm,),
        in_specs=[pl.BlockSpec((TOP_K, tm), lambda i, *_: (0, i), memory_space=pltpu.SMEM),
                  pl.BlockSpec((tm * ROW_TILE, LANES), lambda i, *_: (i, 0))],
        out_specs=pl.BlockSpec(memory_space=pl.ANY),
        scratch_shapes=[pltpu.VMEM((BM * ROW_TILE, LANES), F32), pltpu.SemaphoreType.DMA,
                        pltpu.SemaphoreType.DMA],
    )
    return pl.pallas_call(
        functools.partial(_dispatch_kernel, n_blocks),
        grid_spec=grid_spec,
        out_shape=jax.ShapeDtypeStruct((n_blocks * BM * ROW_TILE, LANES), F32),
        compiler_params=_cparams(("arbitrary",)),
        name="dispatch",
    )(pad_start, pad_len, n_used, dest, h2t)


def _expert_kernel(nblk_ref, bstart_ref, widx_ref, nu_ref, nv_ref, xs_ref, wg_ref, wu_ref, wd_ref, ys_ref,
                   xbuf, ybuf, wg_b, wu_b, wd_b, xsem, ysem):
    e = pl.program_id(0)
    n_used = nu_ref[0]
    rows = BM * ROW_TILE
    piece = rows // EXPERT_CHUNKS
    ring = EXPERT_RING

    def slot_base(g):
        return pl.multiple_of((g % ring) * rows, rows)

    def pieces(g, hbm_ref, buf, sem, to_hbm, act):
        for c in range(EXPERT_CHUNKS):
            @pl.when(c * (BM // EXPERT_CHUNKS) < nv_ref[g])
            def _(c=c):
                in_hbm = hbm_ref.at[pl.ds(pl.multiple_of(g * rows + c * piece, piece), piece), :]
                in_vmem = buf.at[pl.ds(pl.multiple_of(slot_base(g) + c * piece, piece), piece), :]
                src, dst = (in_vmem, in_hbm) if to_hbm else (in_hbm, in_vmem)
                act(pltpu.make_async_copy(src, dst, sem.at[g % ring]))

    def x_load(g, act):
        pieces(g, xs_ref, xbuf, xsem, False, act)

    def y_store(g, act):
        pieces(g, ys_ref, ybuf, ysem, True, act)

    start = lambda cp: cp.start()
    start_store = lambda cp: cp.start(priority=1)
    wait = lambda cp: cp.wait()

    @pl.when(e == 0)
    def _():
        xbuf[...] = jnp.zeros(xbuf.shape, F32)
        for g0 in range(ring - 1):
            @pl.when(g0 < n_used)
            def _(g0=g0):
                x_load(g0, start)

    wg_b[...] = wg_ref[0].astype(BF16)
    wu_b[...] = wu_ref[0].astype(BF16)
    wd_b[...] = wd_ref[0].astype(BF16)

    def block(b, c):
        g = bstart_ref[e] + b
        x_load(g, wait)

        @pl.when(g + ring - 1 < n_used)
        def _():
            x_load(g + ring - 1, start)

        @pl.when(g >= ring)
        def _():
            y_store(g - ring, wait)

        x = _load_token_rows(xbuf, slot_base(g), BM).astype(BF16)
        a = jnp.dot(x, wg_b[...], preferred_element_type=F32)
        u = jnp.dot(x, wu_b[...], preferred_element_type=F32)
        act = (a * _sigmoid(a) * u).astype(BF16)
        _store_token_rows(ybuf, slot_base(g), jnp.dot(act, wd_b[...], preferred_element_type=F32))
        y_store(g, start_store)
        return c

    lax.fori_loop(0, nblk_ref[e], block, 0)

    @pl.when(e == pl.num_programs(0) - 1)
    def _():
        for back in range(ring, 0, -1):
            @pl.when(n_used >= back)
            def _(back=back):
                y_store(n_used - back, wait)


def _experts(nblk, blk_start, w_idx, n_used, block_nv, xs, wg, wu, wd):
    n_exp, d, de = wg.shape
    rows = BM * ROW_TILE
    wsel = lambda e, nb, bs, wi, nu, nv: (wi[e], 0, 0)
    grid_spec = pltpu.PrefetchScalarGridSpec(
        num_scalar_prefetch=5,
        grid=(n_exp,),
        in_specs=[pl.BlockSpec(memory_space=pl.ANY),
                  pl.BlockSpec((1, d, de), wsel), pl.BlockSpec((1, d, de), wsel),
                  pl.BlockSpec((1, de, d), wsel)],
        out_specs=pl.BlockSpec(memory_space=pl.ANY),
        scratch_shapes=[pltpu.VMEM((EXPERT_RING * rows, LANES), F32), pltpu.VMEM((EXPERT_RING * rows, LANES), F32),
                        pltpu.VMEM((d, de), BF16), pltpu.VMEM((d, de), BF16), pltpu.VMEM((de, d), BF16),
                        pltpu.SemaphoreType.DMA((EXPERT_RING,)), pltpu.SemaphoreType.DMA((EXPERT_RING,))],
    )
    return pl.pallas_call(
        _expert_kernel,
        grid_spec=grid_spec,
        out_shape=jax.ShapeDtypeStruct(xs.shape, xs.dtype),
        input_output_aliases={5: 0},
        compiler_params=_cparams(("arbitrary",)),
        name="experts",
    )(nblk, blk_start, w_idx, n_used, block_nv, xs, wg, wu, wd)


def _combine_kernel(dest_ref, next_dest_ref, w_ref, base_ref, mod_ref, g_ref, b_ref, ys_ref, o_ref, buf, sems):
    i = pl.program_id(0)
    tm = dest_ref.shape[1]
    slab = tm * ROW_TILE
    slot_rows = TOP_K * slab

    def gather(idx_ref, slot):
        def issue(j, c):
            for k in range(TOP_K):
                _row_copy(ys_ref, idx_ref[k, j], buf, slot * (TOP_K * tm) + k * tm + j,
                          sems.at[slot]).start(priority=k % 2)
            return c
        lax.fori_loop(0, tm, issue, 0)

    @pl.when(i == 0)
    def _():
        gather(dest_ref, 0)

    @pl.when(i + 1 < pl.num_programs(0))
    def _():
        gather(next_dest_ref, (i + 1) % 2)

    slot = i % 2
    base_row = pl.multiple_of(slot * slot_rows, slot_rows)
    pltpu.make_async_copy(ys_ref.at[pl.ds(0, slot_rows), :], buf.at[pl.ds(base_row, slot_rows), :],
                          sems.at[slot]).wait()

    wpad = jnp.concatenate([w_ref[...], jnp.zeros((LANES - TOP_K, tm), F32)], axis=0)
    wcol = wpad.T
    routed = jnp.zeros(base_ref.shape, F32)
    for k in range(TOP_K):
        routed = routed + wcol[:, k:k + 1] * _load_token_rows(buf, base_row + k * slab, tm)
    gate_f = mod_ref[0, 5:6, :]
    o_ref[...] = _layer_norm(base_ref[...] + gate_f * routed, g_ref[...], b_ref[...])


def _combine(dest, wts, base, mod, g, b, ys, seq):
    t, d = base.shape
    tm = TM_COMBINE
    per_seq = seq // tm
    n_steps = t // tm
    col = lambda i: (0, i)
    row = lambda i: (i, 0)
    return pl.pallas_call(
        _combine_kernel,
        grid=(n_steps,),
        in_specs=[pl.BlockSpec((TOP_K, tm), col, memory_space=pltpu.SMEM),
                  pl.BlockSpec((TOP_K, tm), lambda i: (0, jnp.minimum(i + 1, n_steps - 1)),
                               memory_space=pltpu.SMEM),
                  pl.BlockSpec((TOP_K, tm), col), pl.BlockSpec((tm, d), row),
                  pl.BlockSpec((1,) + mod.shape[1:], lambda i: (i // per_seq, 0, 0)),
                  _const_spec(g.shape), _const_spec(b.shape),
                  pl.BlockSpec(memory_space=pl.ANY)],
        out_specs=pl.BlockSpec((tm, d), row),
        out_shape=jax.ShapeDtypeStruct((t, d), F32),
        scratch_shapes=[pltpu.VMEM((2 * TOP_K * tm * ROW_TILE, LANES), F32), pltpu.SemaphoreType.DMA((2,))],
        compiler_params=_cparams(("arbitrary",)),
        name="combine",
    )(dest, dest, wts, base, mod, g, b, ys)


def _rope_tables(seq):
    half = HEAD_DIM // 2
    inv_freq = ROPE_THETA ** (-jnp.arange(half, dtype=F32) / half)
    ang = jnp.arange(seq, dtype=F32)[:, None] * inv_freq[None, :]
    cos = jnp.concatenate([jnp.cos(ang), jnp.cos(ang)], axis=1)
    sin = jnp.concatenate([-jnp.sin(ang), jnp.sin(ang)], axis=1)
    return jnp.tile(cos, (1, N_HEADS)), jnp.tile(sin, (1, N_HEADS))


def _layer(x, c, w_ada, b_ada, w_in, ln_v_g, ln_v_b, w_spatial, b_spatial, w_proj_a, w_proj_b, w_out,
           ln1_g, ln1_b, w_router, router_bias, w_gate_e, w_up_e, w_down_e, w_gate_sh, w_up_sh,
           w_down_sh, ln2_g, ln2_b, alpha):
    bsz, seq, d = x.shape
    assert seq % (max(DILATIONS) * BAND) == 0 and seq % TM_IN == 0 and d == ROW_TILE * LANES
    t = bsz * seq
    aw = N_HEADS * HEAD_DIM
    gwid = ln_v_g.shape[0]
    n_exp = w_router.shape[1]
    xf = x.reshape(t, d)

    mod = _ada(c, w_ada, b_ada).reshape(bsz, 6, d)

    cuts = [0, aw, 2 * aw, 3 * aw, 3 * aw + gwid, 3 * aw + 2 * gwid, 3 * aw + 2 * gwid + d,
            3 * aw + 2 * gwid + 2 * d]
    wq, wk, wv, wu, wvg, wga, wgb = [w_in[:, a:b].astype(BF16) for a, b in zip(cuts[:-1], cuts[1:])]
    cos, sin = _rope_tables(seq)
    bsp = jnp.repeat(b_spatial.T, gwid // GMLP_GROUPS, axis=1)
    qkv, ga, gbyb = _inproj(xf, mod, cos, sin, wq, wk, wv, wu, wvg, wga, wgb,
                            ln_v_g.reshape(1, gwid), ln_v_b.reshape(1, gwid), w_spatial, bsp,
                            w_proj_b.astype(BF16), seq)

    oa = _attention(qkv, bsz, seq)

    h2t, base, e_idx, wts, rank, cnt = _post(
        alpha, oa.reshape(t, aw), ga, gbyb, xf, mod, w_proj_a.astype(BF16), w_out.astype(BF16),
        ln1_g.reshape(1, d), ln1_b.reshape(1, d), w_router.T.astype(BF16), router_bias.reshape(n_exp, 1),
        w_gate_sh.astype(BF16), w_up_sh.astype(BF16), w_down_sh.astype(BF16), seq)

    counts = cnt[:, 0].astype(jnp.int32)
    nblk = ((counts + BM - 1) // BM).astype(jnp.int32)
    blk_end = jnp.cumsum(nblk).astype(jnp.int32)
    blk_start = blk_end - nblk
    n_blocks = (t * TOP_K) // BM + n_exp
    n_used = blk_end[-1:]
    row_start = blk_start * BM
    ids = jnp.arange(n_exp, dtype=jnp.int32)
    seen = jnp.where((ids[None, :] <= ids[:, None]) & (nblk[None, :] > 0), ids[None, :], -1).max(axis=1)
    w_idx = jnp.where(seen >= 0, seen, jnp.where(nblk > 0, ids, n_exp - 1).min()).astype(jnp.int32)

    bidx = jnp.arange(n_blocks, dtype=jnp.int32)[:, None]
    owner = (bidx >= blk_start[None, :]) & (bidx < blk_end[None, :])
    block_nv = jnp.where(owner, jnp.minimum(counts[None, :] - (bidx - blk_start[None, :]) * BM, BM), 0).sum(axis=1)

    dest = _dest(row_start, e_idx, rank)
    xs = _dispatch(row_start + counts, nblk * BM - counts, n_used, dest, h2t, n_blocks)
    ys = _experts(nblk, blk_start, w_idx, n_used, block_nv.astype(jnp.int32), xs, w_gate_e, w_up_e, w_down_e)
    out = _combine(dest, wts, base, mod, ln2_g.reshape(1, d), ln2_b.reshape(1, d), ys, seq)
    return out.reshape(bsz, seq, d)


def kernel(x, c, w_ada, b_ada, w_in, ln_v_g, ln_v_b, w_spatial, b_spatial, w_proj_a, w_proj_b, w_out,
           ln1_g, ln1_b, w_router, router_bias, w_gate_e, w_up_e, w_down_e, w_gate_sh, w_up_sh,
           w_down_sh, ln2_g, ln2_b):
    depth = w_ada.shape[0]
    alpha = (2.0 * depth) ** 0.25
    for l in range(depth):
        x = _layer(x, c, w_ada[l], b_ada[l], w_in[l], ln_v_g[l], ln_v_b[l], w_spatial[l], b_spatial[l],
                   w_proj_a[l], w_proj_b[l], w_out[l], ln1_g[l], ln1_b[l], w_router[l], router_bias[l],
                   w_gate_e[l], w_up_e[l], w_down_e[l], w_gate_sh[l], w_up_sh[l], w_down_sh[l],
                   ln2_g[l], ln2_b[l], alpha)
    return x
```

```python
import functools
import math

import jax
import jax.numpy as jnp
from jax import lax
from jax.experimental import pallas as pl
from jax.experimental.pallas import tpu as pltpu

F32 = jnp.float32
BF16 = jnp.bfloat16

N_HEADS = 8
HEAD_DIM = 64
DILATIONS = (1, 4, 16)
BAND = 128
ROPE_THETA = 10000.0
GMLP_GROUPS = 8
GMLP_CHUNK = 128
N_EXPERT_GROUPS = 8
TOPK_GROUPS = 4
TOP_K = 8
ROUTED_SCALE = 2.5
LN_EPS = 1e-5
LANES = 128
SUBLANES = 8
ROW_TILE = SUBLANES
VMEM_LIMIT = 56 * 1024 * 1024

TM_IN = 512
TM_POST = 512
TM_DISPATCH = 1024
TM_COMBINE = 256
BM = 256
EXPERT_RING = 4
EXPERT_CHUNKS = 4
ATTN_BLOCKS_PER_TRIP = 16


def _cparams(sem):
    return pltpu.CompilerParams(dimension_semantics=sem, vmem_limit_bytes=VMEM_LIMIT)


def _const_spec(shape):
    nd = len(shape)
    return pl.BlockSpec(shape, lambda *_: (0,) * nd)


def _layer_norm(r, g, b):
    mu = jnp.mean(r, axis=-1, keepdims=True)
    d = r - mu
    var = jnp.mean(d * d, axis=-1, keepdims=True)
    return d * lax.rsqrt(var + LN_EPS) * g + b


def _gelu(t):
    return 0.5 * t * (1.0 + lax.erf(t * (1.0 / math.sqrt(2.0))))


def _sigmoid(t):
    return 1.0 / (1.0 + jnp.exp(-t))


def _store_token_rows(ref, first_row, v):
    m = v.shape[0]
    for s in range(ROW_TILE):
        ref[pl.ds(first_row + s, m, stride=ROW_TILE), :] = v[:, s * LANES:(s + 1) * LANES]


def _load_token_rows(ref, first_row, m):
    return jnp.concatenate([ref[pl.ds(first_row + s, m, stride=ROW_TILE), :] for s in range(ROW_TILE)], axis=1)


def _ada_kernel(c_ref, w_ref, b_ref, o_ref):
    c = c_ref[...]
    o_ref[...] = jnp.dot(c * _sigmoid(c), w_ref[...], preferred_element_type=F32) + b_ref[...]


def _ada(c, w, b):
    bsz, d = c.shape
    n = w.shape[1]
    tn = 1024
    return pl.pallas_call(
        _ada_kernel,
        grid=(n // tn,),
        in_specs=[_const_spec((bsz, d)), pl.BlockSpec((d, tn), lambda j: (0, j)),
                  pl.BlockSpec((1, tn), lambda j: (0, j))],
        out_specs=pl.BlockSpec((bsz, tn), lambda j: (0, j)),
        out_shape=jax.ShapeDtypeStruct((bsz, n), F32),
        compiler_params=_cparams(("arbitrary",)),
        name="ada",
    )(c, w, b.reshape(1, n))


def _inproj_kernel(x_ref, mod_ref, cos_ref, sin_ref, wq_ref, wk_ref, wv_ref, wu_ref, wvg_ref,
                   wga_ref, wgb_ref, lng_ref, lnb_ref, wsp_ref, bsp_ref, wpb_ref, *rest):
    n_cm = len(DILATIONS) - 1
    q_refs, k_refs, v_refs = [rest[(1 + n_cm) * a:(1 + n_cm) * (a + 1)] for a in range(3)]
    ga_ref, gbyb_ref, u_scr, vv_scr, gated_scr, q_stage, k_stage, v_stage = rest[3 * (1 + n_cm):]
    tm = x_ref.shape[0]

    def emit(val, refs, stage):
        refs[0][...] = val.astype(BF16)
        for hp in range(val.shape[1] // LANES):
            stage[pl.ds(hp * tm, tm), :] = val[:, hp * LANES:(hp + 1) * LANES]
        for ref, dil in zip(refs[1:], DILATIONS[1:]):
            for hp in range(val.shape[1] // LANES):
                for r in range(dil):
                    ref[0, hp, r, :, :] = stage[pl.ds(hp * tm + r, tm // dil, stride=dil), :].astype(BF16)
    shift = mod_ref[0, 0:1, :]
    scale = mod_ref[0, 1:2, :]
    h = (x_ref[...] * (1.0 + scale) + shift).astype(BF16)

    cos = cos_ref[...]
    sin = sin_ref[...]
    aw = cos.shape[1]
    lane = lax.broadcasted_iota(jnp.int32, (tm, aw), 1)
    first_half = (lane % HEAD_DIM) < (HEAD_DIM // 2)

    def rope(t):
        partner = jnp.where(first_half, pltpu.roll(t, aw - HEAD_DIM // 2, 1),
                            pltpu.roll(t, HEAD_DIM // 2, 1))
        return t * cos + partner * sin

    emit(rope(jnp.dot(h, wq_ref[...], preferred_element_type=F32)) * (HEAD_DIM ** -0.5), q_refs, q_stage)
    emit(rope(jnp.dot(h, wk_ref[...], preferred_element_type=F32)), k_refs, k_stage)
    emit(jnp.dot(h, wv_ref[...], preferred_element_type=F32), v_refs, v_stage)

    u_scr[...] = _gelu(jnp.dot(h, wu_ref[...], preferred_element_type=F32))
    vg = _gelu(jnp.dot(h, wvg_ref[...], preferred_element_type=F32))
    vv_scr[...] = _layer_norm(vg, lng_ref[...], lnb_ref[...]).astype(BF16)

    ti = lax.broadcasted_iota(jnp.int32, (GMLP_CHUNK, GMLP_CHUNK), 0)
    si = lax.broadcasted_iota(jnp.int32, (GMLP_CHUNK, GMLP_CHUNK), 1)
    causal = si <= ti
    gw = u_scr.shape[1] // GMLP_GROUPS
    for g in range(GMLP_GROUPS):
        ws = jnp.where(causal, wsp_ref[g], 0.0).astype(BF16)
        cols = slice(g * gw, (g + 1) * gw)
        for c in range(tm // GMLP_CHUNK):
            rows = slice(c * GMLP_CHUNK, (c + 1) * GMLP_CHUNK)
            sv = jnp.dot(ws, vv_scr[rows, cols], preferred_element_type=F32) + bsp_ref[:, cols]
            gated_scr[rows, cols] = (u_scr[rows, cols] * sv).astype(BF16)

    y_b = jnp.dot(gated_scr[...], wpb_ref[...], preferred_element_type=F32)
    g_b = _sigmoid(jnp.dot(h, wgb_ref[...], preferred_element_type=F32))
    gbyb_ref[...] = (g_b * y_b).astype(BF16)
    ga_ref[...] = _sigmoid(jnp.dot(h, wga_ref[...], preferred_element_type=F32)).astype(BF16)


def _inproj(xf, mod, cos, sin, wq, wk, wv, wu, wvg, wga, wgb, lng, lnb, wsp, bsp, wpb, seq):
    t, d = xf.shape
    aw = wq.shape[1]
    gwid = wu.shape[1]
    tm = TM_IN
    per_seq = seq // tm
    row = lambda i: (i, 0)
    tab = lambda i: (i % per_seq, 0)
    bsz = t // seq
    qkv_specs = [pl.BlockSpec((tm, aw), row)] + [
        pl.BlockSpec((1, aw // LANES, dil, tm // dil, LANES), lambda i: (i // per_seq, 0, 0, i % per_seq, 0))
        for dil in DILATIONS[1:]]
    qkv_shapes = [jax.ShapeDtypeStruct((t, aw), BF16)] + [
        jax.ShapeDtypeStruct((bsz, aw // LANES, dil, seq // dil, LANES), BF16) for dil in DILATIONS[1:]]
    outs = pl.pallas_call(
        _inproj_kernel,
        grid=(t // tm,),
        in_specs=[pl.BlockSpec((tm, d), row),
                  pl.BlockSpec((1,) + mod.shape[1:], lambda i: (i // per_seq, 0, 0)),
                  pl.BlockSpec((tm, aw), tab), pl.BlockSpec((tm, aw), tab),
                  _const_spec(wq.shape), _const_spec(wk.shape), _const_spec(wv.shape),
                  _const_spec(wu.shape), _const_spec(wvg.shape), _const_spec(wga.shape),
                  _const_spec(wgb.shape), _const_spec(lng.shape), _const_spec(lnb.shape),
                  _const_spec(wsp.shape), _const_spec(bsp.shape), _const_spec(wpb.shape)],
        out_specs=qkv_specs * 3 + [pl.BlockSpec((tm, d), row)] * 2,
        out_shape=qkv_shapes * 3 + [jax.ShapeDtypeStruct((t, d), BF16)] * 2,
        scratch_shapes=[pltpu.VMEM((tm, gwid), F32), pltpu.VMEM((tm, gwid), BF16),
                        pltpu.VMEM((tm, gwid), BF16)] + [pltpu.VMEM((tm * aw // LANES, LANES), F32)] * 3,
        compiler_params=_cparams(("arbitrary",)),
        name="inproj",
    )(xf, mod, cos, sin, wq, wk, wv, wu, wvg, wga, wgb, lng, lnb, wsp, bsp, wpb)
    n_lay = len(DILATIONS)
    return [outs[a * n_lay:(a + 1) * n_lay] for a in range(3)], outs[3 * n_lay], outs[3 * n_lay + 1]


def _rows(start, size, stride):
    if stride == 1:
        return pl.ds(start, size)
    return pl.ds(start, size, stride=stride)


def _band_aligned(start):
    return start if isinstance(start, int) else pl.multiple_of(start, BAND)


def _interleaved_loop(count, load, compute, store):
    per_trip = max(u for u in range(1, ATTN_BLOCKS_PER_TRIP + 1) if count % u == 0) if count else 1

    def trip(it, c):
        items = [it * per_trip + u for u in range(per_trip)]
        results = [compute(ops) for ops in [load(i) for i in items]]
        for i, res in zip(items, results):
            store(i, res)
        return c

    if count == per_trip:
        trip(0, 0)
    elif count:
        lax.fori_loop(0, count // per_trip, trip, 0)


def _attn_kernel(*refs):
    npat = len(DILATIONS)
    qkv = [refs[3 * p:3 * p + 3] for p in range(npat)]
    o_ref, acc_scr, max_scr, sum_scr = refs[3 * npat:]
    seq = o_ref.shape[1]

    lane = lax.broadcasted_iota(jnp.int32, (BAND, LANES), 1)
    head0 = lane < HEAD_DIM
    neg = jnp.float32(-1e30)

    qi = lax.broadcasted_iota(jnp.int32, (BAND, 2 * BAND), 0)
    kj = lax.broadcasted_iota(jnp.int32, (BAND, 2 * BAND), 1)

    def load(p, blk, nb):
        q_ref, k_ref, v_ref = qkv[p]
        q_start = blk * BAND
        k_start = jnp.maximum(q_start - BAND, 0)
        off = q_start - k_start
        lowest = jnp.where(blk % nb == 0, off, qi)
        mask = (kj >= lowest) & (kj <= qi + off)
        krows = pl.ds(_band_aligned(k_start), 2 * BAND)
        return (q_ref[0, 0, pl.ds(_band_aligned(q_start), BAND), :], k_ref[0, 0, krows, :],
                v_ref[0, 0, krows, :], mask)

    def attend(ops):
        qb, kb, vb, mask = ops
        zero = jnp.zeros(qb.shape, qb.dtype)
        parts = []
        for hd in range(2):
            qh = jnp.where(head0 if hd == 0 else ~head0, qb, zero)
            s = lax.dot_general(qh, kb, (((1,), (1,)), ((), ())), preferred_element_type=F32)
            s = jnp.where(mask, s, neg)
            m = jnp.max(s, axis=-1, keepdims=True)
            pr = jnp.exp(s - m)
            l = jnp.sum(pr, axis=-1, keepdims=True)
            acc = jnp.dot(pr.astype(BF16), vb, preferred_element_type=F32)
            parts.append((acc, m, l))
        return [jnp.where(head0, a0, a1) for a0, a1 in zip(*parts)]

    def store(p, seq_start, stride, res):
        rows = _rows(seq_start if stride > 1 else _band_aligned(seq_start), BAND, stride)
        acc_scr[p, rows, :] = res[0]
        max_scr[p, rows, :] = res[1]
        sum_scr[p, rows, :] = res[2]

    for p, dil in enumerate(DILATIONS):
        n = seq // dil
        nb = n // BAND

        _interleaved_loop(
            seq // BAND, lambda blk, p=p, nb=nb: load(p, blk, nb), attend,
            lambda blk, res, p=p, dil=dil, nb=nb: store(p, blk // nb + dil * BAND * (blk % nb), dil, res))

    top = jnp.maximum(jnp.maximum(max_scr[0], max_scr[1]), max_scr[2])
    num = jnp.zeros(top.shape, F32)
    den = jnp.zeros(top.shape, F32)
    for p in range(npat):
        w = jnp.exp(max_scr[p] - top)
        num = num + w * acc_scr[p]
        den = den + w * sum_scr[p]
    o_ref[0] = (num / den).astype(BF16)


def _attention(qkv, bsz, seq):
    aw = qkv[0][0].shape[1]
    npat = len(DILATIONS)
    blk = (1, 1, seq, LANES)
    operands, specs = [], []
    for p, dil in enumerate(DILATIONS):
        for layouts in qkv:
            if dil == 1:
                operands.append(layouts[p].reshape(bsz, 1, seq, aw))
                specs.append(pl.BlockSpec(blk, lambda b, hp: (b, 0, 0, hp)))
            else:
                operands.append(layouts[p].reshape(bsz, aw // LANES, seq, LANES))
                specs.append(pl.BlockSpec(blk, lambda b, hp: (b, hp, 0, 0)))
    return pl.pallas_call(
        _attn_kernel,
        grid=(bsz, aw // LANES),
        in_specs=specs,
        out_specs=pl.BlockSpec((1, seq, LANES), lambda b, hp: (b, 0, hp)),
        out_shape=jax.ShapeDtypeStruct((bsz, seq, aw), BF16),
        scratch_shapes=[pltpu.VMEM((npat, seq, LANES), F32)] * 3,
        compiler_params=_cparams(("arbitrary", "arbitrary")),
        name="attn",
    )(*operands)


def _post_kernel(alpha, oa_ref, ga_ref, gbyb_ref, x_ref, mod_ref, wpa_ref, wo_ref, l1g_ref, l1b_ref,
                 wrt_ref, rb_ref, wgs_ref, wus_ref, wds_ref, tri_ref,
                 h2_ref, base_ref, e_ref, w_ref, r_ref, cnt_ref):
    tm = x_ref.shape[0]
    n_exp = wrt_ref.shape[0]
    per_group = n_exp // N_EXPERT_GROUPS
    gate_a = mod_ref[0, 2:3, :]
    shift_f = mod_ref[0, 3:4, :]
    scale_f = mod_ref[0, 4:5, :]
    gate_f = mod_ref[0, 5:6, :]

    y_a = jnp.dot(oa_ref[...], wpa_ref[...], preferred_element_type=F32)
    merged = ga_ref[...].astype(F32) * y_a + gbyb_ref[...].astype(F32)
    mix = jnp.dot(merged.astype(BF16), wo_ref[...], preferred_element_type=F32)
    x1 = _layer_norm(alpha * x_ref[...] + gate_a * mix, l1g_ref[...], l1b_ref[...])
    h2 = x1 * (1.0 + scale_f) + shift_f
    h2b = h2.astype(BF16)

    _store_token_rows(h2_ref, 0, h2)

    a = jnp.dot(h2b, wgs_ref[...], preferred_element_type=F32)
    b = jnp.dot(h2b, wus_ref[...], preferred_element_type=F32)
    shared = jnp.dot((a * _sigmoid(a) * b).astype(BF16), wds_ref[...], preferred_element_type=F32)
    base_ref[...] = alpha * x1 + gate_f * shared

    logits = lax.dot_general(wrt_ref[...], h2b, (((1,), (1,)), ((), ())), preferred_element_type=F32)
    scores = _sigmoid(logits)
    biased = scores + rb_ref[...]
    ninf = jnp.float32(-jnp.inf)

    sub = lax.broadcasted_iota(jnp.int32, (per_group, tm), 0)
    grp_rows = []
    for g in range(N_EXPERT_GROUPS):
        blk = biased[g * per_group:(g + 1) * per_group, :]
        m1 = jnp.max(blk, axis=0, keepdims=True)
        f1 = jnp.min(jnp.where(blk == m1, sub, per_group), axis=0, keepdims=True)
        m2 = jnp.max(jnp.where(sub == f1, ninf, blk), axis=0, keepdims=True)
        grp_rows.append(m1 + m2)
    grp = jnp.concatenate(grp_rows, axis=0)

    gi = lax.broadcasted_iota(jnp.int32, grp.shape, 0)
    chosen = jnp.zeros(grp.shape, jnp.bool_)
    cur = grp
    for _ in range(TOPK_GROUPS):
        m = jnp.max(cur, axis=0, keepdims=True)
        f = jnp.min(jnp.where(cur == m, gi, N_EXPERT_GROUPS), axis=0, keepdims=True)
        hit = gi == f
        chosen = chosen | hit
        cur = jnp.where(hit, ninf, cur)
    emask = jnp.concatenate(
        [jnp.broadcast_to(chosen[g:g + 1, :], (per_group, tm)) for g in range(N_EXPERT_GROUPS)], axis=0)

    ei = lax.broadcasted_iota(jnp.int32, (n_exp, tm), 0)
    cand = jnp.where(emask, biased, ninf)
    hits, ids, wts = [], [], []
    for _ in range(TOP_K):
        m = jnp.max(cand, axis=0, keepdims=True)
        f = jnp.min(jnp.where(cand == m, ei, n_exp), axis=0, keepdims=True)
        hit = ei == f
        wts.append(jnp.sum(jnp.where(hit, scores, 0.0), axis=0, keepdims=True))
        cand = jnp.where(hit, ninf, cand)
        hits.append(hit)
        ids.append(f)
    onehot = hits[0]
    for hit in hits[1:]:
        onehot = onehot | hit
    onehot = jnp.where(onehot, 1.0, 0.0)

    @pl.when(pl.program_id(0) == 0)
    def _():
        cnt_ref[...] = jnp.zeros(cnt_ref.shape, F32)

    before = jnp.dot(onehot.astype(BF16), tri_ref[...], preferred_element_type=F32) + cnt_ref[...]
    ranks = [jnp.sum(jnp.where(hit, before, 0.0), axis=0, keepdims=True) for hit in hits]
    cnt_ref[...] += jnp.sum(onehot, axis=1, keepdims=True)

    denom = wts[0]
    for wk in wts[1:]:
        denom = denom + wk
    e_ref[...] = jnp.concatenate(ids, axis=0)
    w_ref[...] = jnp.concatenate([ROUTED_SCALE * wk / denom for wk in wts], axis=0)
    r_ref[...] = jnp.concatenate(ranks, axis=0).astype(jnp.int32)


def _post(alpha, oa, ga, gbyb, xf, mod, wpa, wo, l1g, l1b, wrt, rb, wgs, wus, wds, seq):
    t, d = xf.shape
    aw = oa.shape[1]
    n_exp = wrt.shape[0]
    tm = TM_POST
    per_seq = seq // tm
    row = lambda i: (i, 0)
    col = lambda i: (0, i)
    tri = (jnp.arange(tm)[:, None] < jnp.arange(tm)[None, :]).astype(BF16)
    return pl.pallas_call(
        functools.partial(_post_kernel, alpha),
        grid=(t // tm,),
        in_specs=[pl.BlockSpec((tm, aw), row), pl.BlockSpec((tm, d), row), pl.BlockSpec((tm, d), row),
                  pl.BlockSpec((tm, d), row),
                  pl.BlockSpec((1,) + mod.shape[1:], lambda i: (i // per_seq, 0, 0)),
                  _const_spec(wpa.shape), _const_spec(wo.shape), _const_spec(l1g.shape),
                  _const_spec(l1b.shape), _const_spec(wrt.shape), _const_spec(rb.shape),
                  _const_spec(wgs.shape), _const_spec(wus.shape), _const_spec(wds.shape),
                  _const_spec(tri.shape)],
        out_specs=[pl.BlockSpec((tm * ROW_TILE, LANES), row), pl.BlockSpec((tm, d), row),
                   pl.BlockSpec((TOP_K, tm), col), pl.BlockSpec((TOP_K, tm), col),
                   pl.BlockSpec((TOP_K, tm), col), _const_spec((n_exp, 1))],
        out_shape=[jax.ShapeDtypeStruct((t * ROW_TILE, LANES), F32), jax.ShapeDtypeStruct((t, d), F32),
                   jax.ShapeDtypeStruct((TOP_K, t), jnp.int32), jax.ShapeDtypeStruct((TOP_K, t), F32),
                   jax.ShapeDtypeStruct((TOP_K, t), jnp.int32), jax.ShapeDtypeStruct((n_exp, 1), F32)],
        compiler_params=_cparams(("arbitrary",)),
        name="post",
    )(oa, ga, gbyb, xf, mod, wpa, wo, l1g, l1b, wrt, rb, wgs, wus, wds, tri)


def _row_copy(src_ref, src_row, dst_ref, dst_row, sem):
    def tile(ref, row):
        start = row * ROW_TILE
        return ref.at[pl.ds(start if isinstance(start, int) else pl.multiple_of(start, ROW_TILE), ROW_TILE), :]

    return pltpu.make_async_copy(tile(src_ref, src_row), tile(dst_ref, dst_row), sem)


def _dest_kernel(rs_ref, e_ref, r_ref, o_ref):
    e = e_ref[...]
    start = lax.fori_loop(0, rs_ref.shape[0], lambda x, acc: jnp.where(e == x, rs_ref[x], acc),
                          jnp.zeros(e.shape, jnp.int32))
    o_ref[...] = start + r_ref[...]


def _dest(row_start, e_idx, rank):
    t = e_idx.shape[1]
    tm = min(t, 2048)
    blk = pl.BlockSpec((TOP_K, tm), lambda i, *_: (0, i))
    return pl.pallas_call(
        _dest_kernel,
        grid_spec=pltpu.PrefetchScalarGridSpec(num_scalar_prefetch=1, grid=(t // tm,), in_specs=[blk, blk],
                                               out_specs=blk),
        out_shape=jax.ShapeDtypeStruct((TOP_K, t), jnp.int32),
        compiler_params=_cparams(("arbitrary",)),
        name="dest",
    )(row_start, e_idx, rank)


def _dispatch_kernel(n_blocks, ps_ref, pl_ref, nu_ref, dest_ref, h2_ref, xs_ref, zbuf, sem, zsem):
    i = pl.program_id(0)
    n_steps = pl.num_programs(0)
    tm = dest_ref.shape[1]
    n_exp = ps_ref.shape[0]
    per_step = -(-n_exp // n_steps)
    zbuf[...] = jnp.zeros(zbuf.shape, F32)

    def zero_fill(act):
        for q in range(per_step):
            e = jnp.minimum(i * per_step + q, n_exp - 1)
            live = i * per_step + q < n_exp
            gap = pl_ref[e]
            for bit in [1 << s for s in range(BM.bit_length() - 1)]:
                @pl.when(live & ((gap & bit) != 0))
                def _(bit=bit, e=e, gap=gap):
                    off = ps_ref[e] + (gap & ~(2 * bit - 1))
                    act(pltpu.make_async_copy(
                        zbuf.at[pl.ds(0, bit * ROW_TILE), :],
                        xs_ref.at[pl.ds(pl.multiple_of(off * ROW_TILE, ROW_TILE), bit * ROW_TILE), :], zsem))
            blk = nu_ref[0] + i * per_step + q
            @pl.when(blk < n_blocks)
            def _(blk=blk):
                rows = BM * ROW_TILE
                act(pltpu.make_async_copy(
                    zbuf, xs_ref.at[pl.ds(pl.multiple_of(blk * rows, rows), rows), :], zsem))

    zero_fill(lambda cp: cp.start(priority=1))

    def issue(j, c):
        for k in range(TOP_K):
            _row_copy(h2_ref, j, xs_ref, dest_ref[k, j], sem).start(priority=k % 2)
        return c

    lax.fori_loop(0, tm, issue, 0)
    n = tm * TOP_K * ROW_TILE
    pltpu.make_async_copy(xs_ref.at[pl.ds(0, n), :], xs_ref.at[pl.ds(0, n), :], sem).wait()
    zero_fill(lambda cp: cp.wait())


def _dispatch(pad_start, pad_len, n_used, dest, h2t, n_blocks):
    t = dest.shape[1]
    tm = TM_DISPATCH
    grid_spec = pltpu.PrefetchScalarGridSpec(
        num_scalar_prefetch=3,
        grid=(t // tm,),
        in_specs=[pl.BlockSpec((TOP_K, tm), lambda i, *_: (0, i), memory_space=pltpu.SMEM),
                  pl.BlockSpec((tm * ROW_TILE, LANES), lambda i, *_: (i, 0))],
        out_specs=pl.BlockSpec(memory_space=pl.ANY),
        scratch_shapes=[pltpu.VMEM((BM * ROW_TILE, LANES), F32), pltpu.SemaphoreType.DMA,
                        pltpu.SemaphoreType.DMA],
    )
    return pl.pallas_call(
        functools.partial(_dispatch_kernel, n_blocks),
        grid_spec=grid_spec,
        out_shape=jax.ShapeDtypeStruct((n_blocks * BM * ROW_TILE, LANES), F32),
        compiler_params=_cparams(("arbitrary",)),
        name="dispatch",
    )(pad_start, pad_len, n_used, dest, h2t)


def _expert_kernel(nblk_ref, bstart_ref, widx_ref, nu_ref, nv_ref, xs_ref, wg_ref, wu_ref, wd_ref, ys_ref,
                   xbuf, ybuf, wg_b, wu_b, wd_b, xsem, ysem):
    e = pl.program_id(0)
    n_used = nu_ref[0]
    rows = BM * ROW_TILE
    piece = rows // EXPERT_CHUNKS
    ring = EXPERT_RING

    def slot_base(g):
        return pl.multiple_of((g % ring) * rows, rows)

    def pieces(g, hbm_ref, buf, sem, to_hbm, act):
        for c in range(EXPERT_CHUNKS):
            @pl.when(c * (BM // EXPERT_CHUNKS) < nv_ref[g])
            def _(c=c):
                in_hbm = hbm_ref.at[pl.ds(pl.multiple_of(g * rows + c * piece, piece), piece), :]
                in_vmem = buf.at[pl.ds(pl.multiple_of(slot_base(g) + c * piece, piece), piece), :]
                src, dst = (in_vmem, in_hbm) if to_hbm else (in_hbm, in_vmem)
                act(pltpu.make_async_copy(src, dst, sem.at[g % ring]))

    def x_load(g, act):
        pieces(g, xs_ref, xbuf, xsem, False, act)

    def y_store(g, act):
        pieces(g, ys_ref, ybuf, ysem, True, act)

    start = lambda cp: cp.start()
    start_store = lambda cp: cp.start(priority=1)
    wait = lambda cp: cp.wait()

    @pl.when(e == 0)
    def _():
        xbuf[...] = jnp.zeros(xbuf.shape, F32)
        for g0 in range(ring - 1):
            @pl.when(g0 < n_used)
            def _(g0=g0):
                x_load(g0, start)

    wg_b[...] = wg_ref[0].astype(BF16)
    wu_b[...] = wu_ref[0].astype(BF16)
    wd_b[...] = wd_ref[0].astype(BF16)

    def block(b, c):
        g = bstart_ref[e] + b
        x_load(g, wait)

        @pl.when(g + ring - 1 < n_used)
        def _():
            x_load(g + ring - 1, start)

        @pl.when(g >= ring)
        def _():
            y_store(g - ring, wait)

        x = _load_token_rows(xbuf, slot_base(g), BM).astype(BF16)
        a = jnp.dot(x, wg_b[...], preferred_element_type=F32)
        u = jnp.dot(x, wu_b[...], preferred_element_type=F32)
        act = (a * _sigmoid(a) * u).astype(BF16)
        _store_token_rows(ybuf, slot_base(g), jnp.dot(act, wd_b[...], preferred_element_type=F32))
        y_store(g, start_store)
        return c

    lax.fori_loop(0, nblk_ref[e], block, 0)

    @pl.when(e == pl.num_programs(0) - 1)
    def _():
        for back in range(ring, 0, -1):
            @pl.when(n_used >= back)
            def _(back=back):
                y_store(n_used - back, wait)


def _experts(nblk, blk_start, w_idx, n_used, block_nv, xs, wg, wu, wd):
    n_exp, d, de = wg.shape
    rows = BM * ROW_TILE
    wsel = lambda e, nb, bs, wi, nu, nv: (wi[e], 0, 0)
    grid_spec = pltpu.PrefetchScalarGridSpec(
        num_scalar_prefetch=5,
        grid=(n_exp,),
        in_specs=[pl.BlockSpec(memory_space=pl.ANY),
                  pl.BlockSpec((1, d, de), wsel), pl.BlockSpec((1, d, de), wsel),
                  pl.BlockSpec((1, de, d), wsel)],
        out_specs=pl.BlockSpec(memory_space=pl.ANY),
        scratch_shapes=[pltpu.VMEM((EXPERT_RING * rows, LANES), F32), pltpu.VMEM((EXPERT_RING * rows, LANES), F32),
                        pltpu.VMEM((d, de), BF16), pltpu.VMEM((d, de), BF16), pltpu.VMEM((de, d), BF16),
                        pltpu.SemaphoreType.DMA((EXPERT_RING,)), pltpu.SemaphoreType.DMA((EXPERT_RING,))],
    )
    return pl.pallas_call(
        _expert_kernel,
        grid_spec=grid_spec,
        out_shape=jax.ShapeDtypeStruct(xs.shape, xs.dtype),
        input_output_aliases={5: 0},
        compiler_params=_cparams(("arbitrary",)),
        name="experts",
    )(nblk, blk_start, w_idx, n_used, block_nv, xs, wg, wu, wd)


def _combine_kernel(dest_ref, next_dest_ref, w_ref, base_ref, mod_ref, g_ref, b_ref, ys_ref, o_ref, buf, sems):
    i = pl.program_id(0)
    tm = dest_ref.shape[1]
    slab = tm * ROW_TILE
    slot_rows = TOP_K * slab

    def copy(idx_ref, slot, k, j):
        return _row_copy(ys_ref, idx_ref[k, j], buf, slot * (TOP_K * tm) + k * tm + j, sems.at[slot])

    def wait_slot(slot):
        rows = buf.at[pl.ds(slot * slot_rows, slot_rows), :]
        pltpu.make_async_copy(rows, rows, sems.at[slot]).wait()

    @pl.when(i == 0)
    def _():
        def issue(j, c):
            for k in range(TOP_K):
                copy(dest_ref, 0, k, j).start(priority=k % 2)
            return c
        lax.fori_loop(0, tm, issue, 0)

    for slot in range(2):
        @pl.when(i % 2 == slot)
        def _(slot=slot):
            for j in range(tm):
                for k in range(TOP_K):
                    copy(next_dest_ref, 1 - slot, k, j).start(priority=k % 2)
            wait_slot(slot)
            wpad = jnp.concatenate([w_ref[...], jnp.zeros((LANES - TOP_K, tm), F32)], axis=0)
            wcol = wpad.T
            routed = jnp.zeros(base_ref.shape, F32)
            for k in range(TOP_K):
                routed = routed + wcol[:, k:k + 1] * _load_token_rows(buf, slot * slot_rows + k * slab, tm)
            gate_f = mod_ref[0, 5:6, :]
            o_ref[...] = _layer_norm(base_ref[...] + gate_f * routed, g_ref[...], b_ref[...])

            @pl.when(i == pl.num_programs(0) - 1)
            def _():
                wait_slot(1 - slot)


def _combine(dest, wts, base, mod, g, b, ys, seq):
    t, d = base.shape
    tm = TM_COMBINE
    per_seq = seq // tm
    n_steps = t // tm
    col = lambda i: (0, i)
    row = lambda i: (i, 0)
    return pl.pallas_call(
        _combine_kernel,
        grid=(n_steps,),
        in_specs=[pl.BlockSpec((TOP_K, tm), col, memory_space=pltpu.SMEM),
                  pl.BlockSpec((TOP_K, tm), lambda i: (0, jnp.minimum(i + 1, n_steps - 1)),
                               memory_space=pltpu.SMEM),
                  pl.BlockSpec((TOP_K, tm), col), pl.BlockSpec((tm, d), row),
                  pl.BlockSpec((1,) + mod.shape[1:], lambda i: (i // per_seq, 0, 0)),
                  _const_spec(g.shape), _const_spec(b.shape),
                  pl.BlockSpec(memory_space=pl.ANY)],
        out_specs=pl.BlockSpec((tm, d), row),
        out_shape=jax.ShapeDtypeStruct((t, d), F32),
        scratch_shapes=[pltpu.VMEM((2 * TOP_K * tm * ROW_TILE, LANES), F32), pltpu.SemaphoreType.DMA((2,))],
        compiler_params=_cparams(("arbitrary",)),
        name="combine",
    )(dest, dest, wts, base, mod, g, b, ys)


def _rope_tables(seq):
    half = HEAD_DIM // 2
    inv_freq = ROPE_THETA ** (-jnp.arange(half, dtype=F32) / half)
    ang = jnp.arange(seq, dtype=F32)[:, None] * inv_freq[None, :]
    cos = jnp.concatenate([jnp.cos(ang), jnp.cos(ang)], axis=1)
    sin = jnp.concatenate([-jnp.sin(ang), jnp.sin(ang)], axis=1)
    return jnp.tile(cos, (1, N_HEADS)), jnp.tile(sin, (1, N_HEADS))


def _layer(x, c, w_ada, b_ada, w_in, ln_v_g, ln_v_b, w_spatial, b_spatial, w_proj_a, w_proj_b, w_out,
           ln1_g, ln1_b, w_router, router_bias, w_gate_e, w_up_e, w_down_e, w_gate_sh, w_up_sh,
           w_down_sh, ln2_g, ln2_b, alpha):
    bsz, seq, d = x.shape
    assert seq % (max(DILATIONS) * BAND) == 0 and seq % TM_IN == 0 and d == ROW_TILE * LANES
    t = bsz * seq
    aw = N_HEADS * HEAD_DIM
    gwid = ln_v_g.shape[0]
    n_exp = w_router.shape[1]
    xf = x.reshape(t, d)

    mod = _ada(c, w_ada, b_ada).reshape(bsz, 6, d)

    cuts = [0, aw, 2 * aw, 3 * aw, 3 * aw + gwid, 3 * aw + 2 * gwid, 3 * aw + 2 * gwid + d,
            3 * aw + 2 * gwid + 2 * d]
    wq, wk, wv, wu, wvg, wga, wgb = [w_in[:, a:b].astype(BF16) for a, b in zip(cuts[:-1], cuts[1:])]
    cos, sin = _rope_tables(seq)
    bsp = jnp.repeat(b_spatial.T, gwid // GMLP_GROUPS, axis=1)
    qkv, ga, gbyb = _inproj(xf, mod, cos, sin, wq, wk, wv, wu, wvg, wga, wgb,
                            ln_v_g.reshape(1, gwid), ln_v_b.reshape(1, gwid), w_spatial, bsp,
                            w_proj_b.astype(BF16), seq)

    oa = _attention(qkv, bsz, seq)

    h2t, base, e_idx, wts, rank, cnt = _post(
        alpha, oa.reshape(t, aw), ga, gbyb, xf, mod, w_proj_a.astype(BF16), w_out.astype(BF16),
        ln1_g.reshape(1, d), ln1_b.reshape(1, d), w_router.T.astype(BF16), router_bias.reshape(n_exp, 1),
        w_gate_sh.astype(BF16), w_up_sh.astype(BF16), w_down_sh.astype(BF16), seq)

    counts = cnt[:, 0].astype(jnp.int32)
    nblk = ((counts + BM - 1) // BM).astype(jnp.int32)
    blk_end = jnp.cumsum(nblk).astype(jnp.int32)
    blk_start = blk_end - nblk
    n_blocks = (t * TOP_K) // BM + n_exp
    n_used = blk_end[-1:]
    row_start = blk_start * BM
    ids = jnp.arange(n_exp, dtype=jnp.int32)
    seen = jnp.where((ids[None, :] <= ids[:, None]) & (nblk[None, :] > 0), ids[None, :], -1).max(axis=1)
    w_idx = jnp.where(seen >= 0, seen, jnp.where(nblk > 0, ids, n_exp - 1).min()).astype(jnp.int32)

    bidx = jnp.arange(n_blocks, dtype=jnp.int32)[:, None]
    owner = (bidx >= blk_start[None, :]) & (bidx < blk_end[None, :])
    block_nv = jnp.where(owner, jnp.minimum(counts[None, :] - (bidx - blk_start[None, :]) * BM, BM), 0).sum(axis=1)

    dest = _dest(row_start, e_idx, rank)
    xs = _dispatch(row_start + counts, nblk * BM - counts, n_used, dest, h2t, n_blocks)
    ys = _experts(nblk, blk_start, w_idx, n_used, block_nv.astype(jnp.int32), xs, w_gate_e, w_up_e, w_down_e)
    out = _combine(dest, wts, base, mod, ln2_g.reshape(1, d), ln2_b.reshape(1, d), ys, seq)
    return out.reshape(bsz, seq, d)


def kernel(x, c, w_ada, b_ada, w_in, ln_v_g, ln_v_b, w_spatial, b_spatial, w_proj_a, w_proj_b, w_out,
           ln1_g, ln1_b, w_router, router_bias, w_gate_e, w_up_e, w_down_e, w_gate_sh, w_up_sh,
           w_down_sh, ln2_g, ln2_b):
    depth = w_ada.shape[0]
    alpha = (2.0 * depth) ** 0.25
    for l in range(depth):
        x = _layer(x, c, w_ada[l], b_ada[l], w_in[l], ln_v_g[l], ln_v_b[l], w_spatial[l], b_spatial[l],
                   w_proj_a[l], w_proj_b[l], w_out[l], ln1_g[l], ln1_b[l], w_router[l], router_bias[l],
                   w_gate_e[l], w_up_e[l], w_down_e[l], w_gate_sh[l], w_up_sh[l], w_down_sh[l],
                   ln2_g[l], ln2_b[l], alpha)
    return x
```

```python
import functools
import math

import jax
import jax.numpy as jnp
from jax import lax
from jax.experimental import pallas as pl
from jax.experimental.pallas import tpu as pltpu

F32 = jnp.float32
BF16 = jnp.bfloat16

N_HEADS = 8
HEAD_DIM = 64
DILATIONS = (1, 4, 16)
BAND = 128
ROPE_THETA = 10000.0
GMLP_GROUPS = 8
GMLP_CHUNK = 128
N_EXPERT_GROUPS = 8
TOPK_GROUPS = 4
TOP_K = 8
ROUTED_SCALE = 2.5
LN_EPS = 1e-5
LANES = 128
SUBLANES = 8
ROW_TILE = SUBLANES
VMEM_LIMIT = 56 * 1024 * 1024

TM_IN = 512
TM_POST = 512
TM_DISPATCH = 1024
TM_COMBINE = 256
BM = 256
EXPERT_RING = 4
EXPERT_CHUNKS = 4
ATTN_BLOCKS_PER_TRIP = 16


def _cparams(sem):
    return pltpu.CompilerParams(dimension_semantics=sem, vmem_limit_bytes=VMEM_LIMIT)


def _const_spec(shape):
    nd = len(shape)
    return pl.BlockSpec(shape, lambda *_: (0,) * nd)


def _layer_norm(r, g, b):
    mu = jnp.mean(r, axis=-1, keepdims=True)
    d = r - mu
    var = jnp.mean(d * d, axis=-1, keepdims=True)
    return d * lax.rsqrt(var + LN_EPS) * g + b


def _gelu(t):
    return 0.5 * t * (1.0 + lax.erf(t * (1.0 / math.sqrt(2.0))))


def _sigmoid(t):
    return 1.0 / (1.0 + jnp.exp(-t))


def _store_token_rows(ref, first_row, v):
    m = v.shape[0]
    for s in range(ROW_TILE):
        ref[pl.ds(first_row + s, m, stride=ROW_TILE), :] = v[:, s * LANES:(s + 1) * LANES]


def _load_token_rows(ref, first_row, m):
    return jnp.concatenate([ref[pl.ds(first_row + s, m, stride=ROW_TILE), :] for s in range(ROW_TILE)], axis=1)


def _ada_kernel(c_ref, w_ref, b_ref, o_ref):
    c = c_ref[...]
    o_ref[...] = jnp.dot(c * _sigmoid(c), w_ref[...], preferred_element_type=F32) + b_ref[...]


def _ada(c, w, b):
    bsz, d = c.shape
    n = w.shape[1]
    tn = 1024
    return pl.pallas_call(
        _ada_kernel,
        grid=(n // tn,),
        in_specs=[_const_spec((bsz, d)), pl.BlockSpec((d, tn), lambda j: (0, j)),
                  pl.BlockSpec((1, tn), lambda j: (0, j))],
        out_specs=pl.BlockSpec((bsz, tn), lambda j: (0, j)),
        out_shape=jax.ShapeDtypeStruct((bsz, n), F32),
        compiler_params=_cparams(("arbitrary",)),
        name="ada",
    )(c, w, b.reshape(1, n))


def _inproj_kernel(x_ref, mod_ref, cos_ref, sin_ref, wq_ref, wk_ref, wv_ref, wu_ref, wvg_ref,
                   wga_ref, wgb_ref, lng_ref, lnb_ref, wsp_ref, bsp_ref, wpb_ref, *rest):
    n_cm = len(DILATIONS) - 1
    q_refs, k_refs, v_refs = [rest[(1 + n_cm) * a:(1 + n_cm) * (a + 1)] for a in range(3)]
    ga_ref, gbyb_ref, u_scr, vv_scr, gated_scr, q_stage, k_stage, v_stage = rest[3 * (1 + n_cm):]
    tm = x_ref.shape[0]

    def emit(val, refs, stage):
        refs[0][...] = val.astype(BF16)
        for hp in range(val.shape[1] // LANES):
            stage[pl.ds(hp * tm, tm), :] = val[:, hp * LANES:(hp + 1) * LANES]
        for ref, dil in zip(refs[1:], DILATIONS[1:]):
            for hp in range(val.shape[1] // LANES):
                for r in range(dil):
                    ref[0, hp, r, :, :] = stage[pl.ds(hp * tm + r, tm // dil, stride=dil), :].astype(BF16)
    shift = mod_ref[0, 0:1, :]
    scale = mod_ref[0, 1:2, :]
    h = (x_ref[...] * (1.0 + scale) + shift).astype(BF16)

    cos = cos_ref[...]
    sin = sin_ref[...]
    aw = cos.shape[1]
    lane = lax.broadcasted_iota(jnp.int32, (tm, aw), 1)
    first_half = (lane % HEAD_DIM) < (HEAD_DIM // 2)

    def rope(t):
        partner = jnp.where(first_half, pltpu.roll(t, aw - HEAD_DIM // 2, 1),
                            pltpu.roll(t, HEAD_DIM // 2, 1))
        return t * cos + partner * sin

    emit(rope(jnp.dot(h, wq_ref[...], preferred_element_type=F32)) * (HEAD_DIM ** -0.5), q_refs, q_stage)
    emit(rope(jnp.dot(h, wk_ref[...], preferred_element_type=F32)), k_refs, k_stage)
    emit(jnp.dot(h, wv_ref[...], preferred_element_type=F32), v_refs, v_stage)

    u_scr[...] = _gelu(jnp.dot(h, wu_ref[...], preferred_element_type=F32))
    vg = _gelu(jnp.dot(h, wvg_ref[...], preferred_element_type=F32))
    vv_scr[...] = _layer_norm(vg, lng_ref[...], lnb_ref[...]).astype(BF16)

    ti = lax.broadcasted_iota(jnp.int32, (GMLP_CHUNK, GMLP_CHUNK), 0)
    si = lax.broadcasted_iota(jnp.int32, (GMLP_CHUNK, GMLP_CHUNK), 1)
    causal = si <= ti
    gw = u_scr.shape[1] // GMLP_GROUPS
    for g in range(GMLP_GROUPS):
        ws = jnp.where(causal, wsp_ref[g], 0.0).astype(BF16)
        cols = slice(g * gw, (g + 1) * gw)
        for c in range(tm // GMLP_CHUNK):
            rows = slice(c * GMLP_CHUNK, (c + 1) * GMLP_CHUNK)
            sv = jnp.dot(ws, vv_scr[rows, cols], preferred_element_type=F32) + bsp_ref[:, cols]
            gated_scr[rows, cols] = (u_scr[rows, cols] * sv).astype(BF16)

    y_b = jnp.dot(gated_scr[...], wpb_ref[...], preferred_element_type=F32)
    g_b = _sigmoid(jnp.dot(h, wgb_ref[...], preferred_element_type=F32))
    gbyb_ref[...] = (g_b * y_b).astype(BF16)
    ga_ref[...] = _sigmoid(jnp.dot(h, wga_ref[...], preferred_element_type=F32)).astype(BF16)


def _inproj(xf, mod, cos, sin, wq, wk, wv, wu, wvg, wga, wgb, lng, lnb, wsp, bsp, wpb, seq):
    t, d = xf.shape
    aw = wq.shape[1]
    gwid = wu.shape[1]
    tm = TM_IN
    per_seq = seq // tm
    row = lambda i: (i, 0)
    tab = lambda i: (i % per_seq, 0)
    bsz = t // seq
    qkv_specs = [pl.BlockSpec((tm, aw), row)] + [
        pl.BlockSpec((1, aw // LANES, dil, tm // dil, LANES), lambda i: (i // per_seq, 0, 0, i % per_seq, 0))
        for dil in DILATIONS[1:]]
    qkv_shapes = [jax.ShapeDtypeStruct((t, aw), BF16)] + [
        jax.ShapeDtypeStruct((bsz, aw // LANES, dil, seq // dil, LANES), BF16) for dil in DILATIONS[1:]]
    outs = pl.pallas_call(
        _inproj_kernel,
        grid=(t // tm,),
        in_specs=[pl.BlockSpec((tm, d), row),
                  pl.BlockSpec((1,) + mod.shape[1:], lambda i: (i // per_seq, 0, 0)),
                  pl.BlockSpec((tm, aw), tab), pl.BlockSpec((tm, aw), tab),
                  _const_spec(wq.shape), _const_spec(wk.shape), _const_spec(wv.shape),
                  _const_spec(wu.shape), _const_spec(wvg.shape), _const_spec(wga.shape),
                  _const_spec(wgb.shape), _const_spec(lng.shape), _const_spec(lnb.shape),
                  _const_spec(wsp.shape), _const_spec(bsp.shape), _const_spec(wpb.shape)],
        out_specs=qkv_specs * 3 + [pl.BlockSpec((tm, d), row)] * 2,
        out_shape=qkv_shapes * 3 + [jax.ShapeDtypeStruct((t, d), BF16)] * 2,
        scratch_shapes=[pltpu.VMEM((tm, gwid), F32), pltpu.VMEM((tm, gwid), BF16),
                        pltpu.VMEM((tm, gwid), BF16)] + [pltpu.VMEM((tm * aw // LANES, LANES), F32)] * 3,
        compiler_params=_cparams(("arbitrary",)),
        name="inproj",
    )(xf, mod, cos, sin, wq, wk, wv, wu, wvg, wga, wgb, lng, lnb, wsp, bsp, wpb)
    n_lay = len(DILATIONS)
    return [outs[a * n_lay:(a + 1) * n_lay] for a in range(3)], outs[3 * n_lay], outs[3 * n_lay + 1]


def _rows(start, size, stride):
    if stride == 1:
        return pl.ds(start, size)
    return pl.ds(start, size, stride=stride)


def _band_aligned(start):
    return start if isinstance(start, int) else pl.multiple_of(start, BAND)


def _interleaved_loop(count, load, compute, store):
    per_trip = max(u for u in range(1, ATTN_BLOCKS_PER_TRIP + 1) if count % u == 0) if count else 1

    def trip(it, c):
        items = [it * per_trip + u for u in range(per_trip)]
        results = [compute(ops) for ops in [load(i) for i in items]]
        for i, res in zip(items, results):
            store(i, res)
        return c

    if count == per_trip:
        trip(0, 0)
    elif count:
        lax.fori_loop(0, count // per_trip, trip, 0)


def _attn_kernel(*refs):
    npat = len(DILATIONS)
    qkv = [refs[3 * p:3 * p + 3] for p in range(npat)]
    o_ref, acc_scr, max_scr, sum_scr = refs[3 * npat:]
    seq = o_ref.shape[1]

    lane = lax.broadcasted_iota(jnp.int32, (BAND, LANES), 1)
    head0 = lane < HEAD_DIM
    neg = jnp.float32(-1e30)

    qi = lax.broadcasted_iota(jnp.int32, (BAND, 2 * BAND), 0)
    kj = lax.broadcasted_iota(jnp.int32, (BAND, 2 * BAND), 1)

    def load(p, blk, nb):
        q_ref, k_ref, v_ref = qkv[p]
        q_start = blk * BAND
        k_start = jnp.maximum(q_start - BAND, 0)
        off = q_start - k_start
        lowest = jnp.where(blk % nb == 0, off, qi)
        mask = (kj >= lowest) & (kj <= qi + off)
        krows = pl.ds(_band_aligned(k_start), 2 * BAND)
        return (q_ref[0, 0, pl.ds(_band_aligned(q_start), BAND), :], k_ref[0, 0, krows, :],
                v_ref[0, 0, krows, :], mask)

    def attend(ops):
        qb, kb, vb, mask = ops
        zero = jnp.zeros(qb.shape, qb.dtype)
        parts = []
        for hd in range(2):
            qh = jnp.where(head0 if hd == 0 else ~head0, qb, zero)
            s = lax.dot_general(qh, kb, (((1,), (1,)), ((), ())), preferred_element_type=F32)
            s = jnp.where(mask, s, neg)
            m = jnp.max(s, axis=-1, keepdims=True)
            pr = jnp.exp(s - m)
            l = jnp.sum(pr, axis=-1, keepdims=True)
            acc = jnp.dot(pr.astype(BF16), vb, preferred_element_type=F32)
            parts.append((acc, m, l))
        return [jnp.where(head0, a0, a1) for a0, a1 in zip(*parts)]

    def store(p, seq_start, stride, res):
        rows = _rows(seq_start if stride > 1 else _band_aligned(seq_start), BAND, stride)
        acc_scr[p, rows, :] = res[0]
        max_scr[p, rows, :] = res[1]
        sum_scr[p, rows, :] = res[2]

    for p, dil in enumerate(DILATIONS):
        n = seq // dil
        nb = n // BAND

        _interleaved_loop(
            seq // BAND, lambda blk, p=p, nb=nb: load(p, blk, nb), attend,
            lambda blk, res, p=p, dil=dil, nb=nb: store(p, blk // nb + dil * BAND * (blk % nb), dil, res))

    top = jnp.maximum(jnp.maximum(max_scr[0], max_scr[1]), max_scr[2])
    num = jnp.zeros(top.shape, F32)
    den = jnp.zeros(top.shape, F32)
    for p in range(npat):
        w = jnp.exp(max_scr[p] - top)
        num = num + w * acc_scr[p]
        den = den + w * sum_scr[p]
    o_ref[0] = (num / den).astype(BF16)


def _attention(qkv, bsz, seq):
    aw = qkv[0][0].shape[1]
    npat = len(DILATIONS)
    blk = (1, 1, seq, LANES)
    operands, specs = [], []
    for p, dil in enumerate(DILATIONS):
        for layouts in qkv:
            if dil == 1:
                operands.append(layouts[p].reshape(bsz, 1, seq, aw))
                specs.append(pl.BlockSpec(blk, lambda b, hp: (b, 0, 0, hp)))
            else:
                operands.append(layouts[p].reshape(bsz, aw // LANES, seq, LANES))
                specs.append(pl.BlockSpec(blk, lambda b, hp: (b, hp, 0, 0)))
    return pl.pallas_call(
        _attn_kernel,
        grid=(bsz, aw // LANES),
        in_specs=specs,
        out_specs=pl.BlockSpec((1, seq, LANES), lambda b, hp: (b, 0, hp)),
        out_shape=jax.ShapeDtypeStruct((bsz, seq, aw), BF16),
        scratch_shapes=[pltpu.VMEM((npat, seq, LANES), F32)] * 3,
        compiler_params=_cparams(("arbitrary", "arbitrary")),
        name="attn",
    )(*operands)


def _post_kernel(alpha, oa_ref, ga_ref, gbyb_ref, x_ref, mod_ref, wpa_ref, wo_ref, l1g_ref, l1b_ref,
                 wrt_ref, rb_ref, wgs_ref, wus_ref, wds_ref, tri_ref,
                 h2_ref, base_ref, e_ref, w_ref, r_ref, cnt_ref):
    tm = x_ref.shape[0]
    n_exp = wrt_ref.shape[0]
    per_group = n_exp // N_EXPERT_GROUPS
    gate_a = mod_ref[0, 2:3, :]
    shift_f = mod_ref[0, 3:4, :]
    scale_f = mod_ref[0, 4:5, :]
    gate_f = mod_ref[0, 5:6, :]

    y_a = jnp.dot(oa_ref[...], wpa_ref[...], preferred_element_type=F32)
    merged = ga_ref[...].astype(F32) * y_a + gbyb_ref[...].astype(F32)
    mix = jnp.dot(merged.astype(BF16), wo_ref[...], preferred_element_type=F32)
    x1 = _layer_norm(alpha * x_ref[...] + gate_a * mix, l1g_ref[...], l1b_ref[...])
    h2 = x1 * (1.0 + scale_f) + shift_f
    h2b = h2.astype(BF16)

    _store_token_rows(h2_ref, 0, h2)

    a = jnp.dot(h2b, wgs_ref[...], preferred_element_type=F32)
    b = jnp.dot(h2b, wus_ref[...], preferred_element_type=F32)
    shared = jnp.dot((a * _sigmoid(a) * b).astype(BF16), wds_ref[...], preferred_element_type=F32)
    base_ref[...] = alpha * x1 + gate_f * shared

    logits = lax.dot_general(wrt_ref[...], h2b, (((1,), (1,)), ((), ())), preferred_element_type=F32)
    scores = _sigmoid(logits)
    biased = scores + rb_ref[...]
    ninf = jnp.float32(-jnp.inf)

    sub = lax.broadcasted_iota(jnp.int32, (per_group, tm), 0)
    grp_rows = []
    for g in range(N_EXPERT_GROUPS):
        blk = biased[g * per_group:(g + 1) * per_group, :]
        m1 = jnp.max(blk, axis=0, keepdims=True)
        f1 = jnp.min(jnp.where(blk == m1, sub, per_group), axis=0, keepdims=True)
        m2 = jnp.max(jnp.where(sub == f1, ninf, blk), axis=0, keepdims=True)
        grp_rows.append(m1 + m2)
    grp = jnp.concatenate(grp_rows, axis=0)

    gi = lax.broadcasted_iota(jnp.int32, grp.shape, 0)
    chosen = jnp.zeros(grp.shape, jnp.bool_)
    cur = grp
    for _ in range(TOPK_GROUPS):
        m = jnp.max(cur, axis=0, keepdims=True)
        f = jnp.min(jnp.where(cur == m, gi, N_EXPERT_GROUPS), axis=0, keepdims=True)
        hit = gi == f
        chosen = chosen | hit
        cur = jnp.where(hit, ninf, cur)
    emask = jnp.concatenate(
        [jnp.broadcast_to(chosen[g:g + 1, :], (per_group, tm)) for g in range(N_EXPERT_GROUPS)], axis=0)

    ei = lax.broadcasted_iota(jnp.int32, (n_exp, tm), 0)
    cand = jnp.where(emask, biased, ninf)
    hits, ids, wts = [], [], []
    for _ in range(TOP_K):
        m = jnp.max(cand, axis=0, keepdims=True)
        f = jnp.min(jnp.where(cand == m, ei, n_exp), axis=0, keepdims=True)
        hit = ei == f
        wts.append(jnp.sum(jnp.where(hit, scores, 0.0), axis=0, keepdims=True))
        cand = jnp.where(hit, ninf, cand)
        hits.append(hit)
        ids.append(f)
    onehot = hits[0]
    for hit in hits[1:]:
        onehot = onehot | hit
    onehot = jnp.where(onehot, 1.0, 0.0)

    @pl.when(pl.program_id(0) == 0)
    def _():
        cnt_ref[...] = jnp.zeros(cnt_ref.shape, F32)

    before = jnp.dot(onehot.astype(BF16), tri_ref[...], preferred_element_type=F32) + cnt_ref[...]
    ranks = [jnp.sum(jnp.where(hit, before, 0.0), axis=0, keepdims=True) for hit in hits]
    cnt_ref[...] += jnp.sum(onehot, axis=1, keepdims=True)

    denom = wts[0]
    for wk in wts[1:]:
        denom = denom + wk
    e_ref[...] = jnp.concatenate(ids, axis=0)
    w_ref[...] = jnp.concatenate([ROUTED_SCALE * wk / denom for wk in wts], axis=0)
    r_ref[...] = jnp.concatenate(ranks, axis=0).astype(jnp.int32)


def _post(alpha, oa, ga, gbyb, xf, mod, wpa, wo, l1g, l1b, wrt, rb, wgs, wus, wds, seq):
    t, d = xf.shape
    aw = oa.shape[1]
    n_exp = wrt.shape[0]
    tm = TM_POST
    per_seq = seq // tm
    row = lambda i: (i, 0)
    col = lambda i: (0, i)
    tri = (jnp.arange(tm)[:, None] < jnp.arange(tm)[None, :]).astype(BF16)
    return pl.pallas_call(
        functools.partial(_post_kernel, alpha),
        grid=(t // tm,),
        in_specs=[pl.BlockSpec((tm, aw), row), pl.BlockSpec((tm, d), row), pl.BlockSpec((tm, d), row),
                  pl.BlockSpec((tm, d), row),
                  pl.BlockSpec((1,) + mod.shape[1:], lambda i: (i // per_seq, 0, 0)),
                  _const_spec(wpa.shape), _const_spec(wo.shape), _const_spec(l1g.shape),
                  _const_spec(l1b.shape), _const_spec(wrt.shape), _const_spec(rb.shape),
                  _const_spec(wgs.shape), _const_spec(wus.shape), _const_spec(wds.shape),
                  _const_spec(tri.shape)],
        out_specs=[pl.BlockSpec((tm * ROW_TILE, LANES), row), pl.BlockSpec((tm, d), row),
                   pl.BlockSpec((TOP_K, tm), col), pl.BlockSpec((TOP_K, tm), col),
                   pl.BlockSpec((TOP_K, tm), col), _const_spec((n_exp, 1))],
        out_shape=[jax.ShapeDtypeStruct((t * ROW_TILE, LANES), F32), jax.ShapeDtypeStruct((t, d), F32),
                   jax.ShapeDtypeStruct((TOP_K, t), jnp.int32), jax.ShapeDtypeStruct((TOP_K, t), F32),
                   jax.ShapeDtypeStruct((TOP_K, t), jnp.int32), jax.ShapeDtypeStruct((n_exp, 1), F32)],
        compiler_params=_cparams(("arbitrary",)),
        name="post",
    )(oa, ga, gbyb, xf, mod, wpa, wo, l1g, l1b, wrt, rb, wgs, wus, wds, tri)


def _row_copy(src_ref, src_row, dst_ref, dst_row, sem):
    def tile(ref, row):
        start = row * ROW_TILE
        return ref.at[pl.ds(start if isinstance(start, int) else pl.multiple_of(start, ROW_TILE), ROW_TILE), :]

    return pltpu.make_async_copy(tile(src_ref, src_row), tile(dst_ref, dst_row), sem)


def _dest_kernel(rs_ref, e_ref, r_ref, o_ref):
    e = e_ref[...]
    start = lax.fori_loop(0, rs_ref.shape[0], lambda x, acc: jnp.where(e == x, rs_ref[x], acc),
                          jnp.zeros(e.shape, jnp.int32))
    o_ref[...] = start + r_ref[...]


def _dest(row_start, e_idx, rank):
    t = e_idx.shape[1]
    tm = min(t, 2048)
    blk = pl.BlockSpec((TOP_K, tm), lambda i, *_: (0, i))
    return pl.pallas_call(
        _dest_kernel,
        grid_spec=pltpu.PrefetchScalarGridSpec(num_scalar_prefetch=1, grid=(t // tm,), in_specs=[blk, blk],
                                               out_specs=blk),
        out_shape=jax.ShapeDtypeStruct((TOP_K, t), jnp.int32),
        compiler_params=_cparams(("arbitrary",)),
        name="dest",
    )(row_start, e_idx, rank)


def _dispatch_kernel(n_blocks, ps_ref, pl_ref, nu_ref, dest_ref, h2_ref, xs_ref, zbuf, sem, zsem):
    i = pl.program_id(0)
    n_steps = pl.num_programs(0)
    tm = dest_ref.shape[1]
    n_exp = ps_ref.shape[0]
    per_step = -(-n_exp // n_steps)
    zbuf[...] = jnp.zeros(zbuf.shape, F32)

    def zero_fill(act):
        for q in range(per_step):
            e = jnp.minimum(i * per_step + q, n_exp - 1)
            live = i * per_step + q < n_exp
            gap = pl_ref[e]
            for bit in [1 << s for s in range(BM.bit_length() - 1)]:
                @pl.when(live & ((gap & bit) != 0))
                def _(bit=bit, e=e, gap=gap):
                    off = ps_ref[e] + (gap & ~(2 * bit - 1))
                    act(pltpu.make_async_copy(
                        zbuf.at[pl.ds(0, bit * ROW_TILE), :],
                        xs_ref.at[pl.ds(pl.multiple_of(off * ROW_TILE, ROW_TILE), bit * ROW_TILE), :], zsem))
            blk = nu_ref[0] + i * per_step + q
            @pl.when(blk < n_blocks)
            def _(blk=blk):
                rows = BM * ROW_TILE
                act(pltpu.make_async_copy(
                    zbuf, xs_ref.at[pl.ds(pl.multiple_of(blk * rows, rows), rows), :], zsem))

    zero_fill(lambda cp: cp.start(priority=1))

    def issue(j, c):
        for k in range(TOP_K):
            _row_copy(h2_ref, j, xs_ref, dest_ref[k, j], sem).start(priority=k % 2)
        return c

    lax.fori_loop(0, tm, issue, 0)
    n = tm * TOP_K * ROW_TILE
    pltpu.make_async_copy(xs_ref.at[pl.ds(0, n), :], xs_ref.at[pl.ds(0, n), :], sem).wait()
    zero_fill(lambda cp: cp.wait())


def _dispatch(pad_start, pad_len, n_used, dest, h2t, n_blocks):
    t = dest.shape[1]
    tm = TM_DISPATCH
    grid_spec = pltpu.PrefetchScalarGridSpec(
        num_scalar_prefetch=3,
        grid=(t // tm,),
        in_specs=[pl.BlockSpec((TOP_K, tm), lambda i, *_: (0, i), memory_space=pltpu.SMEM),
                  pl.BlockSpec((tm * ROW_TILE, LANES), lambda i, *_: (i, 0))],
        out_specs=pl.BlockSpec(memory_space=pl.ANY),
        scratch_shapes=[pltpu.VMEM((BM * ROW_TILE, LANES), F32), pltpu.SemaphoreType.DMA,
                        pltpu.SemaphoreType.DMA],
    )
    return pl.pallas_call(
        functools.partial(_dispatch_kernel, n_blocks),
        grid_spec=grid_spec,
        out_shape=jax.ShapeDtypeStruct((n_blocks * BM * ROW_TILE, LANES), F32),
        compiler_params=_cparams(("arbitrary",)),
        name="dispatch",
    )(pad_start, pad_len, n_used, dest, h2t)


def _expert_kernel(nblk_ref, bstart_ref, widx_ref, nu_ref, nv_ref, xs_ref, wg_ref, wu_ref, wd_ref, ys_ref,
                   xbuf, ybuf, wg_b, wu_b, wd_b, xsem, ysem):
    e = pl.program_id(0)
    n_used = nu_ref[0]
    rows = BM * ROW_TILE
    piece = rows // EXPERT_CHUNKS
    ring = EXPERT_RING

    def slot_base(g):
        return pl.multiple_of((g % ring) * rows, rows)

    def pieces(g, hbm_ref, buf, sem, to_hbm, act):
        for c in range(EXPERT_CHUNKS):
            @pl.when(c * (BM // EXPERT_CHUNKS) < nv_ref[g])
            def _(c=c):
                in_hbm = hbm_ref.at[pl.ds(pl.multiple_of(g * rows + c * piece, piece), piece), :]
                in_vmem = buf.at[pl.ds(pl.multiple_of(slot_base(g) + c * piece, piece), piece), :]
                src, dst = (in_vmem, in_hbm) if to_hbm else (in_hbm, in_vmem)
                act(pltpu.make_async_copy(src, dst, sem.at[g % ring]))

    def x_load(g, act):
        pieces(g, xs_ref, xbuf, xsem, False, act)

    def y_store(g, act):
        pieces(g, ys_ref, ybuf, ysem, True, act)

    start = lambda cp: cp.start()
    start_store = lambda cp: cp.start(priority=1)
    wait = lambda cp: cp.wait()

    @pl.when(e == 0)
    def _():
        xbuf[...] = jnp.zeros(xbuf.shape, F32)
        for g0 in range(ring - 1):
            @pl.when(g0 < n_used)
            def _(g0=g0):
                x_load(g0, start)

    wg_b[...] = wg_ref[0].astype(BF16)
    wu_b[...] = wu_ref[0].astype(BF16)
    wd_b[...] = wd_ref[0].astype(BF16)

    def block(b, c):
        g = bstart_ref[e] + b
        x_load(g, wait)

        @pl.when(g + ring - 1 < n_used)
        def _():
            x_load(g + ring - 1, start)

        @pl.when(g >= ring)
        def _():
            y_store(g - ring, wait)

        x = _load_token_rows(xbuf, slot_base(g), BM).astype(BF16)
        a = jnp.dot(x, wg_b[...], preferred_element_type=F32)
        u = jnp.dot(x, wu_b[...], preferred_element_type=F32)
        act = (a * _sigmoid(a) * u).astype(BF16)
        _store_token_rows(ybuf, slot_base(g), jnp.dot(act, wd_b[...], preferred_element_type=F32))
        y_store(g, start_store)
        return c

    lax.fori_loop(0, nblk_ref[e], block, 0)

    @pl.when(e == pl.num_programs(0) - 1)
    def _():
        for back in range(ring, 0, -1):
            @pl.when(n_used >= back)
            def _(back=back):
                y_store(n_used - back, wait)


def _experts(nblk, blk_start, w_idx, n_used, block_nv, xs, wg, wu, wd):
    n_exp, d, de = wg.shape
    rows = BM * ROW_TILE
    wsel = lambda e, nb, bs, wi, nu, nv: (wi[e], 0, 0)
    grid_spec = pltpu.PrefetchScalarGridSpec(
        num_scalar_prefetch=5,
        grid=(n_exp,),
        in_specs=[pl.BlockSpec(memory_space=pl.ANY),
                  pl.BlockSpec((1, d, de), wsel), pl.BlockSpec((1, d, de), wsel),
                  pl.BlockSpec((1, de, d), wsel)],
        out_specs=pl.BlockSpec(memory_space=pl.ANY),
        scratch_shapes=[pltpu.VMEM((EXPERT_RING * rows, LANES), F32), pltpu.VMEM((EXPERT_RING * rows, LANES), F32),
                        pltpu.VMEM((d, de), BF16), pltpu.VMEM((d, de), BF16), pltpu.VMEM((de, d), BF16),
                        pltpu.SemaphoreType.DMA((EXPERT_RING,)), pltpu.SemaphoreType.DMA((EXPERT_RING,))],
    )
    return pl.pallas_call(
        _expert_kernel,
        grid_spec=grid_spec,
        out_shape=jax.ShapeDtypeStruct(xs.shape, xs.dtype),
        input_output_aliases={5: 0},
        compiler_params=_cparams(("arbitrary",)),
        name="experts",
    )(nblk, blk_start, w_idx, n_used, block_nv, xs, wg, wu, wd)


def _combine_kernel(dest_ref, next_dest_ref, w_ref, base_ref, mod_ref, g_ref, b_ref, ys_ref, o_ref, buf, sems):
    i = pl.program_id(0)
    tm = dest_ref.shape[1]
    slab = tm * ROW_TILE
    slot_rows = TOP_K * slab

    def copy(idx_ref, slot, k, j):
        return _row_copy(ys_ref, idx_ref[k, j], buf, slot * (TOP_K * tm) + k * tm + j, sems.at[slot])

    def wait_slot(slot):
        rows = buf.at[pl.ds(slot * slot_rows, slot_rows), :]
        pltpu.make_async_copy(rows, rows, sems.at[slot]).wait()

    @pl.when(i == 0)
    def _():
        def issue(j, c):
            for k in range(TOP_K):
                copy(dest_ref, 0, k, j).start(priority=k % 2)
            return c
        lax.fori_loop(0, tm, issue, 0)

    for slot in range(2):
        @pl.when(i % 2 == slot)
        def _(slot=slot):
            wait_slot(slot)
            for j in range(tm):
                for k in range(TOP_K):
                    copy(next_dest_ref, 1 - slot, k, j).start(priority=k % 2)
            wpad = jnp.concatenate([w_ref[...], jnp.zeros((LANES - TOP_K, tm), F32)], axis=0)
            wcol = wpad.T
            routed = jnp.zeros(base_ref.shape, F32)
            for k in range(TOP_K):
                routed = routed + wcol[:, k:k + 1] * _load_token_rows(buf, slot * slot_rows + k * slab, tm)
            gate_f = mod_ref[0, 5:6, :]
            o_ref[...] = _layer_norm(base_ref[...] + gate_f * routed, g_ref[...], b_ref[...])

            @pl.when(i == pl.num_programs(0) - 1)
            def _():
                wait_slot(1 - slot)


def _combine(dest, wts, base, mod, g, b, ys, seq):
    t, d = base.shape
    tm = TM_COMBINE
    per_seq = seq // tm
    n_steps = t // tm
    col = lambda i: (0, i)
    row = lambda i: (i, 0)
    return pl.pallas_call(
        _combine_kernel,
        grid=(n_steps,),
        in_specs=[pl.BlockSpec((TOP_K, tm), col, memory_space=pltpu.SMEM),
                  pl.BlockSpec((TOP_K, tm), lambda i: (0, jnp.minimum(i + 1, n_steps - 1)),
                               memory_space=pltpu.SMEM),
                  pl.BlockSpec((TOP_K, tm), col), pl.BlockSpec((tm, d), row),
                  pl.BlockSpec((1,) + mod.shape[1:], lambda i: (i // per_seq, 0, 0)),
                  _const_spec(g.shape), _const_spec(b.shape),
                  pl.BlockSpec(memory_space=pl.ANY)],
        out_specs=pl.BlockSpec((tm, d), row),
        out_shape=jax.ShapeDtypeStruct((t, d), F32),
        scratch_shapes=[pltpu.VMEM((2 * TOP_K * tm * ROW_TILE, LANES), F32), pltpu.SemaphoreType.DMA((2,))],
        compiler_params=_cparams(("arbitrary",)),
        name="combine",
    )(dest, dest, wts, base, mod, g, b, ys)


def _rope_tables(seq):
    half = HEAD_DIM // 2
    inv_freq = ROPE_THETA ** (-jnp.arange(half, dtype=F32) / half)
    ang = jnp.arange(seq, dtype=F32)[:, None] * inv_freq[None, :]
    cos = jnp.concatenate([jnp.cos(ang), jnp.cos(ang)], axis=1)
    sin = jnp.concatenate([-jnp.sin(ang), jnp.sin(ang)], axis=1)
    return jnp.tile(cos, (1, N_HEADS)), jnp.tile(sin, (1, N_HEADS))


def _layer(x, c, w_ada, b_ada, w_in, ln_v_g, ln_v_b, w_spatial, b_spatial, w_proj_a, w_proj_b, w_out,
           ln1_g, ln1_b, w_router, router_bias, w_gate_e, w_up_e, w_down_e, w_gate_sh, w_up_sh,
           w_down_sh, ln2_g, ln2_b, alpha):
    bsz, seq, d = x.shape
    assert seq % (max(DILATIONS) * BAND) == 0 and seq % TM_IN == 0 and d == ROW_TILE * LANES
    t = bsz * seq
    aw = N_HEADS * HEAD_DIM
    gwid = ln_v_g.shape[0]
    n_exp = w_router.shape[1]
    xf = x.reshape(t, d)

    mod = _ada(c, w_ada, b_ada).reshape(bsz, 6, d)

    cuts = [0, aw, 2 * aw, 3 * aw, 3 * aw + gwid, 3 * aw + 2 * gwid, 3 * aw + 2 * gwid + d,
            3 * aw + 2 * gwid + 2 * d]
    wq, wk, wv, wu, wvg, wga, wgb = [w_in[:, a:b].astype(BF16) for a, b in zip(cuts[:-1], cuts[1:])]
    cos, sin = _rope_tables(seq)
    bsp = jnp.repeat(b_spatial.T, gwid // GMLP_GROUPS, axis=1)
    qkv, ga, gbyb = _inproj(xf, mod, cos, sin, wq, wk, wv, wu, wvg, wga, wgb,
                            ln_v_g.reshape(1, gwid), ln_v_b.reshape(1, gwid), w_spatial, bsp,
                            w_proj_b.astype(BF16), seq)

    oa = _attention(qkv, bsz, seq)

    h2t, base, e_idx, wts, rank, cnt = _post(
        alpha, oa.reshape(t, aw), ga, gbyb, xf, mod, w_proj_a.astype(BF16), w_out.astype(BF16),
        ln1_g.reshape(1, d), ln1_b.reshape(1, d), w_router.T.astype(BF16), router_bias.reshape(n_exp, 1),
        w_gate_sh.astype(BF16), w_up_sh.astype(BF16), w_down_sh.astype(BF16), seq)

    counts = cnt[:, 0].astype(jnp.int32)
    nblk = ((counts + BM - 1) // BM).astype(jnp.int32)
    blk_end = jnp.cumsum(nblk).astype(jnp.int32)
    blk_start = blk_end - nblk
    n_blocks = (t * TOP_K) // BM + n_exp
    n_used = blk_end[-1:]
    row_start = blk_start * BM
    ids = jnp.arange(n_exp, dtype=jnp.int32)
    seen = jnp.where((ids[None, :] <= ids[:, None]) & (nblk[None, :] > 0), ids[None, :], -1).max(axis=1)
    w_idx = jnp.where(seen >= 0, seen, jnp.where(nblk > 0, ids, n_exp - 1).min()).astype(jnp.int32)

    bidx = jnp.arange(n_blocks, dtype=jnp.int32)[:, None]
    owner = (bidx >= blk_start[None, :]) & (bidx < blk_end[None, :])
    block_nv = jnp.where(owner, jnp.minimum(counts[None, :] - (bidx - blk_start[None, :]) * BM, BM), 0).sum(axis=1)

    dest = _dest(row_start, e_idx, rank)
    xs = _dispatch(row_start + counts, nblk * BM - counts, n_used, dest, h2t, n_blocks)
    ys = _experts(nblk, blk_start, w_idx, n_used, block_nv.astype(jnp.int32), xs, w_gate_e, w_up_e, w_down_e)
    out = _combine(dest, wts, base, mod, ln2_g.reshape(1, d), ln2_b.reshape(1, d), ys, seq)
    return out.reshape(bsz, seq, d)


def kernel(x, c, w_ada, b_ada, w_in, ln_v_g, ln_v_b, w_spatial, b_spatial, w_proj_a, w_proj_b, w_out,
           ln1_g, ln1_b, w_router, router_bias, w_gate_e, w_up_e, w_down_e, w_gate_sh, w_up_sh,
           w_down_sh, ln2_g, ln2_b):
    depth = w_ada.shape[0]
    alpha = (2.0 * depth) ** 0.25
    for l in range(depth):
        x = _layer(x, c, w_ada[l], b_ada[l], w_in[l], ln_v_g[l], ln_v_b[l], w_spatial[l], b_spatial[l],
                   w_proj_a[l], w_proj_b[l], w_out[l], ln1_g[l], ln1_b[l], w_router[l], router_bias[l],
                   w_gate_e[l], w_up_e[l], w_down_e[l], w_gate_sh[l], w_up_sh[l], w_down_sh[l],
                   ln2_g[l], ln2_b[l], alpha)
    return x
```
